```python
import math
import jax, jax.numpy as jnp
from jax import lax
import numpy as np

D_MODEL = 1024
BATCH = 16
SEQ = 256
DEPTH = 2
DEC_BATCH = 2
DEC_SEQ = 4096
PAST_LEN = 256

GRID_W = 64
HEAD_DIM = 64
A_HEADS = 6
A_KV = 2
B_HEADS = 6
B_KV = 2
C_HEADS = 4
C_QK_DIM = 32
C_V_DIM = 2 * C_QK_DIM
WINDOW = 128
Q_BLOCK = 128
ROPE_THETA = 10000.0
D_FF = 2816
CONV_W = 3
EPS = 1e-6
NEG_INF = -1e30
IN_SIZES = (A_HEADS * HEAD_DIM, A_KV * HEAD_DIM, A_KV * HEAD_DIM,
            B_HEADS * HEAD_DIM, B_KV * HEAD_DIM, B_KV * HEAD_DIM,
            C_HEADS * 2 * C_QK_DIM, C_HEADS * 2 * C_QK_DIM, C_HEADS * C_V_DIM)
D_IN = 2048
MIX_WIDTH = A_HEADS * HEAD_DIM + B_HEADS * HEAD_DIM + C_HEADS * C_V_DIM

kernel_name = 'hybrid_diffusion_prefix_step'


def rms_norm(x, g):
    xf = x.astype(jnp.float32)
    y = xf * lax.rsqrt(jnp.mean(xf * xf, axis=-1, keepdims=True) + EPS)
    return (y * g.astype(jnp.float32)).astype(x.dtype)


def modulation(cond, w, b):
    m = jax.nn.silu(cond) @ w + b
    return jnp.split(m, 6, axis=-1)


def rope_1d(x, pos):
    half = x.shape[-1] // 2
    inv = ROPE_THETA ** (-jnp.arange(half, dtype=jnp.float32) / half)
    ang = pos.astype(jnp.float32)[:, None] * inv[None, :]
    shape = (1, pos.shape[0]) + (1,) * (x.ndim - 3) + (half,)
    cos = jnp.cos(ang).reshape(shape)
    sin = jnp.sin(ang).reshape(shape)
    xf = x.astype(jnp.float32)
    x1, x2 = xf[..., :half], xf[..., half:]
    return jnp.concatenate([x1 * cos - x2 * sin, x2 * cos + x1 * sin], axis=-1).astype(x.dtype)


def rope_2d(x, rows, cols):
    h = x.shape[-1] // 2
    return jnp.concatenate([rope_1d(x[..., :h], rows), rope_1d(x[..., h:], cols)], axis=-1)


def to_blocks(a):
    B, T = a.shape[0], a.shape[1]
    return a.reshape((B, T // Q_BLOCK, Q_BLOCK) + a.shape[2:]).swapaxes(0, 1)


def from_blocks(a):
    nb, B = a.shape[0], a.shape[1]
    return a.swapaxes(0, 1).reshape((B, nb * Q_BLOCK) + a.shape[3:])


def gqa_dense_blocks(q, k, v, sink=None):
    B, T, H, d = q.shape
    KV = k.shape[2]
    G = H // KV
    scale = d ** -0.5

    def block(qb):
        qg = qb.reshape(B, Q_BLOCK, KV, G, d)
        s = jnp.einsum('bqkgd,bskd->bkgqs', qg, k).astype(jnp.float32) * scale
        if sink is not None:
            sk = jnp.broadcast_to(sink.astype(jnp.float32).reshape(1, KV, G, 1, 1), s.shape[:-1] + (1,))
            p = jax.nn.softmax(jnp.concatenate([s, sk], axis=-1), axis=-1)[..., :-1]
        else:
            p = jax.nn.softmax(s, axis=-1)
        o = jnp.einsum('bkgqs,bskd->bqkgd', p.astype(v.dtype), v)
        return o.reshape(B, Q_BLOCK, H, d)

    return from_blocks(lax.map(block, to_blocks(q)))


def diff_dense_blocks(q1, q2, k1, k2, v, lam):
    scale = q1.shape[-1] ** -0.5

    def block(qs):
        q1b, q2b = qs
        p1 = jax.nn.softmax(jnp.einsum('bqhd,bshd->bhqs', q1b, k1).astype(jnp.float32) * scale, axis=-1)
        p2 = jax.nn.softmax(jnp.einsum('bqhd,bshd->bhqs', q2b, k2).astype(jnp.float32) * scale, axis=-1)
        p = p1 - lam * p2
        return jnp.einsum('bhqs,bshe->bqhe', p.astype(v.dtype), v)

    return from_blocks(lax.map(block, (to_blocks(q1), to_blocks(q2))))


def window_attn_latent(q, k, v, k_ctx, v_ctx, sink):
    B, T, H, d = q.shape
    KV = k.shape[2]
    G = H // KV
    nb = T // Q_BLOCK
    scale = d ** -0.5
    pad = ((0, 0), (WINDOW, WINDOW), (0, 0), (0, 0))
    kp = jnp.pad(k, pad).reshape(B, nb + 2, Q_BLOCK, KV, d)
    vp = jnp.pad(v, pad).reshape(B, nb + 2, Q_BLOCK, KV, d)
    k_band = jnp.concatenate([kp[:, :-2], kp[:, 1:-1], kp[:, 2:]], axis=2)
    v_band = jnp.concatenate([vp[:, :-2], vp[:, 1:-1], vp[:, 2:]], axis=2)
    qg = q.reshape(B, nb, Q_BLOCK, KV, G, d)
    s_band = jnp.einsum('bnqkgd,bnskd->bnkgqs', qg, k_band).astype(jnp.float32) * scale
    qi = jnp.arange(Q_BLOCK)[:, None]
    sj = jnp.arange(3 * Q_BLOCK)[None, :]
    rel = sj - qi - Q_BLOCK
    kpos = jnp.arange(nb)[:, None, None] * Q_BLOCK - Q_BLOCK + sj[None]
    allowed = (jnp.abs(rel) <= WINDOW)[None] & (kpos >= 0) & (kpos < T)
    s_band = jnp.where(allowed[None, :, None, None], s_band, NEG_INF)
    s_ctx = jnp.einsum('bnqkgd,bskd->bnkgqs', qg, k_ctx).astype(jnp.float32) * scale
    sk = jnp.broadcast_to(sink.astype(jnp.float32).reshape(1, 1, KV, G, 1, 1), s_band.shape[:-1] + (1,))
    p = jax.nn.softmax(jnp.concatenate([s_band, s_ctx, sk], axis=-1), axis=-1)
    n_band = 3 * Q_BLOCK
    n_ctx = k_ctx.shape[1]
    o = (jnp.einsum('bnkgqs,bnskd->bnqkgd', p[..., :n_band].astype(v.dtype), v_band)
         + jnp.einsum('bnkgqs,bskd->bnqkgd', p[..., n_band:n_band + n_ctx].astype(v.dtype), v_ctx))
    return o.reshape(B, T, H, d)


def project(h, w_in):
    z = h @ w_in
    B, T = z.shape[0], z.shape[1]
    parts = jnp.split(z, np.cumsum(IN_SIZES)[:-1].tolist(), axis=-1)
    shapes = ((A_HEADS, HEAD_DIM), (A_KV, HEAD_DIM), (A_KV, HEAD_DIM),
              (B_HEADS, HEAD_DIM), (B_KV, HEAD_DIM), (B_KV, HEAD_DIM),
              (C_HEADS, 2, C_QK_DIM), (C_HEADS, 2, C_QK_DIM), (C_HEADS, C_V_DIM))
    return tuple(p.reshape((B, T) + s) for p, s in zip(parts, shapes))


def diff_lambda(lq1, lk1, lq2, lk2, lam_init):
    f = lambda a, b: jnp.exp(jnp.sum(a.astype(jnp.float32) * b.astype(jnp.float32)))
    return f(lq1, lk1) - f(lq2, lk2) + lam_init


def merge_out(oa, ob, oc, g_sub, lam_init, w_out):
    B, T = oa.shape[0], oa.shape[1]
    oc = rms_norm(oc, g_sub) * (1.0 - lam_init)
    o = jnp.concatenate([oa.reshape(B, T, -1), ob.reshape(B, T, -1), oc.reshape(B, T, -1)], axis=-1)
    return o @ w_out


def conv_ffn(h, w_up, conv_w, conv_b, w_down):
    u = h @ w_up
    up = jnp.pad(u, ((0, 0), (1, 1), (0, 0)))
    u = conv_w[0] * up[:, :-2] + conv_w[1] * up[:, 1:-1] + conv_w[2] * up[:, 2:] + conv_b
    a, g = jnp.split(u, 2, axis=-1)
    return (jax.nn.silu(a) * g) @ w_down


def setup_inputs(seed: int = 0) -> dict:
    key = jax.random.key(seed)
    ks = jax.random.split(key, 30)

    def nrm(k, shape, s=1.0):
        return jax.random.normal(k, shape, jnp.float32) * s

    def gain(k, shape):
        return 1.0 + nrm(k, shape, 0.05)

    return {
        'x_prompt': nrm(ks[0], (BATCH, SEQ, D_MODEL)),
        'x_sample': nrm(ks[1], (DEC_BATCH, DEC_SEQ, D_MODEL)),
        'cache_a_k': nrm(ks[2], (DEC_BATCH, DEPTH, PAST_LEN, A_KV, HEAD_DIM)),
        'cache_a_v': nrm(ks[3], (DEC_BATCH, DEPTH, PAST_LEN, A_KV, HEAD_DIM)),
        'cache_b_k': nrm(ks[4], (DEC_BATCH, DEPTH, PAST_LEN, B_KV, HEAD_DIM)),
        'cache_b_v': nrm(ks[5], (DEC_BATCH, DEPTH, PAST_LEN, B_KV, HEAD_DIM)),
        'cache_c_k': nrm(ks[6], (DEC_BATCH, DEPTH, PAST_LEN, C_HEADS, 2 * C_QK_DIM)),
        'cache_c_v': nrm(ks[7], (DEC_BATCH, DEPTH, PAST_LEN, C_HEADS, C_V_DIM)),
        'c': nrm(ks[8], (DEC_BATCH, D_MODEL)),
        'c_ctx': nrm(ks[9], (D_MODEL,)),
        'w_ada': nrm(ks[10], (DEPTH, D_MODEL, 6 * D_MODEL), D_MODEL ** -0.5),
        'b_ada': nrm(ks[11], (DEPTH, 6 * D_MODEL), 0.02),
        'g_norm1': gain(ks[12], (DEPTH, D_MODEL)),
        'g_norm2': gain(ks[13], (DEPTH, D_MODEL)),
        'w_in': nrm(ks[14], (DEPTH, D_MODEL, D_IN), D_MODEL ** -0.5),
        'g_qa': gain(ks[15], (DEPTH, HEAD_DIM)),
        'g_ka': gain(ks[16], (DEPTH, HEAD_DIM)),
        'sink_b': nrm(ks[17], (DEPTH, B_HEADS), 0.5),
        'lam_q1': nrm(ks[18], (DEPTH, C_QK_DIM), 0.1),
        'lam_k1': nrm(ks[19], (DEPTH, C_QK_DIM), 0.1),
        'lam_q2': nrm(ks[20], (DEPTH, C_QK_DIM), 0.1),
        'lam_k2': nrm(ks[21], (DEPTH, C_QK_DIM), 0.1),
        'g_subln': gain(ks[22], (DEPTH, C_V_DIM)),
        'w_out': nrm(ks[23], (DEPTH, MIX_WIDTH, D_MODEL), MIX_WIDTH ** -0.5),
        'w_up': nrm(ks[24], (DEPTH, D_MODEL, 2 * D_FF), D_MODEL ** -0.5),
        'conv_w': nrm(ks[25], (DEPTH, CONV_W, 2 * D_FF), CONV_W ** -0.5),
        'conv_b': nrm(ks[26], (DEPTH, 2 * D_FF), 0.02),
        'w_down': nrm(ks[27], (DEPTH, D_FF, D_MODEL), D_FF ** -0.5),
        'g_final': gain(ks[28], (D_MODEL,)),
    }


def reference(x_prompt, x_sample, cache_a_k, cache_a_v, cache_b_k, cache_b_v, cache_c_k, cache_c_v,
              c, c_ctx, w_ada, b_ada, g_norm1, g_norm2, w_in, g_qa, g_ka, sink_b,
              lam_q1, lam_k1, lam_q2, lam_k2, g_subln, w_out, w_up, conv_w, conv_b, w_down, g_final):
    x = x_prompt
    Bp, Tp = x.shape[0], x.shape[1]
    cond = c_ctx[None, None, :]
    ak_l, av_l, bk_l, bv_l, ck_l, cv_l = [], [], [], [], [], []
    for l in range(DEPTH):
        lam_init = 0.8 - 0.6 * math.exp(-0.3 * l)
        sh1, sc1, gt1, sh2, sc2, gt2 = modulation(cond, w_ada[l], b_ada[l])
        h = rms_norm(x, g_norm1[l]) * (1 + sc1) + sh1
        qa, ka, va, qb, kb, vb, qc, kc, vc = project(h, w_in[l])
        qa = rms_norm(qa, g_qa[l])
        ka = rms_norm(ka, g_ka[l])
        lam = diff_lambda(lam_q1[l], lam_k1[l], lam_q2[l], lam_k2[l], lam_init)
        oa = gqa_dense_blocks(qa, ka, va)
        ob = gqa_dense_blocks(qb, kb, vb, sink_b[l])
        oc = diff_dense_blocks(qc[..., 0, :], qc[..., 1, :], kc[..., 0, :], kc[..., 1, :], vc, lam)
        x = x + gt1 * merge_out(oa, ob, oc, g_subln[l], lam_init, w_out[l])
        h = rms_norm(x, g_norm2[l]) * (1 + sc2) + sh2
        x = x + gt2 * conv_ffn(h, w_up[l], conv_w[l], conv_b[l], w_down[l])
        ak_l.append(ka)
        av_l.append(va)
        bk_l.append(kb)
        bv_l.append(vb)
        ck_l.append(kc.reshape(Bp, Tp, C_HEADS, 2 * C_QK_DIM))
        cv_l.append(vc)
    y_prompt = rms_norm(x, g_final)
    new_a_k = jnp.stack(ak_l, axis=1)
    new_a_v = jnp.stack(av_l, axis=1)
    new_b_k = jnp.stack(bk_l, axis=1)
    new_b_v = jnp.stack(bv_l, axis=1)
    new_c_k = jnp.stack(ck_l, axis=1)
    new_c_v = jnp.stack(cv_l, axis=1)

    x = x_sample
    Bs, Ts = x.shape[0], x.shape[1]
    P = cache_c_k.shape[2]
    t = jnp.arange(Ts)
    rows = t // GRID_W
    cols = t % GRID_W
    cond = c[:, None, :]
    for l in range(DEPTH):
        lam_init = 0.8 - 0.6 * math.exp(-0.3 * l)
        sh1, sc1, gt1, sh2, sc2, gt2 = modulation(cond, w_ada[l], b_ada[l])
        h = rms_norm(x, g_norm1[l]) * (1 + sc1) + sh1
        qa, ka, va, qb, kb, vb, qc, kc, vc = project(h, w_in[l])
        qa = rope_2d(rms_norm(qa, g_qa[l]), rows, cols)
        ka = rope_2d(rms_norm(ka, g_ka[l]), rows, cols)
        qb = rope_2d(qb, rows, cols)
        kb = rope_2d(kb, rows, cols)
        qc = rope_2d(qc, rows, cols)
        kc = rope_2d(kc, rows, cols)
        lam = diff_lambda(lam_q1[l], lam_k1[l], lam_q2[l], lam_k2[l], lam_init)
        ak = jnp.concatenate([cache_a_k[:, l], ka], axis=1)
        av = jnp.concatenate([cache_a_v[:, l], va], axis=1)
        oa = gqa_dense_blocks(qa, ak, av)
        ob = window_attn_latent(qb, kb, vb, cache_b_k[:, l], cache_b_v[:, l], sink_b[l])
        kc_all = jnp.concatenate([cache_c_k[:, l].reshape(Bs, P, C_HEADS, 2, C_QK_DIM), kc], axis=1)
        vc_all = jnp.concatenate([cache_c_v[:, l], vc], axis=1)
        oc = diff_dense_blocks(qc[..., 0, :], qc[..., 1, :], kc_all[..., 0, :], kc_all[..., 1, :], vc_all, lam)
        x = x + gt1 * merge_out(oa, ob, oc, g_subln[l], lam_init, w_out[l])
        h = rms_norm(x, g_norm2[l]) * (1 + sc2) + sh2
        x = x + gt2 * conv_ffn(h, w_up[l], conv_w[l], conv_b[l], w_down[l])
    y_sample = rms_norm(x, g_final)

    return (y_prompt, y_sample, new_a_k, new_a_v, new_b_k, new_b_v, new_c_k, new_c_v)
```

```python
import functools
import math

import numpy as np
import jax
import jax.numpy as jnp
from jax import lax
from jax.experimental import pallas as pl
from jax.experimental.pallas import tpu as pltpu

D_MODEL = 1024
DEPTH = 2
GRID_W = 64
HEAD_DIM = 64
A_HEADS = 6
A_KV = 2
B_HEADS = 6
B_KV = 2
C_HEADS = 4
C_QK_DIM = 32
C_V_DIM = 2 * C_QK_DIM
WINDOW = 128
ROPE_THETA = 10000.0
D_FF = 2816
EPS = 1e-6
NEG = -1e30
LOG2E = math.log2(math.e)

LANES = 128
BF16_ROWS = 16
VMEM_LIMIT = 56 * 1024 * 1024

Z_QA, Z_QB, Z_QC = 0, 384, 768
Z_KA, Z_VA, Z_KB, Z_VB, Z_KC, Z_VC = 1024, 1152, 1280, 1408, 1536, 1792
D_IN = 2048
PAIRED_HEADS = (0, 3, 1, 4, 2, 5)
FFN_CHUNK = 256
HALO = BF16_ROWS

F32 = jnp.float32
BF16 = jnp.bfloat16


def _cparams(sem):
    return pltpu.CompilerParams(dimension_semantics=sem, vmem_limit_bytes=VMEM_LIMIT)


def _const_spec(shape):
    nd = len(shape)
    return pl.BlockSpec(shape, lambda *_: (0,) * nd)


def _mod_kernel(c_ref, w_ref, b_ref, o_ref):
    cond = c_ref[...]
    a = cond / (1.0 + jnp.exp(-cond))
    o_ref[0] = jnp.dot(a.astype(BF16), w_ref[0].astype(BF16), preferred_element_type=F32) + b_ref[0]


def _modulation(conds, w_ada, b_ada):
    nb = 1536
    n_out = w_ada.shape[-1]
    return pl.pallas_call(
        _mod_kernel,
        grid=(DEPTH, n_out // nb),
        in_specs=[
            pl.BlockSpec((8, D_MODEL), lambda l, j: (0, 0)),
            pl.BlockSpec((1, D_MODEL, nb), lambda l, j: (l, 0, j)),
            pl.BlockSpec((1, 1, nb), lambda l, j: (l, 0, j)),
        ],
        out_specs=pl.BlockSpec((1, 8, nb), lambda l, j: (l, 0, j)),
        out_shape=jax.ShapeDtypeStruct((DEPTH, 8, n_out), F32),
        compiler_params=_cparams(("arbitrary", "arbitrary")),
        name="modulation",
    )(conds, w_ada, b_ada.reshape(DEPTH, 1, n_out))


def _rms(x, g):
    ms = jnp.mean(x * x, axis=-1, keepdims=True)
    return x * lax.rsqrt(ms + EPS) * g


def _head_rms(x, g, lo):
    ss = x * x
    s_lo = jnp.sum(jnp.where(lo, ss, 0.0), axis=-1, keepdims=True)
    s_hi = jnp.sum(jnp.where(lo, 0.0, ss), axis=-1, keepdims=True)
    inv = jnp.where(lo, lax.rsqrt(s_lo * (1.0 / HEAD_DIM) + EPS), lax.rsqrt(s_hi * (1.0 / HEAD_DIM) + EPS))
    return x * inv * g


def _rope(x, cos, sin, chunk, first):
    sw = jnp.where(first, pltpu.roll(x, LANES - chunk, 1), pltpu.roll(x, chunk, 1))
    return x * cos + sw * sin


def _softmax_init(m_ref, l_ref, acc_ref, rows):
    m_ref[0:rows] = jnp.full((rows, LANES), NEG, F32)
    l_ref[0:rows] = jnp.zeros((rows, LANES), F32)
    acc_ref[0:rows] = jnp.zeros((rows, LANES), F32)


def _softmax_update(q, k, v, m_ref, l_ref, acc_ref, rows, mask=None):
    s = lax.dot_general(q, k, (((1,), (1,)), ((), ())), preferred_element_type=F32)
    if mask is not None:
        s = jnp.where(mask, s, NEG)
    m_prev = m_ref[0:rows]
    m_new = jnp.maximum(m_prev, jnp.max(s, axis=1, keepdims=True))
    alpha = jnp.exp2(m_prev - m_new)
    p = jnp.exp2(s - m_new[:, 0:1])
    l_ref[0:rows] = alpha * l_ref[0:rows] + jnp.sum(p, axis=1, keepdims=True)
    acc_ref[0:rows] = alpha * acc_ref[0:rows] + jnp.dot(p.astype(BF16), v, preferred_element_type=F32)
    m_ref[0:rows] = m_new


def _softmax_result(l_ref, acc_ref, rows):
    return acc_ref[0:rows] * (1.0 / l_ref[0:rows])


def _gqa_queries(q3, lo):
    slabs = [q3[:, LANES * s:LANES * (s + 1)] for s in range(3)]
    zero = jnp.zeros_like(slabs[0])
    parts = [jnp.where(lo, s, zero) for s in slabs] + [jnp.where(lo, zero, s) for s in slabs]
    return jnp.concatenate(parts, axis=0)


def _gqa_outputs(o, rows, lo):
    return [jnp.where(lo, o[s * rows:(s + 1) * rows], o[(3 + s) * rows:(4 + s) * rows]) for s in range(3)]


def _diff_queries(q, lane):
    zero = jnp.zeros_like(q)
    parts = []
    for j in range(4):
        sel = (lane >= C_QK_DIM * j) & (lane < C_QK_DIM * (j + 1))
        parts.append(jnp.where(sel, q, zero))
    return jnp.concatenate(parts, axis=0)


def _diff_lambda(lam_ref, lam_init):
    f = lambda a, b: jnp.exp(jnp.sum(a * b, axis=-1, keepdims=True))
    return f(lam_ref[0:1], lam_ref[1:2]) - f(lam_ref[2:3], lam_ref[3:4]) + lam_init


def _diff_output(o, rows, lam, gsub, lam_init, lo):
    o_even = o[0:rows] - lam * o[rows:2 * rows]
    o_odd = o[2 * rows:3 * rows] - lam * o[3 * rows:4 * rows]
    oc = jnp.where(lo, o_even, o_odd)
    return _head_rms(oc, gsub, lo) * (1.0 - lam_init)


def _sink_init(sink_ref, m_ref, l_ref, acc_ref, rows):
    for h in range(B_HEADS):
        m_ref[h * rows:(h + 1) * rows] = jnp.full((rows, LANES), sink_ref[h] * LOG2E, F32)
    l_ref[0:B_HEADS * rows] = jnp.ones((B_HEADS * rows, LANES), F32)
    acc_ref[0:B_HEADS * rows] = jnp.zeros((B_HEADS * rows, LANES), F32)


def _in_proj_kernel(*refs, use_rope, emit_cache):
    it = iter(refs)
    x_ref, mod_ref, gn_ref, w_ref, gq_ref, gk_ref = (next(it) for _ in range(6))
    if use_rope:
        cos64, sin64, cos32, sin32 = (next(it)[...] for _ in range(4))
    z_ref = next(it)
    cache_ref = next(it) if emit_cache else None

    x = x_ref[...]
    mod = mod_ref[0]
    h = _rms(x, gn_ref[...]) * (1.0 + mod[1:2]) + mod[0:1]
    z = jnp.dot(h.astype(BF16), w_ref[...], preferred_element_type=F32)

    rows = x.shape[0]
    lane = lax.broadcasted_iota(jnp.int32, (rows, LANES), 1)
    lo = lane < HEAD_DIM
    first16 = (lane & 31) < 16
    first8 = (lane & 15) < 8
    gq = gq_ref[...]
    gk = gk_ref[...]
    q_scale = HEAD_DIM ** -0.5 * LOG2E
    qc_scale = C_QK_DIM ** -0.5 * LOG2E

    def rope64(v):
        return _rope(v, cos64, sin64, 16, first16) if use_rope else v

    def rope32(v):
        return _rope(v, cos32, sin32, 8, first8) if use_rope else v

    def slab(off):
        return z[:, off:off + LANES]

    def put(off, v):
        z_ref[:, off:off + LANES] = v.astype(BF16)

    def put_cache(off, v):
        if emit_cache:
            cache_ref[:, off - Z_KA:off - Z_KA + LANES] = v

    for s in range(3):
        put(Z_QA + LANES * s, rope64(_head_rms(slab(Z_QA + LANES * s), gq, lo)) * q_scale)
        put(Z_QB + LANES * s, rope64(slab(Z_QB + LANES * s)) * q_scale)
    ka = _head_rms(slab(Z_KA), gk, lo)
    put_cache(Z_KA, ka)
    put(Z_KA, rope64(ka))
    kb = slab(Z_KB)
    put_cache(Z_KB, kb)
    put(Z_KB, rope64(kb))
    for off in (Z_VA, Z_VB, Z_VC, Z_VC + LANES):
        put_cache(off, slab(off))
        put(off, slab(off))
    for s in range(2):
        put(Z_QC + LANES * s, rope32(slab(Z_QC + LANES * s)) * qc_scale)
        kc = slab(Z_KC + LANES * s)
        put_cache(Z_KC + LANES * s, kc)
        put(Z_KC + LANES * s, rope32(kc))


def _in_proj(x, mod_l, gn, w_in_p, gq, gk, rope_tabs, *, tm, tiles_per_cond, tiles_per_seq, emit_cache):
    t = x.shape[0]
    use_rope = rope_tabs is not None
    in_specs = [
        pl.BlockSpec((tm, D_MODEL), lambda i: (i, 0)),
        pl.BlockSpec((1, 6, D_MODEL), lambda i: (i // tiles_per_cond, 0, 0)),
        _const_spec((1, D_MODEL)),
        _const_spec((D_MODEL, D_IN)),
        _const_spec((1, LANES)),
        _const_spec((1, LANES)),
    ]
    args = [x, mod_l, gn, w_in_p, gq, gk]
    if use_rope:
        in_specs += [pl.BlockSpec((tm, LANES), lambda i: (i % tiles_per_seq, 0))] * 4
        args += list(rope_tabs)
    out_shape = [jax.ShapeDtypeStruct((t, D_IN), BF16)]
    out_specs = [pl.BlockSpec((tm, D_IN), lambda i: (i, 0))]
    if emit_cache:
        out_shape.append(jax.ShapeDtypeStruct((t, D_IN - Z_KA), F32))
        out_specs.append(pl.BlockSpec((tm, D_IN - Z_KA), lambda i: (i, 0)))
    return pl.pallas_call(
        functools.partial(_in_proj_kernel, use_rope=use_rope, emit_cache=emit_cache),
        grid=(t // tm,),
        in_specs=in_specs,
        out_specs=out_specs,
        out_shape=out_shape,
        compiler_params=_cparams(("arbitrary",)),
        name="in_proj",
    )(*args)


def _attn_ctx_kernel(sink_ref, lam_ref, gsub_ref, z_ref, o_ref, m_ref, l_ref, acc_ref, *, lam_init):
    rows = z_ref.shape[0]
    lane = lax.broadcasted_iota(jnp.int32, (rows, LANES), 1)
    lo = lane < HEAD_DIM
    refs = (m_ref, l_ref, acc_ref)

    def piece(off, width=LANES):
        return z_ref[:, off:off + width]

    n = A_HEADS * rows
    _softmax_init(*refs, n)
    _softmax_update(_gqa_queries(piece(Z_QA, 384), lo), piece(Z_KA), piece(Z_VA), *refs, n)
    for s, o in enumerate(_gqa_outputs(_softmax_result(l_ref, acc_ref, n), rows, lo)):
        o_ref[:, LANES * s:LANES * (s + 1)] = o.astype(BF16)

    _sink_init(sink_ref, *refs, rows)
    _softmax_update(_gqa_queries(piece(Z_QB, 384), lo), piece(Z_KB), piece(Z_VB), *refs, n)
    for s, o in enumerate(_gqa_outputs(_softmax_result(l_ref, acc_ref, n), rows, lo)):
        o_ref[:, 384 + LANES * s:384 + LANES * (s + 1)] = o.astype(BF16)

    lam = _diff_lambda(lam_ref, lam_init)
    n = 4 * rows
    for s in range(2):
        _softmax_init(*refs, n)
        _softmax_update(_diff_queries(piece(Z_QC + LANES * s), lane), piece(Z_KC + LANES * s),
                        piece(Z_VC + LANES * s), *refs, n)
        oc = _diff_output(_softmax_result(l_ref, acc_ref, n), rows, lam, gsub_ref[...], lam_init, lo)
        o_ref[:, 768 + LANES * s:768 + LANES * (s + 1)] = oc.astype(BF16)


def _attn_ctx(z, sink, lam_par, gsub, *, seq, lam_init):
    t = z.shape[0]
    return pl.pallas_call(
        functools.partial(_attn_ctx_kernel, lam_init=lam_init),
        grid=(t // seq,),
        in_specs=[
            pl.BlockSpec(memory_space=pltpu.SMEM),
            _const_spec((8, LANES)),
            _const_spec((1, LANES)),
            pl.BlockSpec((seq, D_IN), lambda b: (b, 0)),
        ],
        out_specs=pl.BlockSpec((seq, D_MODEL), lambda b: (b, 0)),
        out_shape=jax.ShapeDtypeStruct((t, D_MODEL), BF16),
        scratch_shapes=[pltpu.VMEM((A_HEADS * seq, LANES), F32)] * 3,
        compiler_params=_cparams(("arbitrary",)),
        name="attn_ctx",
    )(sink, lam_par, gsub, z)


def _attn_a_kernel(q_ref, kc_ref, vc_ref, k_ref, v_ref, o_ref, q_scr, m_ref, l_ref, acc_ref, *, tk):
    rows = q_ref.shape[0]
    n = A_HEADS * rows
    lo = lax.broadcasted_iota(jnp.int32, (rows, LANES), 1) < HEAD_DIM
    refs = (m_ref, l_ref, acc_ref)
    q_scr[...] = _gqa_queries(q_ref[...], lo)
    _softmax_init(*refs, n)
    _softmax_update(q_scr[...], kc_ref[0], vc_ref[0], *refs, n)

    def body(j, carry):
        off = pl.multiple_of(j * tk, tk)
        _softmax_update(q_scr[...], k_ref[pl.ds(off, tk), :], v_ref[pl.ds(off, tk), :], *refs, n)
        return carry

    lax.fori_loop(0, k_ref.shape[0] // tk, body, 0)
    for s, o in enumerate(_gqa_outputs(_softmax_result(l_ref, acc_ref, n), rows, lo)):
        o_ref[:, LANES * s:LANES * (s + 1)] = o.astype(BF16)


def _attn_a(z, k_ctx, v_ctx, *, seq, tq, tk):
    t = z.shape[0]
    nq = seq // tq
    n_ctx = k_ctx.shape[1]
    return pl.pallas_call(
        functools.partial(_attn_a_kernel, tk=tk),
        grid=(t // seq, nq),
        in_specs=[
            pl.BlockSpec((tq, 384), lambda b, i: (b * nq + i, Z_QA // 384)),
            pl.BlockSpec((1, n_ctx, LANES), lambda b, i: (b, 0, 0)),
            pl.BlockSpec((1, n_ctx, LANES), lambda b, i: (b, 0, 0)),
            pl.BlockSpec((seq, LANES), lambda b, i: (b, Z_KA // LANES)),
            pl.BlockSpec((seq, LANES), lambda b, i: (b, Z_VA // LANES)),
        ],
        out_specs=pl.BlockSpec((tq, 384), lambda b, i: (b * nq + i, 0)),
        out_shape=jax.ShapeDtypeStruct((t, D_MODEL), BF16),
        scratch_shapes=[pltpu.VMEM((A_HEADS * tq, LANES), BF16)] + [pltpu.VMEM((A_HEADS * tq, LANES), F32)] * 3,
        compiler_params=_cparams(("arbitrary", "arbitrary")),
        name="attn_a",
    )(z, k_ctx, v_ctx, z, z)


def _attn_b_kernel(sink_ref, q_ref, kc_ref, vc_ref, k_ref, v_ref, o_in_ref, o_ref, m_ref, l_ref, acc_ref):
    del o_in_ref
    rows = q_ref.shape[0]
    seq = k_ref.shape[0]
    n = B_HEADS * rows
    band = rows + 2 * WINDOW
    i = pl.program_id(1)
    lo = lax.broadcasted_iota(jnp.int32, (rows, LANES), 1) < HEAD_DIM
    refs = (m_ref, l_ref, acc_ref)
    q = _gqa_queries(q_ref[...], lo)
    _sink_init(sink_ref, *refs, rows)
    _softmax_update(q, kc_ref[0], vc_ref[0], *refs, n)
    start = pl.multiple_of(jnp.clip(i * rows - WINDOW, 0, seq - band), LANES)
    qpos = i * rows + (lax.broadcasted_iota(jnp.int32, (n, band), 0) & (rows - 1))
    kpos = start + lax.broadcasted_iota(jnp.int32, (n, band), 1)
    mask = jnp.abs(kpos - qpos) <= WINDOW
    _softmax_update(q, k_ref[pl.ds(start, band), :], v_ref[pl.ds(start, band), :], *refs, n, mask=mask)
    for s, o in enumerate(_gqa_outputs(_softmax_result(l_ref, acc_ref, n), rows, lo)):
        o_ref[:, LANES * s:LANES * (s + 1)] = o.astype(BF16)


def _attn_b(z, k_ctx, v_ctx, sink, o_buf, *, seq, tq):
    t = z.shape[0]
    nq = seq // tq
    n_ctx = k_ctx.shape[1]
    return pl.pallas_call(
        _attn_b_kernel,
        grid=(t // seq, nq),
        in_specs=[
            pl.BlockSpec(memory_space=pltpu.SMEM),
            pl.BlockSpec((tq, 384), lambda b, i: (b * nq + i, Z_QB // 384)),
            pl.BlockSpec((1, n_ctx, LANES), lambda b, i: (b, 0, 0)),
            pl.BlockSpec((1, n_ctx, LANES), lambda b, i: (b, 0, 0)),
            pl.BlockSpec((seq, LANES), lambda b, i: (b, Z_KB // LANES)),
            pl.BlockSpec((seq, LANES), lambda b, i: (b, Z_VB // LANES)),
            pl.BlockSpec(memory_space=pl.ANY),
        ],
        out_specs=pl.BlockSpec((tq, 384), lambda b, i: (b * nq + i, 1)),
        out_shape=jax.ShapeDtypeStruct((t, D_MODEL), BF16),
        input_output_aliases={6: 0},
        scratch_shapes=[pltpu.VMEM((B_HEADS * tq, LANES), F32)] * 3,
        compiler_params=_cparams(("arbitrary", "arbitrary")),
        name="attn_b",
    )(sink, z, k_ctx, v_ctx, z, z, o_buf)


def _attn_c_kernel(lam_ref, gsub_ref, q_ref, kc_ref, vc_ref, k_ref, v_ref, o_in_ref, o_ref,
                   q_scr, m_ref, l_ref, acc_ref, *, tk, lam_init):
    del o_in_ref
    rows = q_ref.shape[0]
    n = 4 * rows
    lane = lax.broadcasted_iota(jnp.int32, (rows, LANES), 1)
    lo = lane < HEAD_DIM
    refs = (m_ref, l_ref, acc_ref)
    q_scr[...] = _diff_queries(q_ref[...], lane)
    _softmax_init(*refs, n)
    _softmax_update(q_scr[...], kc_ref[0], vc_ref[0], *refs, n)

    def body(j, carry):
        off = pl.multiple_of(j * tk, tk)
        _softmax_update(q_scr[...], k_ref[pl.ds(off, tk), :], v_ref[pl.ds(off, tk), :], *refs, n)
        return carry

    lax.fori_loop(0, k_ref.shape[0] // tk, body, 0)
    lam = _diff_lambda(lam_ref, lam_init)
    oc = _diff_output(_softmax_result(l_ref, acc_ref, n), rows, lam, gsub_ref[...], lam_init, lo)
    o_ref[...] = oc.astype(BF16)


def _attn_c(z, k_ctx, v_ctx, lam_par, gsub, o_buf, *, seq, tq, tk, lam_init):
    t = z.shape[0]
    nq = seq // tq
    n_ctx = k_ctx.shape[1]
    return pl.pallas_call(
        functools.partial(_attn_c_kernel, tk=tk, lam_init=lam_init),
        grid=(t // seq, 2, nq),
        in_specs=[
            _const_spec((8, LANES)),
            _const_spec((1, LANES)),
            pl.BlockSpec((tq, LANES), lambda b, s, i: (b * nq + i, Z_QC // LANES + s)),
            pl.BlockSpec((1, n_ctx, LANES), lambda b, s, i: (b, 0, s)),
            pl.BlockSpec((1, n_ctx, LANES), lambda b, s, i: (b, 0, s)),
            pl.BlockSpec((seq, LANES), lambda b, s, i: (b, Z_KC // LANES + s)),
            pl.BlockSpec((seq, LANES), lambda b, s, i: (b, Z_VC // LANES + s)),
            pl.BlockSpec(memory_space=pl.ANY),
        ],
        out_specs=pl.BlockSpec((tq, LANES), lambda b, s, i: (b * nq + i, 768 // LANES + s)),
        out_shape=jax.ShapeDtypeStruct((t, D_MODEL), BF16),
        input_output_aliases={7: 0},
        scratch_shapes=[pltpu.VMEM((4 * tq, LANES), BF16)] + [pltpu.VMEM((4 * tq, LANES), F32)] * 3,
        compiler_params=_cparams(("arbitrary", "arbitrary", "arbitrary")),
        name="attn_c",
    )(lam_par, gsub, z, k_ctx, v_ctx, z, z, o_buf)


def _post_kernel(*refs, halo, tiles_per_seq, final):
    it = iter(refs)
    x_ref, o_ref = next(it), next(it)
    if halo:
        xp_ref, xn_ref, op_ref, on_ref = (next(it) for _ in range(4))
    (mod_ref, gn_ref, wo_ref, wu_ref, cw_ref, cb_ref, wd_ref, gf_ref, out_ref) = (next(it) for _ in range(9))

    tm = x_ref.shape[0]
    if halo:
        x = jnp.concatenate([xp_ref[...], x_ref[...], xn_ref[...]], axis=0)
        o = jnp.concatenate([op_ref[...], o_ref[...], on_ref[...]], axis=0)
    else:
        x, o = x_ref[...], o_ref[...]
    ext = x.shape[0]
    mod = mod_ref[0]
    x1 = x + mod[2:3] * jnp.dot(o, wo_ref[...], preferred_element_type=F32)
    h = _rms(x1, gn_ref[...]) * (1.0 + mod[4:5]) + mod[3:4]
    row = lax.broadcasted_iota(jnp.int32, (ext, 1), 0)
    if halo:
        t_in_seq = pl.program_id(0) % tiles_per_seq
        keep = ((row >= halo) | (t_in_seq > 0)) & ((row < halo + tm) | (t_in_seq < tiles_per_seq - 1))
        h = jnp.where(keep, h, 0.0)
    h = h.astype(BF16)

    def conv(u, c0):
        cw = cw_ref[:, c0:c0 + FFN_CHUNK]
        up = pltpu.roll(u, 1, 0)
        dn = pltpu.roll(u, ext - 1, 0)
        if not halo:
            up = jnp.where(row == 0, 0.0, up)
            dn = jnp.where(row == ext - 1, 0.0, dn)
        v = cw[0:1] * up + cw[1:2] * u + cw[2:3] * dn + cb_ref[:, c0:c0 + FFN_CHUNK]
        return v[halo:halo + tm]

    acc = jnp.zeros((tm, D_MODEL), F32)
    for c in range(D_FF // FFN_CHUNK):
        ca, cg = c * FFN_CHUNK, D_FF + c * FFN_CHUNK
        a = conv(jnp.dot(h, wu_ref[:, ca:ca + FFN_CHUNK], preferred_element_type=F32), ca)
        g = conv(jnp.dot(h, wu_ref[:, cg:cg + FFN_CHUNK], preferred_element_type=F32), cg)
        act = a / (1.0 + jnp.exp(-a)) * g
        acc = acc + jnp.dot(act.astype(BF16), wd_ref[ca:ca + FFN_CHUNK, :], preferred_element_type=F32)
    x2 = x1[halo:halo + tm] + mod[5:6] * acc
    if final:
        x2 = _rms(x2, gf_ref[...])
    out_ref[...] = x2


def _post(x, o, mod_l, gn2, wo, wu, cw, cb, wd, gf, *, tm, tiles_per_cond, tiles_per_seq, final):
    t = x.shape[0]
    halo = HALO if tiles_per_seq > 1 else 0
    tile = lambda i: (i, 0)
    in_specs = [pl.BlockSpec((tm, D_MODEL), tile), pl.BlockSpec((tm, D_MODEL), tile)]
    args = [x, o]
    if halo:
        per = tm // halo
        prev = lambda i: (jnp.maximum(i * per - 1, 0), 0)
        nxt = lambda i: (jnp.minimum((i + 1) * per, t // halo - 1), 0)
        in_specs += [pl.BlockSpec((halo, D_MODEL), prev), pl.BlockSpec((halo, D_MODEL), nxt)] * 2
        args += [x, x, o, o]
    in_specs += [
        pl.BlockSpec((1, 6, D_MODEL), lambda i: (i // tiles_per_cond, 0, 0)),
        _const_spec((1, D_MODEL)),
        _const_spec((D_MODEL, D_MODEL)),
        _const_spec((D_MODEL, 2 * D_FF)),
        _const_spec((3, 2 * D_FF)),
        _const_spec((1, 2 * D_FF)),
        _const_spec((D_FF, D_MODEL)),
        _const_spec((1, D_MODEL)),
    ]
    args += [mod_l, gn2, wo, wu, cw, cb, wd, gf]
    return pl.pallas_call(
        functools.partial(_post_kernel, halo=halo, tiles_per_seq=tiles_per_seq, final=final),
        grid=(t // tm,),
        in_specs=in_specs,
        out_specs=pl.BlockSpec((tm, D_MODEL), tile),
        out_shape=jax.ShapeDtypeStruct((t, D_MODEL), F32),
        compiler_params=_cparams(("arbitrary",)),
        name="post",
    )(*args)


def _rope_tables(seq):
    t = jnp.arange(seq)
    rows = (t // GRID_W).astype(F32)[:, None]
    cols = (t % GRID_W).astype(F32)[:, None]

    def tab(half, reps):
        inv = ROPE_THETA ** (-jnp.arange(half, dtype=F32) / half)
        ar, ac = rows * inv[None, :], cols * inv[None, :]
        cos = jnp.concatenate([jnp.cos(ar), jnp.cos(ar), jnp.cos(ac), jnp.cos(ac)], axis=-1)
        sin = jnp.concatenate([-jnp.sin(ar), jnp.sin(ar), -jnp.sin(ac), jnp.sin(ac)], axis=-1)
        return jnp.tile(cos, (1, reps)), jnp.tile(sin, (1, reps))

    cos64, sin64 = tab(HEAD_DIM // 4, LANES // HEAD_DIM)
    cos32, sin32 = tab(C_QK_DIM // 4, LANES // C_QK_DIM)
    return cos64, sin64, cos32, sin32


def _w_in_perm():
    sizes = (A_HEADS * HEAD_DIM, A_KV * HEAD_DIM, A_KV * HEAD_DIM, B_HEADS * HEAD_DIM, B_KV * HEAD_DIM,
             B_KV * HEAD_DIM, C_HEADS * 2 * C_QK_DIM, C_HEADS * 2 * C_QK_DIM, C_HEADS * C_V_DIM)
    offs = np.concatenate([[0], np.cumsum(sizes)])
    qa, ka, va, qb, kb, vb, qc, kc, vc = (np.arange(offs[j], offs[j + 1]) for j in range(9))
    pair = np.concatenate([np.arange(h * HEAD_DIM, (h + 1) * HEAD_DIM) for h in PAIRED_HEADS])
    return np.concatenate([qa[pair], qb[pair], qc, ka, va, kb, vb, kc, vc])


def _w_out_perm():
    pair = np.concatenate([np.arange(h * HEAD_DIM, (h + 1) * HEAD_DIM) for h in PAIRED_HEADS])
    return np.concatenate([pair, A_HEADS * HEAD_DIM + pair, np.arange(768, D_MODEL)])


def kernel(x_prompt, x_sample, cache_a_k, cache_a_v, cache_b_k, cache_b_v, cache_c_k, cache_c_v, c, c_ctx, w_ada, b_ada, g_norm1, g_norm2, w_in, g_qa, g_ka, sink_b, lam_q1, lam_k1, lam_q2, lam_k2, g_subln, w_out, w_up, conv_w, conv_b, w_down, g_final):
    n_ctx_req, ctx_len, _ = x_prompt.shape
    n_lat_req, lat_len, _ = x_sample.shape
    past = cache_a_k.shape[2]

    conds = jnp.zeros((8, D_MODEL), F32).at[0].set(c_ctx).at[1:1 + n_lat_req].set(c)
    mod = _modulation(conds, w_ada, b_ada).reshape(DEPTH, 8, 6, D_MODEL)

    w_in_p = w_in[:, :, _w_in_perm()].astype(BF16)
    w_out_p = w_out[:, _w_out_perm(), :].astype(BF16)
    w_up_b = w_up.astype(BF16)
    w_down_b = w_down.astype(BF16)
    rope_tabs = _rope_tables(lat_len)
    zpad = jnp.zeros((DEPTH, LANES - C_QK_DIM), F32)
    lam_rows = [jnp.concatenate([v, zpad], axis=-1) for v in (lam_q1, lam_k1, lam_q2, lam_k2)]
    lam_par = jnp.concatenate([jnp.stack(lam_rows, axis=1), jnp.zeros((DEPTH, 4, LANES), F32)], axis=1)
    gf = g_final.reshape(1, D_MODEL)

    xc = x_prompt.reshape(n_ctx_req * ctx_len, D_MODEL)
    xs = x_sample.reshape(n_lat_req * lat_len, D_MODEL)
    lat_tm = 512
    caches = []
    for l in range(DEPTH):
        lam_init = 0.8 - 0.6 * math.exp(-0.3 * l)
        gn1 = g_norm1[l].reshape(1, D_MODEL)
        gn2 = g_norm2[l].reshape(1, D_MODEL)
        gq = jnp.tile(g_qa[l], LANES // HEAD_DIM).reshape(1, LANES)
        gk = jnp.tile(g_ka[l], LANES // HEAD_DIM).reshape(1, LANES)
        gsub = jnp.tile(g_subln[l], LANES // C_V_DIM).reshape(1, LANES)
        post_w = (gn2, w_out_p[l], w_up_b[l], conv_w[l], conv_b[l].reshape(1, 2 * D_FF), w_down_b[l], gf)
        final = l == DEPTH - 1

        n_tiles = n_ctx_req
        zc, cache = _in_proj(xc, mod[l, 0:1], gn1, w_in_p[l], gq, gk, None, tm=ctx_len, tiles_per_cond=n_tiles,
                             tiles_per_seq=1, emit_cache=True)
        caches.append(cache)
        oc = _attn_ctx(zc, sink_b[l], lam_par[l], gsub, seq=ctx_len, lam_init=lam_init)
        xc = _post(xc, oc, mod[l, 0:1], *post_w, tm=ctx_len, tiles_per_cond=n_tiles, tiles_per_seq=1, final=final)

        per_seq = lat_len // lat_tm
        zs = _in_proj(xs, mod[l, 1:1 + n_lat_req], gn1, w_in_p[l], gq, gk, rope_tabs, tm=lat_tm,
                      tiles_per_cond=per_seq, tiles_per_seq=per_seq, emit_cache=False)[0]
        flat = lambda a: a[:, l].reshape(n_lat_req, past, -1).astype(BF16)
        os_ = _attn_a(zs, flat(cache_a_k), flat(cache_a_v), seq=lat_len, tq=128, tk=512)
        os_ = _attn_b(zs, flat(cache_b_k), flat(cache_b_v), sink_b[l], os_, seq=lat_len, tq=128)
        os_ = _attn_c(zs, flat(cache_c_k), flat(cache_c_v), lam_par[l], gsub, os_, seq=lat_len, tq=256, tk=512,
                      lam_init=lam_init)
        xs = _post(xs, os_, mod[l, 1:1 + n_lat_req], *post_w, tm=lat_tm, tiles_per_cond=per_seq,
                   tiles_per_seq=per_seq, final=final)

    def stack(lo, hi, heads):
        parts = [cch[:, lo - Z_KA:hi - Z_KA].reshape(n_ctx_req, ctx_len, heads, -1) for cch in caches]
        return jnp.stack(parts, axis=1)

    y_prompt = xc.reshape(x_prompt.shape)
    y_sample = xs.reshape(x_sample.shape)
    return (y_prompt, y_sample,
            stack(Z_KA, Z_VA, A_KV), stack(Z_VA, Z_KB, A_KV), stack(Z_KB, Z_VB, B_KV), stack(Z_VB, Z_KC, B_KV),
            stack(Z_KC, Z_VC, C_HEADS), stack(Z_VC, D_IN, C_HEADS))
```

```python
import functools
import math

import numpy as np
import jax
import jax.numpy as jnp
from jax import lax
from jax.experimental import pallas as pl
from jax.experimental.pallas import tpu as pltpu

D_MODEL = 1024
DEPTH = 2
GRID_W = 64
HEAD_DIM = 64
A_HEADS = 6
A_KV = 2
B_HEADS = 6
B_KV = 2
C_HEADS = 4
C_QK_DIM = 32
C_V_DIM = 2 * C_QK_DIM
WINDOW = 128
ROPE_THETA = 10000.0
D_FF = 2816
EPS = 1e-6
NEG = -1e30
LOG2E = math.log2(math.e)

LANES = 128
BF16_ROWS = 16
VMEM_LIMIT = 56 * 1024 * 1024

Z_QA, Z_QB, Z_QC = 0, 384, 768
Z_KA, Z_VA, Z_KB, Z_VB, Z_KC, Z_VC = 1024, 1152, 1280, 1408, 1536, 1792
D_IN = 2048
T_QA, T_VA, T_QC, T_VC, T_ROWS = 0, 384, 512, 768, 1024
PAIRED_HEADS = (0, 3, 1, 4, 2, 5)
FFN_CHUNK = 256
HALO = BF16_ROWS

F32 = jnp.float32
BF16 = jnp.bfloat16


def _cparams(sem):
    return pltpu.CompilerParams(dimension_semantics=sem, vmem_limit_bytes=VMEM_LIMIT)


def _const_spec(shape):
    nd = len(shape)
    return pl.BlockSpec(shape, lambda *_: (0,) * nd)


def _mod_kernel(c_ref, w_ref, b_ref, o_ref):
    cond = c_ref[...]
    a = cond / (1.0 + jnp.exp(-cond))
    o_ref[0] = jnp.dot(a.astype(BF16), w_ref[0].astype(BF16), preferred_element_type=F32) + b_ref[0]


def _modulation(conds, w_ada, b_ada):
    nb = 1536
    n_out = w_ada.shape[-1]
    return pl.pallas_call(
        _mod_kernel,
        grid=(DEPTH, n_out // nb),
        in_specs=[
            pl.BlockSpec((8, D_MODEL), lambda l, j: (0, 0)),
            pl.BlockSpec((1, D_MODEL, nb), lambda l, j: (l, 0, j)),
            pl.BlockSpec((1, 1, nb), lambda l, j: (l, 0, j)),
        ],
        out_specs=pl.BlockSpec((1, 8, nb), lambda l, j: (l, 0, j)),
        out_shape=jax.ShapeDtypeStruct((DEPTH, 8, n_out), F32),
        compiler_params=_cparams(("arbitrary", "arbitrary")),
        name="modulation",
    )(conds, w_ada, b_ada.reshape(DEPTH, 1, n_out))


def _rms(x, g):
    ms = jnp.mean(x * x, axis=-1, keepdims=True)
    return x * lax.rsqrt(ms + EPS) * g


def _head_rms(x, g, lo):
    ss = x * x
    s_lo = jnp.sum(jnp.where(lo, ss, 0.0), axis=-1, keepdims=True)
    s_hi = jnp.sum(jnp.where(lo, 0.0, ss), axis=-1, keepdims=True)
    inv = jnp.where(lo, lax.rsqrt(s_lo * (1.0 / HEAD_DIM) + EPS), lax.rsqrt(s_hi * (1.0 / HEAD_DIM) + EPS))
    return x * inv * g


def _rope(x, cos, sin, chunk, first):
    sw = jnp.where(first, pltpu.roll(x, LANES - chunk, 1), pltpu.roll(x, chunk, 1))
    return x * cos + sw * sin


def _softmax_init(m_ref, l_ref, acc_ref, rows):
    m_ref[0:rows] = jnp.full((rows, LANES), NEG, F32)
    l_ref[0:rows] = jnp.zeros((rows, LANES), F32)
    acc_ref[0:rows] = jnp.zeros((rows, LANES), F32)


def _softmax_update(q, k, v, m_ref, l_ref, acc_ref, rows, mask=None):
    s = lax.dot_general(q, k, (((1,), (1,)), ((), ())), preferred_element_type=F32)
    if mask is not None:
        s = jnp.where(mask, s, NEG)
    m_prev = m_ref[0:rows]
    m_new = jnp.maximum(m_prev, jnp.max(s, axis=1, keepdims=True))
    alpha = jnp.exp2(m_prev - m_new)
    p = jnp.exp2(s - m_new[:, 0:1])
    l_ref[0:rows] = alpha * l_ref[0:rows] + jnp.sum(p, axis=1, keepdims=True)
    acc_ref[0:rows] = alpha * acc_ref[0:rows] + jnp.dot(p.astype(BF16), v, preferred_element_type=F32)
    m_ref[0:rows] = m_new


def _softmax_result(l_ref, acc_ref, rows):
    return acc_ref[0:rows] * (1.0 / l_ref[0:rows])


def _softmax_init_t(m_ref, l_ref, acc_ref):
    m_ref[...] = jnp.full(m_ref.shape, NEG, F32)
    l_ref[...] = jnp.zeros(l_ref.shape, F32)
    acc_ref[...] = jnp.zeros(acc_ref.shape, F32)


def _softmax_update_t(qt, k, vt, m_ref, l_ref, acc_ref):
    s = jnp.dot(k, qt, preferred_element_type=F32)
    m_prev = m_ref[...]
    m_new = jnp.maximum(m_prev, jnp.max(s, axis=0, keepdims=True))
    alpha = jnp.exp2(m_prev - m_new)
    p = jnp.exp2(s - m_new)
    l_ref[...] = alpha * l_ref[...] + jnp.sum(p, axis=0, keepdims=True)
    acc_ref[...] = alpha * acc_ref[...] + jnp.dot(vt, p.astype(BF16), preferred_element_type=F32)
    m_ref[...] = m_new


def _scores_t(k, q_scr, s_ref):
    s_ref[...] = jnp.dot(k, q_scr[...], preferred_element_type=F32)


def _probs_t(s_ref, p_ref, a_ref, m_ref, l_ref):
    s = s_ref[...]
    m_prev = m_ref[...]
    m_new = jnp.maximum(m_prev, jnp.max(s, axis=0, keepdims=True))
    alpha = jnp.exp2(m_prev - m_new)
    p = jnp.exp2(s - m_new)
    l_ref[...] = alpha * l_ref[...] + jnp.sum(p, axis=0, keepdims=True)
    m_ref[...] = m_new
    a_ref[...] = alpha
    p_ref[...] = p.astype(BF16)


def _values_t(vt, p_ref, a_ref, acc_ref):
    acc_ref[...] = a_ref[...] * acc_ref[...] + jnp.dot(vt, p_ref[...], preferred_element_type=F32)


def _attend_pipelined_t(q_scr, k_ref, vt_ref, tk, s_bufs, p_bufs, a_bufs, m_ref, l_ref, acc_ref):
    n_blocks = k_ref.shape[0] // tk
    assert n_blocks % 2 == 0 and n_blocks >= 4

    def k_blk(t):
        return k_ref[pl.ds(pl.multiple_of(t * tk, tk), tk), :]

    def v_blk(t):
        return vt_ref[:, pl.ds(pl.multiple_of(t * tk, tk), tk)]

    def step(t, par, with_scores):
        if with_scores:
            _scores_t(k_blk(t + 1), q_scr, s_bufs[1 - par])
        _probs_t(s_bufs[par], p_bufs[par], a_bufs[par], m_ref, l_ref)
        _values_t(v_blk(t - 1), p_bufs[1 - par], a_bufs[1 - par], acc_ref)

    _scores_t(k_blk(0), q_scr, s_bufs[0])
    _probs_t(s_bufs[0], p_bufs[0], a_bufs[0], m_ref, l_ref)
    _scores_t(k_blk(1), q_scr, s_bufs[1])

    def pair(jj, carry):
        t = 1 + 2 * jj
        step(t, 1, True)
        step(t + 1, 0, True)
        return carry

    lax.fori_loop(0, n_blocks // 2 - 1, pair, 0)
    step(n_blocks - 1, 1, False)
    _values_t(v_blk(n_blocks - 1), p_bufs[1], a_bufs[1], acc_ref)


def _gqa_queries(q3, lo):
    slabs = [q3[:, LANES * s:LANES * (s + 1)] for s in range(3)]
    zero = jnp.zeros_like(slabs[0])
    parts = [jnp.where(lo, s, zero) for s in slabs] + [jnp.where(lo, zero, s) for s in slabs]
    return jnp.concatenate(parts, axis=0)


def _gqa_outputs(o, rows, lo):
    return [jnp.where(lo, o[s * rows:(s + 1) * rows], o[(3 + s) * rows:(4 + s) * rows]) for s in range(3)]


def _diff_queries(q, lane):
    zero = jnp.zeros_like(q)
    parts = []
    for j in range(4):
        sel = (lane >= C_QK_DIM * j) & (lane < C_QK_DIM * (j + 1))
        parts.append(jnp.where(sel, q, zero))
    return jnp.concatenate(parts, axis=0)


def _diff_lambda(lam_ref, lam_init):
    f = lambda a, b: jnp.exp(jnp.sum(a * b, axis=-1, keepdims=True))
    return f(lam_ref[0:1], lam_ref[1:2]) - f(lam_ref[2:3], lam_ref[3:4]) + lam_init


def _diff_output(o, rows, lam, gsub, lam_init, lo):
    o_even = o[0:rows] - lam * o[rows:2 * rows]
    o_odd = o[2 * rows:3 * rows] - lam * o[3 * rows:4 * rows]
    oc = jnp.where(lo, o_even, o_odd)
    return _head_rms(oc, gsub, lo) * (1.0 - lam_init)


def _sink_init(sink_ref, m_ref, l_ref, acc_ref, rows):
    for h in range(B_HEADS):
        m_ref[h * rows:(h + 1) * rows] = jnp.full((rows, LANES), sink_ref[h] * LOG2E, F32)
    l_ref[0:B_HEADS * rows] = jnp.ones((B_HEADS * rows, LANES), F32)
    acc_ref[0:B_HEADS * rows] = jnp.zeros((B_HEADS * rows, LANES), F32)


def _in_proj_kernel(*refs, use_rope, emit_cache, emit_t):
    it = iter(refs)
    x_ref, mod_ref, gn_ref, w_ref, gq_ref, gk_ref = (next(it) for _ in range(6))
    if use_rope:
        cos64, sin64, cos32, sin32 = (next(it)[...] for _ in range(4))
    z_ref = next(it)
    cache_ref = next(it) if emit_cache else None
    zt_ref = next(it) if emit_t else None

    x = x_ref[...]
    mod = mod_ref[0]
    h = _rms(x, gn_ref[...]) * (1.0 + mod[1:2]) + mod[0:1]
    z = jnp.dot(h.astype(BF16), w_ref[...], preferred_element_type=F32)

    rows = x.shape[0]
    lane = lax.broadcasted_iota(jnp.int32, (rows, LANES), 1)
    lo = lane < HEAD_DIM
    first16 = (lane & 31) < 16
    first8 = (lane & 15) < 8
    gq = gq_ref[...]
    gk = gk_ref[...]
    q_scale = HEAD_DIM ** -0.5 * LOG2E
    qc_scale = C_QK_DIM ** -0.5 * LOG2E

    def rope64(v):
        return _rope(v, cos64, sin64, 16, first16) if use_rope else v

    def rope32(v):
        return _rope(v, cos32, sin32, 8, first8) if use_rope else v

    def slab(off):
        return z[:, off:off + LANES]

    def put(off, v):
        z_ref[:, off:off + LANES] = v.astype(BF16)

    def put_cache(off, v):
        if emit_cache:
            cache_ref[:, off - Z_KA:off - Z_KA + LANES] = v

    def put_t(off, v):
        if emit_t:
            zt_ref[off:off + LANES, :] = v.T.astype(BF16)

    for s in range(3):
        qa = rope64(_head_rms(slab(Z_QA + LANES * s), gq, lo)) * q_scale
        put(Z_QA + LANES * s, qa)
        put_t(T_QA + LANES * s, qa)
        put(Z_QB + LANES * s, rope64(slab(Z_QB + LANES * s)) * q_scale)
    ka = _head_rms(slab(Z_KA), gk, lo)
    put_cache(Z_KA, ka)
    put(Z_KA, rope64(ka))
    kb = slab(Z_KB)
    put_cache(Z_KB, kb)
    put(Z_KB, rope64(kb))
    for off in (Z_VA, Z_VB, Z_VC, Z_VC + LANES):
        put_cache(off, slab(off))
        put(off, slab(off))
    put_t(T_VA, slab(Z_VA))
    for s in range(2):
        put_t(T_VC + LANES * s, slab(Z_VC + LANES * s))
        qc = rope32(slab(Z_QC + LANES * s)) * qc_scale
        put(Z_QC + LANES * s, qc)
        put_t(T_QC + LANES * s, qc)
        kc = slab(Z_KC + LANES * s)
        put_cache(Z_KC + LANES * s, kc)
        put(Z_KC + LANES * s, rope32(kc))


def _in_proj(x, mod_l, gn, w_in_p, gq, gk, rope_tabs, *, tm, tiles_per_cond, tiles_per_seq, emit_cache, emit_t):
    t = x.shape[0]
    use_rope = rope_tabs is not None
    in_specs = [
        pl.BlockSpec((tm, D_MODEL), lambda i: (i, 0)),
        pl.BlockSpec((1, 6, D_MODEL), lambda i: (i // tiles_per_cond, 0, 0)),
        _const_spec((1, D_MODEL)),
        _const_spec((D_MODEL, D_IN)),
        _const_spec((1, LANES)),
        _const_spec((1, LANES)),
    ]
    args = [x, mod_l, gn, w_in_p, gq, gk]
    if use_rope:
        in_specs += [pl.BlockSpec((tm, LANES), lambda i: (i % tiles_per_seq, 0))] * 4
        args += list(rope_tabs)
    out_shape = [jax.ShapeDtypeStruct((t, D_IN), BF16)]
    out_specs = [pl.BlockSpec((tm, D_IN), lambda i: (i, 0))]
    if emit_cache:
        out_shape.append(jax.ShapeDtypeStruct((t, D_IN - Z_KA), F32))
        out_specs.append(pl.BlockSpec((tm, D_IN - Z_KA), lambda i: (i, 0)))
    if emit_t:
        out_shape.append(jax.ShapeDtypeStruct((T_ROWS, t), BF16))
        out_specs.append(pl.BlockSpec((T_ROWS, tm), lambda i: (0, i)))
    return pl.pallas_call(
        functools.partial(_in_proj_kernel, use_rope=use_rope, emit_cache=emit_cache, emit_t=emit_t),
        grid=(t // tm,),
        in_specs=in_specs,
        out_specs=out_specs,
        out_shape=out_shape,
        compiler_params=_cparams(("arbitrary",)),
        name="in_proj",
    )(*args)


def _attn_ctx_kernel(sink_ref, lam_ref, gsub_ref, z_ref, o_ref, m_ref, l_ref, acc_ref, *, lam_init):
    rows = z_ref.shape[0]
    lane = lax.broadcasted_iota(jnp.int32, (rows, LANES), 1)
    lo = lane < HEAD_DIM
    refs = (m_ref, l_ref, acc_ref)

    def piece(off, width=LANES):
        return z_ref[:, off:off + width]

    n = A_HEADS * rows
    _softmax_init(*refs, n)
    _softmax_update(_gqa_queries(piece(Z_QA, 384), lo), piece(Z_KA), piece(Z_VA), *refs, n)
    for s, o in enumerate(_gqa_outputs(_softmax_result(l_ref, acc_ref, n), rows, lo)):
        o_ref[:, LANES * s:LANES * (s + 1)] = o.astype(BF16)

    _sink_init(sink_ref, *refs, rows)
    _softmax_update(_gqa_queries(piece(Z_QB, 384), lo), piece(Z_KB), piece(Z_VB), *refs, n)
    for s, o in enumerate(_gqa_outputs(_softmax_result(l_ref, acc_ref, n), rows, lo)):
        o_ref[:, 384 + LANES * s:384 + LANES * (s + 1)] = o.astype(BF16)

    lam = _diff_lambda(lam_ref, lam_init)
    n = 4 * rows
    for s in range(2):
        _softmax_init(*refs, n)
        _softmax_update(_diff_queries(piece(Z_QC + LANES * s), lane), piece(Z_KC + LANES * s),
                        piece(Z_VC + LANES * s), *refs, n)
        oc = _diff_output(_softmax_result(l_ref, acc_ref, n), rows, lam, gsub_ref[...], lam_init, lo)
        o_ref[:, 768 + LANES * s:768 + LANES * (s + 1)] = oc.astype(BF16)


def _attn_ctx(z, sink, lam_par, gsub, *, seq, lam_init):
    t = z.shape[0]
    return pl.pallas_call(
        functools.partial(_attn_ctx_kernel, lam_init=lam_init),
        grid=(t // seq,),
        in_specs=[
            pl.BlockSpec(memory_space=pltpu.SMEM),
            _const_spec((8, LANES)),
            _const_spec((1, LANES)),
            pl.BlockSpec((seq, D_IN), lambda b: (b, 0)),
        ],
        out_specs=pl.BlockSpec((seq, D_MODEL), lambda b: (b, 0)),
        out_shape=jax.ShapeDtypeStruct((t, D_MODEL), BF16),
        scratch_shapes=[pltpu.VMEM((A_HEADS * seq, LANES), F32)] * 3,
        compiler_params=_cparams(("arbitrary",)),
        name="attn_ctx",
    )(sink, lam_par, gsub, z)


def _attn_a_kernel(qt_ref, kc_ref, vct_ref, k_ref, vt_ref, o_ref, q_scr, m_ref, l_ref, acc_ref,
                   s0, s1, p0, p1, a0, a1, *, tk):
    tq = qt_ref.shape[1]
    lo = lax.broadcasted_iota(jnp.int32, (LANES, tq), 0) < HEAD_DIM
    refs = (m_ref, l_ref, acc_ref)
    slabs = [qt_ref[LANES * s:LANES * (s + 1), :] for s in range(3)]
    zero = jnp.zeros_like(slabs[0])
    q_scr[...] = jnp.concatenate([jnp.where(lo, s, zero) for s in slabs] + [jnp.where(lo, zero, s) for s in slabs],
                                 axis=1)
    _softmax_init_t(*refs)
    _softmax_update_t(q_scr[...], kc_ref[0], vct_ref[0], *refs)
    _attend_pipelined_t(q_scr, k_ref, vt_ref, tk, (s0, s1), (p0, p1), (a0, a1), *refs)
    ot = acc_ref[...] * (1.0 / l_ref[...])
    for s in range(3):
        slab_t = jnp.where(lo, ot[:, s * tq:(s + 1) * tq], ot[:, (3 + s) * tq:(4 + s) * tq])
        o_ref[:, LANES * s:LANES * (s + 1)] = slab_t.T.astype(BF16)


def _keys_major_scratch(n, tk):
    return [pltpu.VMEM((LANES, n), BF16), pltpu.VMEM((1, n), F32), pltpu.VMEM((1, n), F32),
            pltpu.VMEM((LANES, n), F32),
            pltpu.VMEM((tk, n), F32), pltpu.VMEM((tk, n), F32),
            pltpu.VMEM((tk, n), BF16), pltpu.VMEM((tk, n), BF16),
            pltpu.VMEM((1, n), F32), pltpu.VMEM((1, n), F32)]


def _attn_a(z, zt, k_ctx, vt_ctx, *, seq, tq, tk):
    t = z.shape[0]
    nq = seq // tq
    n_ctx = k_ctx.shape[1]
    n = A_HEADS * tq
    return pl.pallas_call(
        functools.partial(_attn_a_kernel, tk=tk),
        grid=(t // seq, nq),
        in_specs=[
            pl.BlockSpec((384, tq), lambda b, i: (T_QA // 384, b * nq + i)),
            pl.BlockSpec((1, n_ctx, LANES), lambda b, i: (b, 0, 0)),
            pl.BlockSpec((1, LANES, n_ctx), lambda b, i: (b, 0, 0)),
            pl.BlockSpec((seq, LANES), lambda b, i: (b, Z_KA // LANES)),
            pl.BlockSpec((LANES, seq), lambda b, i: (T_VA // LANES, b)),
        ],
        out_specs=pl.BlockSpec((tq, 384), lambda b, i: (b * nq + i, 0)),
        out_shape=jax.ShapeDtypeStruct((t, D_MODEL), BF16),
        scratch_shapes=_keys_major_scratch(n, tk),
        compiler_params=_cparams(("arbitrary", "arbitrary")),
        name="attn_a",
    )(zt, k_ctx, vt_ctx, z, zt)


def _attn_b_kernel(sink_ref, q_ref, kc_ref, vc_ref, k_ref, v_ref, o_in_ref, o_ref, m_ref, l_ref, acc_ref):
    del o_in_ref
    rows = q_ref.shape[0]
    seq = k_ref.shape[0]
    n = B_HEADS * rows
    band = rows + 2 * WINDOW
    i = pl.program_id(1)
    lo = lax.broadcasted_iota(jnp.int32, (rows, LANES), 1) < HEAD_DIM
    refs = (m_ref, l_ref, acc_ref)
    q = _gqa_queries(q_ref[...], lo)
    _sink_init(sink_ref, *refs, rows)
    _softmax_update(q, kc_ref[0], vc_ref[0], *refs, n)
    start = pl.multiple_of(jnp.clip(i * rows - WINDOW, 0, seq - band), LANES)
    qpos = i * rows + (lax.broadcasted_iota(jnp.int32, (n, band), 0) & (rows - 1))
    kpos = start + lax.broadcasted_iota(jnp.int32, (n, band), 1)
    mask = jnp.abs(kpos - qpos) <= WINDOW
    _softmax_update(q, k_ref[pl.ds(start, band), :], v_ref[pl.ds(start, band), :], *refs, n, mask=mask)
    for s, o in enumerate(_gqa_outputs(_softmax_result(l_ref, acc_ref, n), rows, lo)):
        o_ref[:, LANES * s:LANES * (s + 1)] = o.astype(BF16)


def _attn_b(z, k_ctx, v_ctx, sink, o_buf, *, seq, tq):
    t = z.shape[0]
    nq = seq // tq
    n_ctx = k_ctx.shape[1]
    return pl.pallas_call(
        _attn_b_kernel,
        grid=(t // seq, nq),
        in_specs=[
            pl.BlockSpec(memory_space=pltpu.SMEM),
            pl.BlockSpec((tq, 384), lambda b, i: (b * nq + i, Z_QB // 384)),
            pl.BlockSpec((1, n_ctx, LANES), lambda b, i: (b, 0, 0)),
            pl.BlockSpec((1, n_ctx, LANES), lambda b, i: (b, 0, 0)),
            pl.BlockSpec((seq, LANES), lambda b, i: (b, Z_KB // LANES)),
            pl.BlockSpec((seq, LANES), lambda b, i: (b, Z_VB // LANES)),
            pl.BlockSpec(memory_space=pl.ANY),
        ],
        out_specs=pl.BlockSpec((tq, 384), lambda b, i: (b * nq + i, 1)),
        out_shape=jax.ShapeDtypeStruct((t, D_MODEL), BF16),
        input_output_aliases={6: 0},
        scratch_shapes=[pltpu.VMEM((B_HEADS * tq, LANES), F32)] * 3,
        compiler_params=_cparams(("arbitrary", "arbitrary")),
        name="attn_b",
    )(sink, z, k_ctx, v_ctx, z, z, o_buf)


def _attn_c_kernel(lam_ref, gsub_ref, qt_ref, kc_ref, vct_ref, k_ref, vt_ref, o_in_ref, o_ref,
                   q_scr, m_ref, l_ref, acc_ref, s0, s1, p0, p1, a0, a1, *, tk, lam_init):
    del o_in_ref
    tq = qt_ref.shape[1]
    row = lax.broadcasted_iota(jnp.int32, (LANES, tq), 0)
    refs = (m_ref, l_ref, acc_ref)
    qt = qt_ref[...]
    zero = jnp.zeros_like(qt)
    parts = [jnp.where((row >= C_QK_DIM * j) & (row < C_QK_DIM * (j + 1)), qt, zero) for j in range(4)]
    q_scr[...] = jnp.concatenate(parts, axis=1)
    _softmax_init_t(*refs)
    _softmax_update_t(q_scr[...], kc_ref[0], vct_ref[0], *refs)
    _attend_pipelined_t(q_scr, k_ref, vt_ref, tk, (s0, s1), (p0, p1), (a0, a1), *refs)
    lam = _diff_lambda(lam_ref, lam_init)
    ot = acc_ref[...] * (1.0 / l_ref[...])
    o_even = ot[:, 0:tq] - lam * ot[:, tq:2 * tq]
    o_odd = ot[:, 2 * tq:3 * tq] - lam * ot[:, 3 * tq:4 * tq]
    oc = jnp.where(row < C_V_DIM, o_even, o_odd).T
    lo = lax.broadcasted_iota(jnp.int32, (tq, LANES), 1) < C_V_DIM
    o_ref[...] = (_head_rms(oc, gsub_ref[...], lo) * (1.0 - lam_init)).astype(BF16)


def _attn_c(z, zt, k_ctx, vt_ctx, lam_par, gsub, o_buf, *, seq, tq, tk, lam_init):
    t = z.shape[0]
    nq = seq // tq
    n_ctx = k_ctx.shape[1]
    n = 4 * tq
    return pl.pallas_call(
        functools.partial(_attn_c_kernel, tk=tk, lam_init=lam_init),
        grid=(t // seq, 2, nq),
        in_specs=[
            _const_spec((8, LANES)),
            _const_spec((1, LANES)),
            pl.BlockSpec((LANES, tq), lambda b, s, i: (T_QC // LANES + s, b * nq + i)),
            pl.BlockSpec((1, n_ctx, LANES), lambda b, s, i: (b, 0, s)),
            pl.BlockSpec((1, LANES, n_ctx), lambda b, s, i: (b, s, 0)),
            pl.BlockSpec((seq, LANES), lambda b, s, i: (b, Z_KC // LANES + s)),
            pl.BlockSpec((LANES, seq), lambda b, s, i: (T_VC // LANES + s, b)),
            pl.BlockSpec(memory_space=pl.ANY),
        ],
        out_specs=pl.BlockSpec((tq, LANES), lambda b, s, i: (b * nq + i, 768 // LANES + s)),
        out_shape=jax.ShapeDtypeStruct((t, D_MODEL), BF16),
        input_output_aliases={7: 0},
        scratch_shapes=_keys_major_scratch(n, tk),
        compiler_params=_cparams(("arbitrary", "arbitrary", "arbitrary")),
        name="attn_c",
    )(lam_par, gsub, zt, k_ctx, vt_ctx, z, zt, o_buf)


def _post_kernel(*refs, halo, tiles_per_seq, final):
    it = iter(refs)
    x_ref, o_ref = next(it), next(it)
    if halo:
        xp_ref, xn_ref, op_ref, on_ref = (next(it) for _ in range(4))
    (mod_ref, gn_ref, wo_ref, wu_ref, cw_ref, cb_ref, wd_ref, gf_ref, out_ref) = (next(it) for _ in range(9))

    tm = x_ref.shape[0]
    if halo:
        x = jnp.concatenate([xp_ref[...], x_ref[...], xn_ref[...]], axis=0)
        o = jnp.concatenate([op_ref[...], o_ref[...], on_ref[...]], axis=0)
    else:
        x, o = x_ref[...], o_ref[...]
    ext = x.shape[0]
    mod = mod_ref[0]
    x1 = x + mod[2:3] * jnp.dot(o, wo_ref[...], preferred_element_type=F32)
    h = _rms(x1, gn_ref[...]) * (1.0 + mod[4:5]) + mod[3:4]
    row = lax.broadcasted_iota(jnp.int32, (ext, 1), 0)
    if halo:
        t_in_seq = pl.program_id(0) % tiles_per_seq
        keep = ((row >= halo) | (t_in_seq > 0)) & ((row < halo + tm) | (t_in_seq < tiles_per_seq - 1))
        h = jnp.where(keep, h, 0.0)
    h = h.astype(BF16)

    def conv(u, c0):
        cw = cw_ref[:, c0:c0 + FFN_CHUNK]
        up = pltpu.roll(u, 1, 0)
        dn = pltpu.roll(u, ext - 1, 0)
        if not halo:
            up = jnp.where(row == 0, 0.0, up)
            dn = jnp.where(row == ext - 1, 0.0, dn)
        v = cw[0:1] * up + cw[1:2] * u + cw[2:3] * dn + cb_ref[:, c0:c0 + FFN_CHUNK]
        return v[halo:halo + tm]

    acc = jnp.zeros((tm, D_MODEL), F32)
    for c in range(D_FF // FFN_CHUNK):
        ca, cg = c * FFN_CHUNK, D_FF + c * FFN_CHUNK
        a = conv(jnp.dot(h, wu_ref[:, ca:ca + FFN_CHUNK], preferred_element_type=F32), ca)
        g = conv(jnp.dot(h, wu_ref[:, cg:cg + FFN_CHUNK], preferred_element_type=F32), cg)
        act = a / (1.0 + jnp.exp(-a)) * g
        acc = acc + jnp.dot(act.astype(BF16), wd_ref[ca:ca + FFN_CHUNK, :], preferred_element_type=F32)
    x2 = x1[halo:halo + tm] + mod[5:6] * acc
    if final:
        x2 = _rms(x2, gf_ref[...])
    out_ref[...] = x2


def _post(x, o, mod_l, gn2, wo, wu, cw, cb, wd, gf, *, tm, tiles_per_cond, tiles_per_seq, final):
    t = x.shape[0]
    halo = HALO if tiles_per_seq > 1 else 0
    tile = lambda i: (i, 0)
    in_specs = [pl.BlockSpec((tm, D_MODEL), tile), pl.BlockSpec((tm, D_MODEL), tile)]
    args = [x, o]
    if halo:
        per = tm // halo
        prev = lambda i: (jnp.maximum(i * per - 1, 0), 0)
        nxt = lambda i: (jnp.minimum((i + 1) * per, t // halo - 1), 0)
        in_specs += [pl.BlockSpec((halo, D_MODEL), prev), pl.BlockSpec((halo, D_MODEL), nxt)] * 2
        args += [x, x, o, o]
    in_specs += [
        pl.BlockSpec((1, 6, D_MODEL), lambda i: (i // tiles_per_cond, 0, 0)),
        _const_spec((1, D_MODEL)),
        _const_spec((D_MODEL, D_MODEL)),
        _const_spec((D_MODEL, 2 * D_FF)),
        _const_spec((3, 2 * D_FF)),
        _const_spec((1, 2 * D_FF)),
        _const_spec((D_FF, D_MODEL)),
        _const_spec((1, D_MODEL)),
    ]
    args += [mod_l, gn2, wo, wu, cw, cb, wd, gf]
    return pl.pallas_call(
        functools.partial(_post_kernel, halo=halo, tiles_per_seq=tiles_per_seq, final=final),
        grid=(t // tm,),
        in_specs=in_specs,
        out_specs=pl.BlockSpec((tm, D_MODEL), tile),
        out_shape=jax.ShapeDtypeStruct((t, D_MODEL), F32),
        compiler_params=_cparams(("arbitrary",)),
        name="post",
    )(*args)


def _rope_tables(seq):
    t = jnp.arange(seq)
    rows = (t // GRID_W).astype(F32)[:, None]
    cols = (t % GRID_W).astype(F32)[:, None]

    def tab(half, reps):
        inv = ROPE_THETA ** (-jnp.arange(half, dtype=F32) / half)
        ar, ac = rows * inv[None, :], cols * inv[None, :]
        cos = jnp.concatenate([jnp.cos(ar), jnp.cos(ar), jnp.cos(ac), jnp.cos(ac)], axis=-1)
        sin = jnp.concatenate([-jnp.sin(ar), jnp.sin(ar), -jnp.sin(ac), jnp.sin(ac)], axis=-1)
        return jnp.tile(cos, (1, reps)), jnp.tile(sin, (1, reps))

    cos64, sin64 = tab(HEAD_DIM // 4, LANES // HEAD_DIM)
    cos32, sin32 = tab(C_QK_DIM // 4, LANES // C_QK_DIM)
    return cos64, sin64, cos32, sin32


def _w_in_perm():
    sizes = (A_HEADS * HEAD_DIM, A_KV * HEAD_DIM, A_KV * HEAD_DIM, B_HEADS * HEAD_DIM, B_KV * HEAD_DIM,
             B_KV * HEAD_DIM, C_HEADS * 2 * C_QK_DIM, C_HEADS * 2 * C_QK_DIM, C_HEADS * C_V_DIM)
    offs = np.concatenate([[0], np.cumsum(sizes)])
    qa, ka, va, qb, kb, vb, qc, kc, vc = (np.arange(offs[j], offs[j + 1]) for j in range(9))
    pair = np.concatenate([np.arange(h * HEAD_DIM, (h + 1) * HEAD_DIM) for h in PAIRED_HEADS])
    return np.concatenate([qa[pair], qb[pair], qc, ka, va, kb, vb, kc, vc])


def _w_out_perm():
    pair = np.concatenate([np.arange(h * HEAD_DIM, (h + 1) * HEAD_DIM) for h in PAIRED_HEADS])
    return np.concatenate([pair, A_HEADS * HEAD_DIM + pair, np.arange(768, D_MODEL)])


def kernel(x_prompt, x_sample, cache_a_k, cache_a_v, cache_b_k, cache_b_v, cache_c_k, cache_c_v, c, c_ctx, w_ada, b_ada, g_norm1, g_norm2, w_in, g_qa, g_ka, sink_b, lam_q1, lam_k1, lam_q2, lam_k2, g_subln, w_out, w_up, conv_w, conv_b, w_down, g_final):
    n_ctx_req, ctx_len, _ = x_prompt.shape
    n_lat_req, lat_len, _ = x_sample.shape
    past = cache_a_k.shape[2]

    conds = jnp.zeros((8, D_MODEL), F32).at[0].set(c_ctx).at[1:1 + n_lat_req].set(c)
    mod = _modulation(conds, w_ada, b_ada).reshape(DEPTH, 8, 6, D_MODEL)

    w_in_p = w_in[:, :, _w_in_perm()].astype(BF16)
    w_out_p = w_out[:, _w_out_perm(), :].astype(BF16)
    w_up_b = w_up.astype(BF16)
    w_down_b = w_down.astype(BF16)
    rope_tabs = _rope_tables(lat_len)
    zpad = jnp.zeros((DEPTH, LANES - C_QK_DIM), F32)
    lam_rows = [jnp.concatenate([v, zpad], axis=-1) for v in (lam_q1, lam_k1, lam_q2, lam_k2)]
    lam_par = jnp.concatenate([jnp.stack(lam_rows, axis=1), jnp.zeros((DEPTH, 4, LANES), F32)], axis=1)
    gf = g_final.reshape(1, D_MODEL)

    xc = x_prompt.reshape(n_ctx_req * ctx_len, D_MODEL)
    xs = x_sample.reshape(n_lat_req * lat_len, D_MODEL)
    lat_tm = 512
    caches = []
    for l in range(DEPTH):
        lam_init = 0.8 - 0.6 * math.exp(-0.3 * l)
        gn1 = g_norm1[l].reshape(1, D_MODEL)
        gn2 = g_norm2[l].reshape(1, D_MODEL)
        gq = jnp.tile(g_qa[l], LANES // HEAD_DIM).reshape(1, LANES)
        gk = jnp.tile(g_ka[l], LANES // HEAD_DIM).reshape(1, LANES)
        gsub = jnp.tile(g_subln[l], LANES // C_V_DIM).reshape(1, LANES)
        post_w = (gn2, w_out_p[l], w_up_b[l], conv_w[l], conv_b[l].reshape(1, 2 * D_FF), w_down_b[l], gf)
        final = l == DEPTH - 1

        n_tiles = n_ctx_req
        zc, cache = _in_proj(xc, mod[l, 0:1], gn1, w_in_p[l], gq, gk, None, tm=ctx_len, tiles_per_cond=n_tiles,
                             tiles_per_seq=1, emit_cache=True, emit_t=False)
        caches.append(cache)
        oc = _attn_ctx(zc, sink_b[l], lam_par[l], gsub, seq=ctx_len, lam_init=lam_init)
        xc = _post(xc, oc, mod[l, 0:1], *post_w, tm=ctx_len, tiles_per_cond=n_tiles, tiles_per_seq=1, final=final)

        per_seq = lat_len // lat_tm
        zs, zts = _in_proj(xs, mod[l, 1:1 + n_lat_req], gn1, w_in_p[l], gq, gk, rope_tabs, tm=lat_tm,
                           tiles_per_cond=per_seq, tiles_per_seq=per_seq, emit_cache=False, emit_t=True)
        flat = lambda a: a[:, l].reshape(n_lat_req, past, -1).astype(BF16)
        flat_t = lambda a: jnp.swapaxes(flat(a), 1, 2)
        os_ = _attn_a(zs, zts, flat(cache_a_k), flat_t(cache_a_v), seq=lat_len, tq=128, tk=512)
        os_ = _attn_b(zs, flat(cache_b_k), flat(cache_b_v), sink_b[l], os_, seq=lat_len, tq=128)
        os_ = _attn_c(zs, zts, flat(cache_c_k), flat_t(cache_c_v), lam_par[l], gsub, os_, seq=lat_len, tq=256,
                      tk=512, lam_init=lam_init)
        xs = _post(xs, os_, mod[l, 1:1 + n_lat_req], *post_w, tm=lat_tm, tiles_per_cond=per_seq,
                   tiles_per_seq=per_seq, final=final)

    def stack(lo, hi, heads):
        parts = [cch[:, lo - Z_KA:hi - Z_KA].reshape(n_ctx_req, ctx_len, heads, -1) for cch in caches]
        return jnp.stack(parts, axis=1)

    y_prompt = xc.reshape(x_prompt.shape)
    y_sample = xs.reshape(x_sample.shape)
    return (y_prompt, y_sample,
            stack(Z_KA, Z_VA, A_KV), stack(Z_VA, Z_KB, A_KV), stack(Z_KB, Z_VB, B_KV), stack(Z_VB, Z_KC, B_KV),
            stack(Z_KC, Z_VC, C_HEADS), stack(Z_VC, D_IN, C_HEADS))
```

```python
import functools
import math

import numpy as np
import jax
import jax.numpy as jnp
from jax import lax
from jax.experimental import pallas as pl
from jax.experimental.pallas import tpu as pltpu

D_MODEL = 1024
DEPTH = 2
GRID_W = 64
HEAD_DIM = 64
A_HEADS = 6
A_KV = 2
B_HEADS = 6
B_KV = 2
C_HEADS = 4
C_QK_DIM = 32
C_V_DIM = 2 * C_QK_DIM
WINDOW = 128
ROPE_THETA = 10000.0
D_FF = 2816
EPS = 1e-6
NEG = -1e30
LOG2E = math.log2(math.e)

LANES = 128
BF16_ROWS = 16
MXU_COLS = 256
VMEM_LIMIT = 56 * 1024 * 1024

Z_QA, Z_QB, Z_QC = 0, 384, 768
Z_KA, Z_VA, Z_KB, Z_VB, Z_KC, Z_VC = 1024, 1152, 1280, 1408, 1536, 1792
D_IN = 2048
T_QA, T_VA, T_QC, T_VC, T_ROWS = 0, 384, 512, 768, 1024
PAIRED_HEADS = (0, 3, 1, 4, 2, 5)
FFN_CHUNK = 256
HALO = BF16_ROWS

F32 = jnp.float32
BF16 = jnp.bfloat16


def _cparams(sem):
    return pltpu.CompilerParams(dimension_semantics=sem, vmem_limit_bytes=VMEM_LIMIT)


def _const_spec(shape):
    nd = len(shape)
    return pl.BlockSpec(shape, lambda *_: (0,) * nd)


def _mod_kernel(c_ref, w_ref, b_ref, o_ref):
    cond = c_ref[...]
    a = cond / (1.0 + jnp.exp(-cond))
    o_ref[0] = jnp.dot(a.astype(BF16), w_ref[0].astype(BF16), preferred_element_type=F32) + b_ref[0]


def _modulation(conds, w_ada, b_ada):
    nb = 1536
    n_out = w_ada.shape[-1]
    return pl.pallas_call(
        _mod_kernel,
        grid=(DEPTH, n_out // nb),
        in_specs=[
            pl.BlockSpec((8, D_MODEL), lambda l, j: (0, 0)),
            pl.BlockSpec((1, D_MODEL, nb), lambda l, j: (l, 0, j)),
            pl.BlockSpec((1, 1, nb), lambda l, j: (l, 0, j)),
        ],
        out_specs=pl.BlockSpec((1, 8, nb), lambda l, j: (l, 0, j)),
        out_shape=jax.ShapeDtypeStruct((DEPTH, 8, n_out), F32),
        compiler_params=_cparams(("arbitrary", "arbitrary")),
        name="modulation",
    )(conds, w_ada, b_ada.reshape(DEPTH, 1, n_out))


def _rms(x, g):
    ms = jnp.mean(x * x, axis=-1, keepdims=True)
    return x * lax.rsqrt(ms + EPS) * g


def _head_rms(x, g, lo):
    ss = x * x
    s_lo = jnp.sum(jnp.where(lo, ss, 0.0), axis=-1, keepdims=True)
    s_hi = jnp.sum(jnp.where(lo, 0.0, ss), axis=-1, keepdims=True)
    inv = jnp.where(lo, lax.rsqrt(s_lo * (1.0 / HEAD_DIM) + EPS), lax.rsqrt(s_hi * (1.0 / HEAD_DIM) + EPS))
    return x * inv * g


def _rope(x, cos, sin, chunk, first):
    sw = jnp.where(first, pltpu.roll(x, LANES - chunk, 1), pltpu.roll(x, chunk, 1))
    return x * cos + sw * sin


def _softmax_init(m_ref, l_ref, acc_ref, rows):
    m_ref[0:rows] = jnp.full((rows, LANES), NEG, F32)
    l_ref[0:rows] = jnp.zeros((rows, LANES), F32)
    acc_ref[0:rows] = jnp.zeros((rows, LANES), F32)


def _softmax_update(q, k, v, m_ref, l_ref, acc_ref, rows, mask=None):
    s = lax.dot_general(q, k, (((1,), (1,)), ((), ())), preferred_element_type=F32)
    if mask is not None:
        s = jnp.where(mask, s, NEG)
    m_prev = m_ref[0:rows]
    m_new = jnp.maximum(m_prev, jnp.max(s, axis=1, keepdims=True))
    alpha = jnp.exp2(m_prev - m_new)
    p = jnp.exp2(s - m_new[:, 0:1])
    l_ref[0:rows] = alpha * l_ref[0:rows] + jnp.sum(p, axis=1, keepdims=True)
    acc_ref[0:rows] = alpha * acc_ref[0:rows] + jnp.dot(p.astype(BF16), v, preferred_element_type=F32)
    m_ref[0:rows] = m_new


def _softmax_result(l_ref, acc_ref, rows):
    return acc_ref[0:rows] * (1.0 / l_ref[0:rows])


def _softmax_init_t(m_ref, l_ref, acc_ref):
    m_ref[...] = jnp.full(m_ref.shape, NEG, F32)
    l_ref[...] = jnp.zeros(l_ref.shape, F32)
    acc_ref[...] = jnp.zeros(acc_ref.shape, F32)


def _softmax_update_t(qt, k, vt, m_ref, l_ref, acc_ref):
    s = jnp.dot(k, qt, preferred_element_type=F32)
    m_prev = m_ref[...]
    m_new = jnp.maximum(m_prev, jnp.max(s, axis=0, keepdims=True))
    alpha = jnp.exp2(m_prev - m_new)
    p = jnp.exp2(s - m_new)
    l_ref[...] = alpha * l_ref[...] + jnp.sum(p, axis=0, keepdims=True)
    acc_ref[...] = alpha * acc_ref[...] + jnp.dot(vt, p.astype(BF16), preferred_element_type=F32)
    m_ref[...] = m_new


def _scores_t(k, q_scr, s_ref, mx_ref, cs):
    s = jnp.dot(k, q_scr[:, cs], preferred_element_type=F32)
    s_ref[0:k.shape[0], cs] = s
    mx_ref[:, cs] = jnp.max(s, axis=0, keepdims=True)


def _probs_t(rows, s_ref, mx_ref, p_ref, a_ref, m_ref, l_ref, cs):
    s = s_ref[0:rows, cs]
    m_prev = m_ref[:, cs]
    m_new = jnp.maximum(m_prev, mx_ref[:, cs])
    alpha = jnp.exp2(m_prev - m_new)
    p = jnp.exp2(s - m_new)
    l_ref[:, cs] = alpha * l_ref[:, cs] + jnp.sum(p, axis=0, keepdims=True)
    m_ref[:, cs] = m_new
    a_ref[:, cs] = alpha
    p_ref[0:rows, cs] = p.astype(BF16)


def _values_t(vt, p_ref, a_ref, acc_ref, cs):
    pv = jnp.dot(vt, p_ref[0:vt.shape[1], cs], preferred_element_type=F32)
    acc_ref[:, cs] = a_ref[:, cs] * acc_ref[:, cs] + pv


def _attend_pipelined_t(q_scr, kc_ref, vct_ref, k_ref, vt_ref, tk, s_bufs, x_bufs, p_bufs, a_bufs,
                        m_ref, l_ref, acc_ref):
    n_lat = k_ref.shape[0] // tk
    n_ctx = kc_ref.shape[1]
    assert n_ctx <= tk and n_lat % 2 == 0 and n_lat >= 4

    def k_lat(j):
        return k_ref[pl.ds(pl.multiple_of(j * tk, tk), tk), :]

    def v_lat(j):
        return vt_ref[:, pl.ds(pl.multiple_of(j * tk, tk), tk)]

    n = q_scr.shape[1]
    groups = [slice(c, c + MXU_COLS) for c in range(0, n, MXU_COLS)]

    def step(par, k=None, vt=None, probs=tk):
        for cs in groups:
            if probs:
                _probs_t(probs, s_bufs[par], x_bufs[par], p_bufs[par], a_bufs[par], m_ref, l_ref, cs)
            if vt is not None:
                _values_t(vt, p_bufs[1 - par], a_bufs[1 - par], acc_ref, cs)
            if k is not None:
                _scores_t(k, q_scr, s_bufs[1 - par], x_bufs[1 - par], cs)

    step(1, k=kc_ref[0], probs=0)
    step(0, k=k_lat(0), probs=n_ctx)
    step(1, k=k_lat(1), vt=vct_ref[0])

    def pair(jj, carry):
        j = 2 * jj
        step(0, k=k_lat(j + 2), vt=v_lat(j))
        step(1, k=k_lat(j + 3), vt=v_lat(j + 1))
        return carry

    lax.fori_loop(0, n_lat // 2 - 1, pair, 0)
    step(0, vt=v_lat(n_lat - 2))
    step(1, vt=v_lat(n_lat - 1), probs=0)


def _gqa_queries(q3, lo):
    slabs = [q3[:, LANES * s:LANES * (s + 1)] for s in range(3)]
    zero = jnp.zeros_like(slabs[0])
    parts = [jnp.where(lo, s, zero) for s in slabs] + [jnp.where(lo, zero, s) for s in slabs]
    return jnp.concatenate(parts, axis=0)


def _gqa_outputs(o, rows, lo):
    return [jnp.where(lo, o[s * rows:(s + 1) * rows], o[(3 + s) * rows:(4 + s) * rows]) for s in range(3)]


def _diff_queries(q, lane):
    zero = jnp.zeros_like(q)
    parts = []
    for j in range(4):
        sel = (lane >= C_QK_DIM * j) & (lane < C_QK_DIM * (j + 1))
        parts.append(jnp.where(sel, q, zero))
    return jnp.concatenate(parts, axis=0)


def _diff_lambda(lam_ref, lam_init):
    f = lambda a, b: jnp.exp(jnp.sum(a * b, axis=-1, keepdims=True))
    return f(lam_ref[0:1], lam_ref[1:2]) - f(lam_ref[2:3], lam_ref[3:4]) + lam_init


def _diff_output(o, rows, lam, gsub, lam_init, lo):
    o_even = o[0:rows] - lam * o[rows:2 * rows]
    o_odd = o[2 * rows:3 * rows] - lam * o[3 * rows:4 * rows]
    oc = jnp.where(lo, o_even, o_odd)
    return _head_rms(oc, gsub, lo) * (1.0 - lam_init)


def _sink_init(sink_ref, m_ref, l_ref, acc_ref, rows):
    for h in range(B_HEADS):
        m_ref[h * rows:(h + 1) * rows] = jnp.full((rows, LANES), sink_ref[h] * LOG2E, F32)
    l_ref[0:B_HEADS * rows] = jnp.ones((B_HEADS * rows, LANES), F32)
    acc_ref[0:B_HEADS * rows] = jnp.zeros((B_HEADS * rows, LANES), F32)


def _in_proj_kernel(*refs, use_rope, emit_cache, emit_t):
    it = iter(refs)
    x_ref, mod_ref, gn_ref, w_ref, gq_ref, gk_ref = (next(it) for _ in range(6))
    if use_rope:
        cos64, sin64, cos32, sin32 = (next(it)[...] for _ in range(4))
    z_ref = next(it)
    cache_ref = next(it) if emit_cache else None
    zt_ref = next(it) if emit_t else None

    x = x_ref[...]
    mod = mod_ref[0]
    h = _rms(x, gn_ref[...]) * (1.0 + mod[1:2]) + mod[0:1]
    z = jnp.dot(h.astype(BF16), w_ref[...], preferred_element_type=F32)

    rows = x.shape[0]
    lane = lax.broadcasted_iota(jnp.int32, (rows, LANES), 1)
    lo = lane < HEAD_DIM
    first16 = (lane & 31) < 16
    first8 = (lane & 15) < 8
    gq = gq_ref[...]
    gk = gk_ref[...]
    q_scale = HEAD_DIM ** -0.5 * LOG2E
    qc_scale = C_QK_DIM ** -0.5 * LOG2E

    def rope64(v):
        return _rope(v, cos64, sin64, 16, first16) if use_rope else v

    def rope32(v):
        return _rope(v, cos32, sin32, 8, first8) if use_rope else v

    def slab(off):
        return z[:, off:off + LANES]

    def put(off, v):
        z_ref[:, off:off + LANES] = v.astype(BF16)

    def put_cache(off, v):
        if emit_cache:
            cache_ref[:, off - Z_KA:off - Z_KA + LANES] = v

    def put_t(off, v):
        if emit_t:
            zt_ref[off:off + LANES, :] = v.T.astype(BF16)

    for s in range(3):
        qa = rope64(_head_rms(slab(Z_QA + LANES * s), gq, lo)) * q_scale
        put(Z_QA + LANES * s, qa)
        put_t(T_QA + LANES * s, qa)
        put(Z_QB + LANES * s, rope64(slab(Z_QB + LANES * s)) * q_scale)
    ka = _head_rms(slab(Z_KA), gk, lo)
    put_cache(Z_KA, ka)
    put(Z_KA, rope64(ka))
    kb = slab(Z_KB)
    put_cache(Z_KB, kb)
    put(Z_KB, rope64(kb))
    for off in (Z_VA, Z_VB, Z_VC, Z_VC + LANES):
        put_cache(off, slab(off))
        put(off, slab(off))
    put_t(T_VA, slab(Z_VA))
    for s in range(2):
        put_t(T_VC + LANES * s, slab(Z_VC + LANES * s))
        qc = rope32(slab(Z_QC + LANES * s)) * qc_scale
        put(Z_QC + LANES * s, qc)
        put_t(T_QC + LANES * s, qc)
        kc = slab(Z_KC + LANES * s)
        put_cache(Z_KC + LANES * s, kc)
        put(Z_KC + LANES * s, rope32(kc))


def _in_proj(x, mod_l, gn, w_in_p, gq, gk, rope_tabs, *, tm, tiles_per_cond, tiles_per_seq, emit_cache, emit_t):
    t = x.shape[0]
    use_rope = rope_tabs is not None
    in_specs = [
        pl.BlockSpec((tm, D_MODEL), lambda i: (i, 0)),
        pl.BlockSpec((1, 6, D_MODEL), lambda i: (i // tiles_per_cond, 0, 0)),
        _const_spec((1, D_MODEL)),
        _const_spec((D_MODEL, D_IN)),
        _const_spec((1, LANES)),
        _const_spec((1, LANES)),
    ]
    args = [x, mod_l, gn, w_in_p, gq, gk]
    if use_rope:
        in_specs += [pl.BlockSpec((tm, LANES), lambda i: (i % tiles_per_seq, 0))] * 4
        args += list(rope_tabs)
    out_shape = [jax.ShapeDtypeStruct((t, D_IN), BF16)]
    out_specs = [pl.BlockSpec((tm, D_IN), lambda i: (i, 0))]
    if emit_cache:
        out_shape.append(jax.ShapeDtypeStruct((t, D_IN - Z_KA), F32))
        out_specs.append(pl.BlockSpec((tm, D_IN - Z_KA), lambda i: (i, 0)))
    if emit_t:
        out_shape.append(jax.ShapeDtypeStruct((T_ROWS, t), BF16))
        out_specs.append(pl.BlockSpec((T_ROWS, tm), lambda i: (0, i)))
    return pl.pallas_call(
        functools.partial(_in_proj_kernel, use_rope=use_rope, emit_cache=emit_cache, emit_t=emit_t),
        grid=(t // tm,),
        in_specs=in_specs,
        out_specs=out_specs,
        out_shape=out_shape,
        compiler_params=_cparams(("arbitrary",)),
        name="in_proj",
    )(*args)


def _attn_ctx_kernel(sink_ref, lam_ref, gsub_ref, z_ref, o_ref, m_ref, l_ref, acc_ref, *, lam_init):
    rows = z_ref.shape[0]
    lane = lax.broadcasted_iota(jnp.int32, (rows, LANES), 1)
    lo = lane < HEAD_DIM
    refs = (m_ref, l_ref, acc_ref)

    def piece(off, width=LANES):
        return z_ref[:, off:off + width]

    n = A_HEADS * rows
    _softmax_init(*refs, n)
    _softmax_update(_gqa_queries(piece(Z_QA, 384), lo), piece(Z_KA), piece(Z_VA), *refs, n)
    for s, o in enumerate(_gqa_outputs(_softmax_result(l_ref, acc_ref, n), rows, lo)):
        o_ref[:, LANES * s:LANES * (s + 1)] = o.astype(BF16)

    _sink_init(sink_ref, *refs, rows)
    _softmax_update(_gqa_queries(piece(Z_QB, 384), lo), piece(Z_KB), piece(Z_VB), *refs, n)
    for s, o in enumerate(_gqa_outputs(_softmax_result(l_ref, acc_ref, n), rows, lo)):
        o_ref[:, 384 + LANES * s:384 + LANES * (s + 1)] = o.astype(BF16)

    lam = _diff_lambda(lam_ref, lam_init)
    n = 4 * rows
    for s in range(2):
        _softmax_init(*refs, n)
        _softmax_update(_diff_queries(piece(Z_QC + LANES * s), lane), piece(Z_KC + LANES * s),
                        piece(Z_VC + LANES * s), *refs, n)
        oc = _diff_output(_softmax_result(l_ref, acc_ref, n), rows, lam, gsub_ref[...], lam_init, lo)
        o_ref[:, 768 + LANES * s:768 + LANES * (s + 1)] = oc.astype(BF16)


def _attn_ctx(z, sink, lam_par, gsub, *, seq, lam_init):
    t = z.shape[0]
    return pl.pallas_call(
        functools.partial(_attn_ctx_kernel, lam_init=lam_init),
        grid=(t // seq,),
        in_specs=[
            pl.BlockSpec(memory_space=pltpu.SMEM),
            _const_spec((8, LANES)),
            _const_spec((1, LANES)),
            pl.BlockSpec((seq, D_IN), lambda b: (b, 0)),
        ],
        out_specs=pl.BlockSpec((seq, D_MODEL), lambda b: (b, 0)),
        out_shape=jax.ShapeDtypeStruct((t, D_MODEL), BF16),
        scratch_shapes=[pltpu.VMEM((A_HEADS * seq, LANES), F32)] * 3,
        compiler_params=_cparams(("arbitrary",)),
        name="attn_ctx",
    )(sink, lam_par, gsub, z)


def _attn_a_kernel(qt_ref, kc_ref, vct_ref, k_ref, vt_ref, o_ref, q_scr, m_ref, l_ref, acc_ref,
                   s0, s1, x0, x1, p0, p1, a0, a1, *, tk):
    tq = qt_ref.shape[1]
    lo = lax.broadcasted_iota(jnp.int32, (LANES, tq), 0) < HEAD_DIM
    refs = (m_ref, l_ref, acc_ref)
    slabs = [qt_ref[LANES * s:LANES * (s + 1), :] for s in range(3)]
    zero = jnp.zeros_like(slabs[0])
    q_scr[...] = jnp.concatenate([jnp.where(lo, s, zero) for s in slabs] + [jnp.where(lo, zero, s) for s in slabs],
                                 axis=1)
    _softmax_init_t(*refs)
    _attend_pipelined_t(q_scr, kc_ref, vct_ref, k_ref, vt_ref, tk, (s0, s1), (x0, x1), (p0, p1), (a0, a1), *refs)
    ot = acc_ref[...] * (1.0 / l_ref[...])
    for s in range(3):
        slab_t = jnp.where(lo, ot[:, s * tq:(s + 1) * tq], ot[:, (3 + s) * tq:(4 + s) * tq])
        o_ref[:, LANES * s:LANES * (s + 1)] = slab_t.T.astype(BF16)


def _keys_major_scratch(n, tk):
    return [pltpu.VMEM((LANES, n), BF16), pltpu.VMEM((1, n), F32), pltpu.VMEM((1, n), F32),
            pltpu.VMEM((LANES, n), F32),
            pltpu.VMEM((tk, n), F32), pltpu.VMEM((tk, n), F32),
            pltpu.VMEM((1, n), F32), pltpu.VMEM((1, n), F32),
            pltpu.VMEM((tk, n), BF16), pltpu.VMEM((tk, n), BF16),
            pltpu.VMEM((1, n), F32), pltpu.VMEM((1, n), F32)]


def _attn_a(z, zt, k_ctx, vt_ctx, *, seq, tq, tk):
    t = z.shape[0]
    nq = seq // tq
    n_ctx = k_ctx.shape[1]
    n = A_HEADS * tq
    return pl.pallas_call(
        functools.partial(_attn_a_kernel, tk=tk),
        grid=(t // seq, nq),
        in_specs=[
            pl.BlockSpec((384, tq), lambda b, i: (T_QA // 384, b * nq + i)),
            pl.BlockSpec((1, n_ctx, LANES), lambda b, i: (b, 0, 0)),
            pl.BlockSpec((1, LANES, n_ctx), lambda b, i: (b, 0, 0)),
            pl.BlockSpec((seq, LANES), lambda b, i: (b, Z_KA // LANES)),
            pl.BlockSpec((LANES, seq), lambda b, i: (T_VA // LANES, b)),
        ],
        out_specs=pl.BlockSpec((tq, 384), lambda b, i: (b * nq + i, 0)),
        out_shape=jax.ShapeDtypeStruct((t, D_MODEL), BF16),
        scratch_shapes=_keys_major_scratch(n, tk),
        compiler_params=_cparams(("arbitrary", "arbitrary")),
        name="attn_a",
    )(zt, k_ctx, vt_ctx, z, zt)


def _attn_b_kernel(sink_ref, q_ref, kc_ref, vc_ref, k_ref, v_ref, o_in_ref, o_ref, m_ref, l_ref, acc_ref):
    del o_in_ref
    rows = q_ref.shape[0]
    seq = k_ref.shape[0]
    n = B_HEADS * rows
    band = rows + 2 * WINDOW
    i = pl.program_id(1)
    lo = lax.broadcasted_iota(jnp.int32, (rows, LANES), 1) < HEAD_DIM
    refs = (m_ref, l_ref, acc_ref)
    q = _gqa_queries(q_ref[...], lo)
    _sink_init(sink_ref, *refs, rows)
    _softmax_update(q, kc_ref[0], vc_ref[0], *refs, n)
    start = pl.multiple_of(jnp.clip(i * rows - WINDOW, 0, seq - band), LANES)
    qpos = i * rows + (lax.broadcasted_iota(jnp.int32, (n, band), 0) & (rows - 1))
    kpos = start + lax.broadcasted_iota(jnp.int32, (n, band), 1)
    mask = jnp.abs(kpos - qpos) <= WINDOW
    _softmax_update(q, k_ref[pl.ds(start, band), :], v_ref[pl.ds(start, band), :], *refs, n, mask=mask)
    for s, o in enumerate(_gqa_outputs(_softmax_result(l_ref, acc_ref, n), rows, lo)):
        o_ref[:, LANES * s:LANES * (s + 1)] = o.astype(BF16)


def _attn_b(z, k_ctx, v_ctx, sink, o_buf, *, seq, tq):
    t = z.shape[0]
    nq = seq // tq
    n_ctx = k_ctx.shape[1]
    return pl.pallas_call(
        _attn_b_kernel,
        grid=(t // seq, nq),
        in_specs=[
            pl.BlockSpec(memory_space=pltpu.SMEM),
            pl.BlockSpec((tq, 384), lambda b, i: (b * nq + i, Z_QB // 384)),
            pl.BlockSpec((1, n_ctx, LANES), lambda b, i: (b, 0, 0)),
            pl.BlockSpec((1, n_ctx, LANES), lambda b, i: (b, 0, 0)),
            pl.BlockSpec((seq, LANES), lambda b, i: (b, Z_KB // LANES)),
            pl.BlockSpec((seq, LANES), lambda b, i: (b, Z_VB // LANES)),
            pl.BlockSpec(memory_space=pl.ANY),
        ],
        out_specs=pl.BlockSpec((tq, 384), lambda b, i: (b * nq + i, 1)),
        out_shape=jax.ShapeDtypeStruct((t, D_MODEL), BF16),
        input_output_aliases={6: 0},
        scratch_shapes=[pltpu.VMEM((B_HEADS * tq, LANES), F32)] * 3,
        compiler_params=_cparams(("arbitrary", "arbitrary")),
        name="attn_b",
    )(sink, z, k_ctx, v_ctx, z, z, o_buf)


def _attn_c_kernel(lam_ref, gsub_ref, qt_ref, kc_ref, vct_ref, k_ref, vt_ref, o_in_ref, o_ref,
                   q_scr, m_ref, l_ref, acc_ref, s0, s1, x0, x1, p0, p1, a0, a1, *, tk, lam_init):
    del o_in_ref
    tq = qt_ref.shape[1]
    row = lax.broadcasted_iota(jnp.int32, (LANES, tq), 0)
    refs = (m_ref, l_ref, acc_ref)
    qt = qt_ref[...]
    zero = jnp.zeros_like(qt)
    parts = [jnp.where((row >= C_QK_DIM * j) & (row < C_QK_DIM * (j + 1)), qt, zero) for j in range(4)]
    q_scr[...] = jnp.concatenate(parts, axis=1)
    _softmax_init_t(*refs)
    _attend_pipelined_t(q_scr, kc_ref, vct_ref, k_ref, vt_ref, tk, (s0, s1), (x0, x1), (p0, p1), (a0, a1), *refs)
    lam = _diff_lambda(lam_ref, lam_init)
    ot = acc_ref[...] * (1.0 / l_ref[...])
    o_even = ot[:, 0:tq] - lam * ot[:, tq:2 * tq]
    o_odd = ot[:, 2 * tq:3 * tq] - lam * ot[:, 3 * tq:4 * tq]
    oc = jnp.where(row < C_V_DIM, o_even, o_odd).T
    lo = lax.broadcasted_iota(jnp.int32, (tq, LANES), 1) < C_V_DIM
    o_ref[...] = (_head_rms(oc, gsub_ref[...], lo) * (1.0 - lam_init)).astype(BF16)


def _attn_c(z, zt, k_ctx, vt_ctx, lam_par, gsub, o_buf, *, seq, tq, tk, lam_init):
    t = z.shape[0]
    nq = seq // tq
    n_ctx = k_ctx.shape[1]
    n = 4 * tq
    return pl.pallas_call(
        functools.partial(_attn_c_kernel, tk=tk, lam_init=lam_init),
        grid=(t // seq, 2, nq),
        in_specs=[
            _const_spec((8, LANES)),
            _const_spec((1, LANES)),
            pl.BlockSpec((LANES, tq), lambda b, s, i: (T_QC // LANES + s, b * nq + i)),
            pl.BlockSpec((1, n_ctx, LANES), lambda b, s, i: (b, 0, s)),
            pl.BlockSpec((1, LANES, n_ctx), lambda b, s, i: (b, s, 0)),
            pl.BlockSpec((seq, LANES), lambda b, s, i: (b, Z_KC // LANES + s)),
            pl.BlockSpec((LANES, seq), lambda b, s, i: (T_VC // LANES + s, b)),
            pl.BlockSpec(memory_space=pl.ANY),
        ],
        out_specs=pl.BlockSpec((tq, LANES), lambda b, s, i: (b * nq + i, 768 // LANES + s)),
        out_shape=jax.ShapeDtypeStruct((t, D_MODEL), BF16),
        input_output_aliases={7: 0},
        scratch_shapes=_keys_major_scratch(n, tk),
        compiler_params=_cparams(("arbitrary", "arbitrary", "arbitrary")),
        name="attn_c",
    )(lam_par, gsub, zt, k_ctx, vt_ctx, z, zt, o_buf)


def _post_kernel(*refs, halo, tiles_per_seq, final):
    it = iter(refs)
    x_ref, o_ref = next(it), next(it)
    if halo:
        xp_ref, xn_ref, op_ref, on_ref = (next(it) for _ in range(4))
    (mod_ref, gn_ref, wo_ref, wu_ref, cw_ref, cb_ref, wd_ref, gf_ref, out_ref) = (next(it) for _ in range(9))

    tm = x_ref.shape[0]
    if halo:
        x = jnp.concatenate([xp_ref[...], x_ref[...], xn_ref[...]], axis=0)
        o = jnp.concatenate([op_ref[...], o_ref[...], on_ref[...]], axis=0)
    else:
        x, o = x_ref[...], o_ref[...]
    ext = x.shape[0]
    mod = mod_ref[0]
    x1 = x + mod[2:3] * jnp.dot(o, wo_ref[...], preferred_element_type=F32)
    h = _rms(x1, gn_ref[...]) * (1.0 + mod[4:5]) + mod[3:4]
    row = lax.broadcasted_iota(jnp.int32, (ext, 1), 0)
    if halo:
        t_in_seq = pl.program_id(0) % tiles_per_seq
        keep = ((row >= halo) | (t_in_seq > 0)) & ((row < halo + tm) | (t_in_seq < tiles_per_seq - 1))
        h = jnp.where(keep, h, 0.0)
    h = h.astype(BF16)

    def conv(u, c0):
        cw = cw_ref[:, c0:c0 + FFN_CHUNK]
        up = pltpu.roll(u, 1, 0)
        dn = pltpu.roll(u, ext - 1, 0)
        if not halo:
            up = jnp.where(row == 0, 0.0, up)
            dn = jnp.where(row == ext - 1, 0.0, dn)
        v = cw[0:1] * up + cw[1:2] * u + cw[2:3] * dn + cb_ref[:, c0:c0 + FFN_CHUNK]
        return v[halo:halo + tm]

    acc = jnp.zeros((tm, D_MODEL), F32)
    for c in range(D_FF // FFN_CHUNK):
        ca, cg = c * FFN_CHUNK, D_FF + c * FFN_CHUNK
        a = conv(jnp.dot(h, wu_ref[:, ca:ca + FFN_CHUNK], preferred_element_type=F32), ca)
        g = conv(jnp.dot(h, wu_ref[:, cg:cg + FFN_CHUNK], preferred_element_type=F32), cg)
        act = a / (1.0 + jnp.exp(-a)) * g
        acc = acc + jnp.dot(act.astype(BF16), wd_ref[ca:ca + FFN_CHUNK, :], preferred_element_type=F32)
    x2 = x1[halo:halo + tm] + mod[5:6] * acc
    if final:
        x2 = _rms(x2, gf_ref[...])
    out_ref[...] = x2


def _post(x, o, mod_l, gn2, wo, wu, cw, cb, wd, gf, *, tm, tiles_per_cond, tiles_per_seq, final):
    t = x.shape[0]
    halo = HALO if tiles_per_seq > 1 else 0
    tile = lambda i: (i, 0)
    in_specs = [pl.BlockSpec((tm, D_MODEL), tile), pl.BlockSpec((tm, D_MODEL), tile)]
    args = [x, o]
    if halo:
        per = tm // halo
        prev = lambda i: (jnp.maximum(i * per - 1, 0), 0)
        nxt = lambda i: (jnp.minimum((i + 1) * per, t // halo - 1), 0)
        in_specs += [pl.BlockSpec((halo, D_MODEL), prev), pl.BlockSpec((halo, D_MODEL), nxt)] * 2
        args += [x, x, o, o]
    in_specs += [
        pl.BlockSpec((1, 6, D_MODEL), lambda i: (i // tiles_per_cond, 0, 0)),
        _const_spec((1, D_MODEL)),
        _const_spec((D_MODEL, D_MODEL)),
        _const_spec((D_MODEL, 2 * D_FF)),
        _const_spec((3, 2 * D_FF)),
        _const_spec((1, 2 * D_FF)),
        _const_spec((D_FF, D_MODEL)),
        _const_spec((1, D_MODEL)),
    ]
    args += [mod_l, gn2, wo, wu, cw, cb, wd, gf]
    return pl.pallas_call(
        functools.partial(_post_kernel, halo=halo, tiles_per_seq=tiles_per_seq, final=final),
        grid=(t // tm,),
        in_specs=in_specs,
        out_specs=pl.BlockSpec((tm, D_MODEL), tile),
        out_shape=jax.ShapeDtypeStruct((t, D_MODEL), F32),
        compiler_params=_cparams(("arbitrary",)),
        name="post",
    )(*args)


def _rope_tables(seq):
    t = jnp.arange(seq)
    rows = (t // GRID_W).astype(F32)[:, None]
    cols = (t % GRID_W).astype(F32)[:, None]

    def tab(half, reps):
        inv = ROPE_THETA ** (-jnp.arange(half, dtype=F32) / half)
        ar, ac = rows * inv[None, :], cols * inv[None, :]
        cos = jnp.concatenate([jnp.cos(ar), jnp.cos(ar), jnp.cos(ac), jnp.cos(ac)], axis=-1)
        sin = jnp.concatenate([-jnp.sin(ar), jnp.sin(ar), -jnp.sin(ac), jnp.sin(ac)], axis=-1)
        return jnp.tile(cos, (1, reps)), jnp.tile(sin, (1, reps))

    cos64, sin64 = tab(HEAD_DIM // 4, LANES // HEAD_DIM)
    cos32, sin32 = tab(C_QK_DIM // 4, LANES // C_QK_DIM)
    return cos64, sin64, cos32, sin32


def _w_in_perm():
    sizes = (A_HEADS * HEAD_DIM, A_KV * HEAD_DIM, A_KV * HEAD_DIM, B_HEADS * HEAD_DIM, B_KV * HEAD_DIM,
             B_KV * HEAD_DIM, C_HEADS * 2 * C_QK_DIM, C_HEADS * 2 * C_QK_DIM, C_HEADS * C_V_DIM)
    offs = np.concatenate([[0], np.cumsum(sizes)])
    qa, ka, va, qb, kb, vb, qc, kc, vc = (np.arange(offs[j], offs[j + 1]) for j in range(9))
    pair = np.concatenate([np.arange(h * HEAD_DIM, (h + 1) * HEAD_DIM) for h in PAIRED_HEADS])
    return np.concatenate([qa[pair], qb[pair], qc, ka, va, kb, vb, kc, vc])


def _w_out_perm():
    pair = np.concatenate([np.arange(h * HEAD_DIM, (h + 1) * HEAD_DIM) for h in PAIRED_HEADS])
    return np.concatenate([pair, A_HEADS * HEAD_DIM + pair, np.arange(768, D_MODEL)])


def kernel(x_prompt, x_sample, cache_a_k, cache_a_v, cache_b_k, cache_b_v, cache_c_k, cache_c_v, c, c_ctx, w_ada, b_ada, g_norm1, g_norm2, w_in, g_qa, g_ka, sink_b, lam_q1, lam_k1, lam_q2, lam_k2, g_subln, w_out, w_up, conv_w, conv_b, w_down, g_final):
    n_ctx_req, ctx_len, _ = x_prompt.shape
    n_lat_req, lat_len, _ = x_sample.shape
    past = cache_a_k.shape[2]

    conds = jnp.zeros((8, D_MODEL), F32).at[0].set(c_ctx).at[1:1 + n_lat_req].set(c)
    mod = _modulation(conds, w_ada, b_ada).reshape(DEPTH, 8, 6, D_MODEL)

    w_in_p = w_in[:, :, _w_in_perm()].astype(BF16)
    w_out_p = w_out[:, _w_out_perm(), :].astype(BF16)
    w_up_b = w_up.astype(BF16)
    w_down_b = w_down.astype(BF16)
    rope_tabs = _rope_tables(lat_len)
    zpad = jnp.zeros((DEPTH, LANES - C_QK_DIM), F32)
    lam_rows = [jnp.concatenate([v, zpad], axis=-1) for v in (lam_q1, lam_k1, lam_q2, lam_k2)]
    lam_par = jnp.concatenate([jnp.stack(lam_rows, axis=1), jnp.zeros((DEPTH, 4, LANES), F32)], axis=1)
    gf = g_final.reshape(1, D_MODEL)

    xc = x_prompt.reshape(n_ctx_req * ctx_len, D_MODEL)
    xs = x_sample.reshape(n_lat_req * lat_len, D_MODEL)
    lat_tm = 512
    caches = []
    for l in range(DEPTH):
        lam_init = 0.8 - 0.6 * math.exp(-0.3 * l)
        gn1 = g_norm1[l].reshape(1, D_MODEL)
        gn2 = g_norm2[l].reshape(1, D_MODEL)
        gq = jnp.tile(g_qa[l], LANES // HEAD_DIM).reshape(1, LANES)
        gk = jnp.tile(g_ka[l], LANES // HEAD_DIM).reshape(1, LANES)
        gsub = jnp.tile(g_subln[l], LANES // C_V_DIM).reshape(1, LANES)
        post_w = (gn2, w_out_p[l], w_up_b[l], conv_w[l], conv_b[l].reshape(1, 2 * D_FF), w_down_b[l], gf)
        final = l == DEPTH - 1

        n_tiles = n_ctx_req
        zc, cache = _in_proj(xc, mod[l, 0:1], gn1, w_in_p[l], gq, gk, None, tm=ctx_len, tiles_per_cond=n_tiles,
                             tiles_per_seq=1, emit_cache=True, emit_t=False)
        caches.append(cache)
        oc = _attn_ctx(zc, sink_b[l], lam_par[l], gsub, seq=ctx_len, lam_init=lam_init)
        xc = _post(xc, oc, mod[l, 0:1], *post_w, tm=ctx_len, tiles_per_cond=n_tiles, tiles_per_seq=1, final=final)

        per_seq = lat_len // lat_tm
        zs, zts = _in_proj(xs, mod[l, 1:1 + n_lat_req], gn1, w_in_p[l], gq, gk, rope_tabs, tm=lat_tm,
                           tiles_per_cond=per_seq, tiles_per_seq=per_seq, emit_cache=False, emit_t=True)
        flat = lambda a: a[:, l].reshape(n_lat_req, past, -1).astype(BF16)
        flat_t = lambda a: jnp.swapaxes(flat(a), 1, 2)
        os_ = _attn_a(zs, zts, flat(cache_a_k), flat_t(cache_a_v), seq=lat_len, tq=128, tk=512)
        os_ = _attn_b(zs, flat(cache_b_k), flat(cache_b_v), sink_b[l], os_, seq=lat_len, tq=128)
        os_ = _attn_c(zs, zts, flat(cache_c_k), flat_t(cache_c_v), lam_par[l], gsub, os_, seq=lat_len, tq=256,
                      tk=512, lam_init=lam_init)
        xs = _post(xs, os_, mod[l, 1:1 + n_lat_req], *post_w, tm=lat_tm, tiles_per_cond=per_seq,
                   tiles_per_seq=per_seq, final=final)

    def stack(lo, hi, heads):
        parts = [cch[:, lo - Z_KA:hi - Z_KA].reshape(n_ctx_req, ctx_len, heads, -1) for cch in caches]
        return jnp.stack(parts, axis=1)

    y_prompt = xc.reshape(x_prompt.shape)
    y_sample = xs.reshape(x_sample.shape)
    return (y_prompt, y_sample,
            stack(Z_KA, Z_VA, A_KV), stack(Z_VA, Z_KB, A_KV), stack(Z_KB, Z_VB, B_KV), stack(Z_VB, Z_KC, B_KV),
            stack(Z_KC, Z_VC, C_HEADS), stack(Z_VC, D_IN, C_HEADS))
```

```python
import functools
import math

import numpy as np
import jax
import jax.numpy as jnp
from jax import lax
from jax.experimental import pallas as pl
from jax.experimental.pallas import tpu as pltpu

D_MODEL = 1024
DEPTH = 2
GRID_W = 64
HEAD_DIM = 64
A_HEADS = 6
A_KV = 2
B_HEADS = 6
B_KV = 2
C_HEADS = 4
C_QK_DIM = 32
C_V_DIM = 2 * C_QK_DIM
WINDOW = 128
ROPE_THETA = 10000.0
D_FF = 2816
EPS = 1e-6
NEG = -1e30
LOG2E = math.log2(math.e)

LANES = 128
BF16_ROWS = 16
MXU_COLS = 256
VMEM_LIMIT = 56 * 1024 * 1024

Z_QA, Z_QB, Z_QC = 0, 384, 768
Z_KA, Z_VA, Z_KB, Z_VB, Z_KC, Z_VC = 1024, 1152, 1280, 1408, 1536, 1792
D_IN = 2048
T_QA, T_VA, T_QC, T_VC, T_VB, T_QB, T_ROWS = 0, 384, 512, 768, 1024, 1152, 1536
PAIRED_HEADS = (0, 3, 1, 4, 2, 5)
FFN_CHUNK = 256
HALO = BF16_ROWS

F32 = jnp.float32
BF16 = jnp.bfloat16


def _cparams(sem):
    return pltpu.CompilerParams(dimension_semantics=sem, vmem_limit_bytes=VMEM_LIMIT)


def _const_spec(shape):
    nd = len(shape)
    return pl.BlockSpec(shape, lambda *_: (0,) * nd)


def _mod_kernel(c_ref, w_ref, b_ref, o_ref):
    cond = c_ref[...]
    a = cond / (1.0 + jnp.exp(-cond))
    o_ref[0] = jnp.dot(a.astype(BF16), w_ref[0].astype(BF16), preferred_element_type=F32) + b_ref[0]


def _modulation(conds, w_ada, b_ada):
    nb = 1536
    n_out = w_ada.shape[-1]
    return pl.pallas_call(
        _mod_kernel,
        grid=(DEPTH, n_out // nb),
        in_specs=[
            pl.BlockSpec((8, D_MODEL), lambda l, j: (0, 0)),
            pl.BlockSpec((1, D_MODEL, nb), lambda l, j: (l, 0, j)),
            pl.BlockSpec((1, 1, nb), lambda l, j: (l, 0, j)),
        ],
        out_specs=pl.BlockSpec((1, 8, nb), lambda l, j: (l, 0, j)),
        out_shape=jax.ShapeDtypeStruct((DEPTH, 8, n_out), F32),
        compiler_params=_cparams(("arbitrary", "arbitrary")),
        name="modulation",
    )(conds, w_ada, b_ada.reshape(DEPTH, 1, n_out))


def _rms(x, g):
    ms = jnp.mean(x * x, axis=-1, keepdims=True)
    return x * lax.rsqrt(ms + EPS) * g


def _head_rms(x, g, lo):
    ss = x * x
    s_lo = jnp.sum(jnp.where(lo, ss, 0.0), axis=-1, keepdims=True)
    s_hi = jnp.sum(jnp.where(lo, 0.0, ss), axis=-1, keepdims=True)
    inv = jnp.where(lo, lax.rsqrt(s_lo * (1.0 / HEAD_DIM) + EPS), lax.rsqrt(s_hi * (1.0 / HEAD_DIM) + EPS))
    return x * inv * g


def _rope(x, cos, sin, chunk, first):
    sw = jnp.where(first, pltpu.roll(x, LANES - chunk, 1), pltpu.roll(x, chunk, 1))
    return x * cos + sw * sin


def _softmax_init(m_ref, l_ref, acc_ref, rows):
    m_ref[0:rows] = jnp.full((rows, LANES), NEG, F32)
    l_ref[0:rows] = jnp.zeros((rows, LANES), F32)
    acc_ref[0:rows] = jnp.zeros((rows, LANES), F32)


def _softmax_update(q, k, v, m_ref, l_ref, acc_ref, rows, mask=None):
    s = lax.dot_general(q, k, (((1,), (1,)), ((), ())), preferred_element_type=F32)
    if mask is not None:
        s = jnp.where(mask, s, NEG)
    m_prev = m_ref[0:rows]
    m_new = jnp.maximum(m_prev, jnp.max(s, axis=1, keepdims=True))
    alpha = jnp.exp2(m_prev - m_new)
    p = jnp.exp2(s - m_new[:, 0:1])
    l_ref[0:rows] = alpha * l_ref[0:rows] + jnp.sum(p, axis=1, keepdims=True)
    acc_ref[0:rows] = alpha * acc_ref[0:rows] + jnp.dot(p.astype(BF16), v, preferred_element_type=F32)
    m_ref[0:rows] = m_new


def _softmax_result(l_ref, acc_ref, rows):
    return acc_ref[0:rows] * (1.0 / l_ref[0:rows])


def _softmax_init_t(m_ref, l_ref, acc_ref):
    m_ref[...] = jnp.full(m_ref.shape, NEG, F32)
    l_ref[...] = jnp.zeros(l_ref.shape, F32)
    acc_ref[...] = jnp.zeros(acc_ref.shape, F32)


def _softmax_update_t(qt, k, vt, m_ref, l_ref, acc_ref):
    s = jnp.dot(k, qt, preferred_element_type=F32)
    m_prev = m_ref[...]
    m_new = jnp.maximum(m_prev, jnp.max(s, axis=0, keepdims=True))
    alpha = jnp.exp2(m_prev - m_new)
    p = jnp.exp2(s - m_new)
    l_ref[...] = alpha * l_ref[...] + jnp.sum(p, axis=0, keepdims=True)
    acc_ref[...] = alpha * acc_ref[...] + jnp.dot(vt, p.astype(BF16), preferred_element_type=F32)
    m_ref[...] = m_new


def _scores_t(k, q_scr, s_ref, mx_ref, cs):
    s = jnp.dot(k, q_scr[:, cs], preferred_element_type=F32)
    s_ref[0:k.shape[0], cs] = s
    mx_ref[:, cs] = jnp.max(s, axis=0, keepdims=True)


def _probs_t(rows, s_ref, mx_ref, p_ref, a_ref, m_ref, l_ref, cs):
    s = s_ref[0:rows, cs]
    m_prev = m_ref[:, cs]
    m_new = jnp.maximum(m_prev, mx_ref[:, cs])
    alpha = jnp.exp2(m_prev - m_new)
    p = jnp.exp2(s - m_new)
    l_ref[:, cs] = alpha * l_ref[:, cs] + jnp.sum(p, axis=0, keepdims=True)
    m_ref[:, cs] = m_new
    a_ref[:, cs] = alpha
    p_ref[0:rows, cs] = p.astype(BF16)


def _values_t(vt, p_ref, a_ref, acc_ref, cs):
    pv = jnp.dot(vt, p_ref[0:vt.shape[1], cs], preferred_element_type=F32)
    acc_ref[:, cs] = a_ref[:, cs] * acc_ref[:, cs] + pv


def _attend_pipelined_t(q_scr, kc_ref, vct_ref, k_ref, vt_ref, tk, s_bufs, x_bufs, p_bufs, a_bufs,
                        m_ref, l_ref, acc_ref):
    n_lat = k_ref.shape[0] // tk
    n_ctx = kc_ref.shape[1]
    assert n_ctx <= tk and n_lat % 2 == 0 and n_lat >= 4

    def k_lat(j):
        return k_ref[pl.ds(pl.multiple_of(j * tk, tk), tk), :]

    def v_lat(j):
        return vt_ref[:, pl.ds(pl.multiple_of(j * tk, tk), tk)]

    n = q_scr.shape[1]
    groups = [slice(c, c + MXU_COLS) for c in range(0, n, MXU_COLS)]

    def step(par, k=None, vt=None, probs=tk):
        for cs in groups:
            if probs:
                _probs_t(probs, s_bufs[par], x_bufs[par], p_bufs[par], a_bufs[par], m_ref, l_ref, cs)
            if vt is not None:
                _values_t(vt, p_bufs[1 - par], a_bufs[1 - par], acc_ref, cs)
            if k is not None:
                _scores_t(k, q_scr, s_bufs[1 - par], x_bufs[1 - par], cs)

    step(1, k=kc_ref[0], probs=0)
    step(0, k=k_lat(0), probs=n_ctx)
    step(1, k=k_lat(1), vt=vct_ref[0])

    def pair(jj, carry):
        j = 2 * jj
        step(0, k=k_lat(j + 2), vt=v_lat(j))
        step(1, k=k_lat(j + 3), vt=v_lat(j + 1))
        return carry

    lax.fori_loop(0, n_lat // 2 - 1, pair, 0)
    step(0, vt=v_lat(n_lat - 2))
    step(1, vt=v_lat(n_lat - 1), probs=0)


def _gqa_queries(q3, lo):
    slabs = [q3[:, LANES * s:LANES * (s + 1)] for s in range(3)]
    zero = jnp.zeros_like(slabs[0])
    parts = [jnp.where(lo, s, zero) for s in slabs] + [jnp.where(lo, zero, s) for s in slabs]
    return jnp.concatenate(parts, axis=0)


def _gqa_outputs(o, rows, lo):
    return [jnp.where(lo, o[s * rows:(s + 1) * rows], o[(3 + s) * rows:(4 + s) * rows]) for s in range(3)]


def _diff_queries(q, lane):
    zero = jnp.zeros_like(q)
    parts = []
    for j in range(4):
        sel = (lane >= C_QK_DIM * j) & (lane < C_QK_DIM * (j + 1))
        parts.append(jnp.where(sel, q, zero))
    return jnp.concatenate(parts, axis=0)


def _diff_lambda(lam_ref, lam_init):
    f = lambda a, b: jnp.exp(jnp.sum(a * b, axis=-1, keepdims=True))
    return f(lam_ref[0:1], lam_ref[1:2]) - f(lam_ref[2:3], lam_ref[3:4]) + lam_init


def _diff_output(o, rows, lam, gsub, lam_init, lo):
    o_even = o[0:rows] - lam * o[rows:2 * rows]
    o_odd = o[2 * rows:3 * rows] - lam * o[3 * rows:4 * rows]
    oc = jnp.where(lo, o_even, o_odd)
    return _head_rms(oc, gsub, lo) * (1.0 - lam_init)


def _sink_init(sink_ref, m_ref, l_ref, acc_ref, rows):
    for h in range(B_HEADS):
        m_ref[h * rows:(h + 1) * rows] = jnp.full((rows, LANES), sink_ref[h] * LOG2E, F32)
    l_ref[0:B_HEADS * rows] = jnp.ones((B_HEADS * rows, LANES), F32)
    acc_ref[0:B_HEADS * rows] = jnp.zeros((B_HEADS * rows, LANES), F32)


def _in_proj_kernel(*refs, use_rope, emit_cache, emit_t):
    it = iter(refs)
    x_ref, mod_ref, gn_ref, w_ref, gq_ref, gk_ref = (next(it) for _ in range(6))
    if use_rope:
        cos64, sin64, cos32, sin32 = (next(it)[...] for _ in range(4))
    z_ref = next(it)
    cache_ref = next(it) if emit_cache else None
    zt_ref = next(it) if emit_t else None

    x = x_ref[...]
    mod = mod_ref[0]
    h = _rms(x, gn_ref[...]) * (1.0 + mod[1:2]) + mod[0:1]
    z = jnp.dot(h.astype(BF16), w_ref[...], preferred_element_type=F32)

    rows = x.shape[0]
    lane = lax.broadcasted_iota(jnp.int32, (rows, LANES), 1)
    lo = lane < HEAD_DIM
    first16 = (lane & 31) < 16
    first8 = (lane & 15) < 8
    gq = gq_ref[...]
    gk = gk_ref[...]
    q_scale = HEAD_DIM ** -0.5 * LOG2E
    qc_scale = C_QK_DIM ** -0.5 * LOG2E

    def rope64(v):
        return _rope(v, cos64, sin64, 16, first16) if use_rope else v

    def rope32(v):
        return _rope(v, cos32, sin32, 8, first8) if use_rope else v

    def slab(off):
        return z[:, off:off + LANES]

    def put(off, v):
        z_ref[:, off:off + LANES] = v.astype(BF16)

    def put_cache(off, v):
        if emit_cache:
            cache_ref[:, off - Z_KA:off - Z_KA + LANES] = v

    def put_t(off, v):
        if emit_t:
            zt_ref[off:off + LANES, :] = v.T.astype(BF16)

    for s in range(3):
        qa = rope64(_head_rms(slab(Z_QA + LANES * s), gq, lo)) * q_scale
        put(Z_QA + LANES * s, qa)
        put_t(T_QA + LANES * s, qa)
        qb = rope64(slab(Z_QB + LANES * s)) * q_scale
        put(Z_QB + LANES * s, qb)
        put_t(T_QB + LANES * s, qb)
    ka = _head_rms(slab(Z_KA), gk, lo)
    put_cache(Z_KA, ka)
    put(Z_KA, rope64(ka))
    kb = slab(Z_KB)
    put_cache(Z_KB, kb)
    put(Z_KB, rope64(kb))
    for off in (Z_VA, Z_VB, Z_VC, Z_VC + LANES):
        put_cache(off, slab(off))
        put(off, slab(off))
    put_t(T_VA, slab(Z_VA))
    put_t(T_VB, slab(Z_VB))
    for s in range(2):
        put_t(T_VC + LANES * s, slab(Z_VC + LANES * s))
        qc = rope32(slab(Z_QC + LANES * s)) * qc_scale
        put(Z_QC + LANES * s, qc)
        put_t(T_QC + LANES * s, qc)
        kc = slab(Z_KC + LANES * s)
        put_cache(Z_KC + LANES * s, kc)
        put(Z_KC + LANES * s, rope32(kc))


def _in_proj(x, mod_l, gn, w_in_p, gq, gk, rope_tabs, *, tm, tiles_per_cond, tiles_per_seq, emit_cache, emit_t):
    t = x.shape[0]
    use_rope = rope_tabs is not None
    in_specs = [
        pl.BlockSpec((tm, D_MODEL), lambda i: (i, 0)),
        pl.BlockSpec((1, 6, D_MODEL), lambda i: (i // tiles_per_cond, 0, 0)),
        _const_spec((1, D_MODEL)),
        _const_spec((D_MODEL, D_IN)),
        _const_spec((1, LANES)),
        _const_spec((1, LANES)),
    ]
    args = [x, mod_l, gn, w_in_p, gq, gk]
    if use_rope:
        in_specs += [pl.BlockSpec((tm, LANES), lambda i: (i % tiles_per_seq, 0))] * 4
        args += list(rope_tabs)
    out_shape = [jax.ShapeDtypeStruct((t, D_IN), BF16)]
    out_specs = [pl.BlockSpec((tm, D_IN), lambda i: (i, 0))]
    if emit_cache:
        out_shape.append(jax.ShapeDtypeStruct((t, D_IN - Z_KA), F32))
        out_specs.append(pl.BlockSpec((tm, D_IN - Z_KA), lambda i: (i, 0)))
    if emit_t:
        out_shape.append(jax.ShapeDtypeStruct((T_ROWS, t), BF16))
        out_specs.append(pl.BlockSpec((T_ROWS, tm), lambda i: (0, i)))
    return pl.pallas_call(
        functools.partial(_in_proj_kernel, use_rope=use_rope, emit_cache=emit_cache, emit_t=emit_t),
        grid=(t // tm,),
        in_specs=in_specs,
        out_specs=out_specs,
        out_shape=out_shape,
        compiler_params=_cparams(("arbitrary",)),
        name="in_proj",
    )(*args)


def _attn_ctx_kernel(sink_ref, lam_ref, gsub_ref, z_ref, o_ref, m_ref, l_ref, acc_ref, *, lam_init):
    rows = z_ref.shape[0]
    lane = lax.broadcasted_iota(jnp.int32, (rows, LANES), 1)
    lo = lane < HEAD_DIM
    refs = (m_ref, l_ref, acc_ref)

    def piece(off, width=LANES):
        return z_ref[:, off:off + width]

    n = A_HEADS * rows
    _softmax_init(*refs, n)
    _softmax_update(_gqa_queries(piece(Z_QA, 384), lo), piece(Z_KA), piece(Z_VA), *refs, n)
    for s, o in enumerate(_gqa_outputs(_softmax_result(l_ref, acc_ref, n), rows, lo)):
        o_ref[:, LANES * s:LANES * (s + 1)] = o.astype(BF16)

    _sink_init(sink_ref, *refs, rows)
    _softmax_update(_gqa_queries(piece(Z_QB, 384), lo), piece(Z_KB), piece(Z_VB), *refs, n)
    for s, o in enumerate(_gqa_outputs(_softmax_result(l_ref, acc_ref, n), rows, lo)):
        o_ref[:, 384 + LANES * s:384 + LANES * (s + 1)] = o.astype(BF16)

    lam = _diff_lambda(lam_ref, lam_init)
    n = 4 * rows
    for s in range(2):
        _softmax_init(*refs, n)
        _softmax_update(_diff_queries(piece(Z_QC + LANES * s), lane), piece(Z_KC + LANES * s),
                        piece(Z_VC + LANES * s), *refs, n)
        oc = _diff_output(_softmax_result(l_ref, acc_ref, n), rows, lam, gsub_ref[...], lam_init, lo)
        o_ref[:, 768 + LANES * s:768 + LANES * (s + 1)] = oc.astype(BF16)


def _attn_ctx(z, sink, lam_par, gsub, *, seq, lam_init):
    t = z.shape[0]
    return pl.pallas_call(
        functools.partial(_attn_ctx_kernel, lam_init=lam_init),
        grid=(t // seq,),
        in_specs=[
            pl.BlockSpec(memory_space=pltpu.SMEM),
            _const_spec((8, LANES)),
            _const_spec((1, LANES)),
            pl.BlockSpec((seq, D_IN), lambda b: (b, 0)),
        ],
        out_specs=pl.BlockSpec((seq, D_MODEL), lambda b: (b, 0)),
        out_shape=jax.ShapeDtypeStruct((t, D_MODEL), BF16),
        scratch_shapes=[pltpu.VMEM((A_HEADS * seq, LANES), F32)] * 3,
        compiler_params=_cparams(("arbitrary",)),
        name="attn_ctx",
    )(sink, lam_par, gsub, z)


def _gqa_queries_t(qt_ref, lo):
    slabs = [qt_ref[LANES * s:LANES * (s + 1), :] for s in range(3)]
    zero = jnp.zeros_like(slabs[0])
    return jnp.concatenate([jnp.where(lo, s, zero) for s in slabs] + [jnp.where(lo, zero, s) for s in slabs], axis=1)


def _gqa_store_t(ot, o_ref, tq, lo):
    for s in range(3):
        slab_t = jnp.where(lo, ot[:, s * tq:(s + 1) * tq], ot[:, (3 + s) * tq:(4 + s) * tq])
        o_ref[:, LANES * s:LANES * (s + 1)] = slab_t.T.astype(BF16)


def _attn_a_kernel(qt_ref, kc_ref, vct_ref, k_ref, vt_ref, o_ref, q_scr, m_ref, l_ref, acc_ref,
                   s0, s1, x0, x1, p0, p1, a0, a1, *, tk):
    tq = qt_ref.shape[1]
    lo = lax.broadcasted_iota(jnp.int32, (LANES, tq), 0) < HEAD_DIM
    refs = (m_ref, l_ref, acc_ref)
    q_scr[...] = _gqa_queries_t(qt_ref, lo)
    _softmax_init_t(*refs)
    _attend_pipelined_t(q_scr, kc_ref, vct_ref, k_ref, vt_ref, tk, (s0, s1), (x0, x1), (p0, p1), (a0, a1), *refs)
    _gqa_store_t(acc_ref[...] * (1.0 / l_ref[...]), o_ref, tq, lo)


def _keys_major_scratch(n, tk):
    return [pltpu.VMEM((LANES, n), BF16), pltpu.VMEM((1, n), F32), pltpu.VMEM((1, n), F32),
            pltpu.VMEM((LANES, n), F32),
            pltpu.VMEM((tk, n), F32), pltpu.VMEM((tk, n), F32),
            pltpu.VMEM((1, n), F32), pltpu.VMEM((1, n), F32),
            pltpu.VMEM((tk, n), BF16), pltpu.VMEM((tk, n), BF16),
            pltpu.VMEM((1, n), F32), pltpu.VMEM((1, n), F32)]


def _attn_a(z, zt, k_ctx, vt_ctx, *, seq, tq, tk):
    t = z.shape[0]
    nq = seq // tq
    n_ctx = k_ctx.shape[1]
    n = A_HEADS * tq
    return pl.pallas_call(
        functools.partial(_attn_a_kernel, tk=tk),
        grid=(t // seq, nq),
        in_specs=[
            pl.BlockSpec((384, tq), lambda b, i: (T_QA // 384, b * nq + i)),
            pl.BlockSpec((1, n_ctx, LANES), lambda b, i: (b, 0, 0)),
            pl.BlockSpec((1, LANES, n_ctx), lambda b, i: (b, 0, 0)),
            pl.BlockSpec((seq, LANES), lambda b, i: (b, Z_KA // LANES)),
            pl.BlockSpec((LANES, seq), lambda b, i: (T_VA // LANES, b)),
        ],
        out_specs=pl.BlockSpec((tq, 384), lambda b, i: (b * nq + i, 0)),
        out_shape=jax.ShapeDtypeStruct((t, 384), BF16),
        scratch_shapes=_keys_major_scratch(n, tk),
        compiler_params=_cparams(("arbitrary", "arbitrary")),
        name="attn_a",
    )(zt, k_ctx, vt_ctx, z, zt)


def _attn_b_kernel(sink_ref, qt_ref, kc_ref, vct_ref, k_ref, vt_ref, o_ref):
    tq = qt_ref.shape[1]
    seq = k_ref.shape[0]
    n = B_HEADS * tq
    band = tq + 2 * WINDOW
    i = pl.program_id(1)
    lo = lax.broadcasted_iota(jnp.int32, (LANES, tq), 0) < HEAD_DIM
    q = _gqa_queries_t(qt_ref, lo)
    start = pl.multiple_of(jnp.clip(i * tq - WINDOW, 0, seq - band), LANES)
    s_c = jnp.dot(kc_ref[0], q, preferred_element_type=F32)
    s_b = jnp.dot(k_ref[pl.ds(start, band), :], q, preferred_element_type=F32)
    qpos = i * tq + (lax.broadcasted_iota(jnp.int32, (band, n), 1) & (tq - 1))
    kpos = start + lax.broadcasted_iota(jnp.int32, (band, n), 0)
    s_b = jnp.where(jnp.abs(kpos - qpos) <= WINDOW, s_b, NEG)
    sink = jnp.concatenate([jnp.full((1, tq), sink_ref[h] * LOG2E, F32) for h in range(B_HEADS)], axis=1)
    m = jnp.maximum(sink, jnp.maximum(jnp.max(s_c, axis=0, keepdims=True), jnp.max(s_b, axis=0, keepdims=True)))
    p_c = jnp.exp2(s_c - m)
    p_b = jnp.exp2(s_b - m)
    l = jnp.exp2(sink - m) + jnp.sum(p_c, axis=0, keepdims=True) + jnp.sum(p_b, axis=0, keepdims=True)
    acc = (jnp.dot(vct_ref[0], p_c.astype(BF16), preferred_element_type=F32)
           + jnp.dot(vt_ref[:, pl.ds(start, band)], p_b.astype(BF16), preferred_element_type=F32))
    _gqa_store_t(acc * (1.0 / l), o_ref, tq, lo)


def _attn_b(z, zt, k_ctx, vt_ctx, sink, *, seq, tq):
    t = z.shape[0]
    nq = seq // tq
    n_ctx = k_ctx.shape[1]
    return pl.pallas_call(
        _attn_b_kernel,
        grid=(t // seq, nq),
        in_specs=[
            pl.BlockSpec(memory_space=pltpu.SMEM),
            pl.BlockSpec((384, tq), lambda b, i: (T_QB // 384, b * nq + i)),
            pl.BlockSpec((1, n_ctx, LANES), lambda b, i: (b, 0, 0)),
            pl.BlockSpec((1, LANES, n_ctx), lambda b, i: (b, 0, 0)),
            pl.BlockSpec((seq, LANES), lambda b, i: (b, Z_KB // LANES)),
            pl.BlockSpec((LANES, seq), lambda b, i: (T_VB // LANES, b)),
        ],
        out_specs=pl.BlockSpec((tq, 384), lambda b, i: (b * nq + i, 0)),
        out_shape=jax.ShapeDtypeStruct((t, 384), BF16),
        compiler_params=_cparams(("arbitrary", "arbitrary")),
        name="attn_b",
    )(sink, zt, k_ctx, vt_ctx, z, zt)


def _attn_c_kernel(lam_ref, gsub_ref, qt_ref, kc_ref, vct_ref, k_ref, vt_ref, o_ref,
                   q_scr, m_ref, l_ref, acc_ref, s0, s1, x0, x1, p0, p1, a0, a1, *, tk, lam_init):
    tq = qt_ref.shape[1]
    row = lax.broadcasted_iota(jnp.int32, (LANES, tq), 0)
    refs = (m_ref, l_ref, acc_ref)
    qt = qt_ref[...]
    zero = jnp.zeros_like(qt)
    parts = [jnp.where((row >= C_QK_DIM * j) & (row < C_QK_DIM * (j + 1)), qt, zero) for j in range(4)]
    q_scr[...] = jnp.concatenate(parts, axis=1)
    _softmax_init_t(*refs)
    _attend_pipelined_t(q_scr, kc_ref, vct_ref, k_ref, vt_ref, tk, (s0, s1), (x0, x1), (p0, p1), (a0, a1), *refs)
    lam = _diff_lambda(lam_ref, lam_init)
    ot = acc_ref[...] * (1.0 / l_ref[...])
    o_even = ot[:, 0:tq] - lam * ot[:, tq:2 * tq]
    o_odd = ot[:, 2 * tq:3 * tq] - lam * ot[:, 3 * tq:4 * tq]
    oc = jnp.where(row < C_V_DIM, o_even, o_odd).T
    lo = lax.broadcasted_iota(jnp.int32, (tq, LANES), 1) < C_V_DIM
    o_ref[...] = (_head_rms(oc, gsub_ref[...], lo) * (1.0 - lam_init)).astype(BF16)


def _attn_c(z, zt, k_ctx, vt_ctx, lam_par, gsub, *, seq, tq, tk, lam_init):
    t = z.shape[0]
    nq = seq // tq
    n_ctx = k_ctx.shape[1]
    n = 4 * tq
    return pl.pallas_call(
        functools.partial(_attn_c_kernel, tk=tk, lam_init=lam_init),
        grid=(t // seq, 2, nq),
        in_specs=[
            _const_spec((8, LANES)),
            _const_spec((1, LANES)),
            pl.BlockSpec((LANES, tq), lambda b, s, i: (T_QC // LANES + s, b * nq + i)),
            pl.BlockSpec((1, n_ctx, LANES), lambda b, s, i: (b, 0, s)),
            pl.BlockSpec((1, LANES, n_ctx), lambda b, s, i: (b, s, 0)),
            pl.BlockSpec((seq, LANES), lambda b, s, i: (b, Z_KC // LANES + s)),
            pl.BlockSpec((LANES, seq), lambda b, s, i: (T_VC // LANES + s, b)),
        ],
        out_specs=pl.BlockSpec((tq, LANES), lambda b, s, i: (b * nq + i, s)),
        out_shape=jax.ShapeDtypeStruct((t, C_HEADS * C_V_DIM), BF16),
        scratch_shapes=_keys_major_scratch(n, tk),
        compiler_params=_cparams(("arbitrary", "arbitrary", "arbitrary")),
        name="attn_c",
    )(lam_par, gsub, zt, k_ctx, vt_ctx, z, zt)


def _post_kernel(*refs, n_parts, halo, tiles_per_seq, final):
    it = iter(refs)
    tiles = [next(it) for _ in range(1 + n_parts)]
    halos = [(next(it), next(it)) for _ in range(1 + n_parts)] if halo else None
    (mod_ref, gn_ref, wo_ref, wu_ref, cw_ref, cb_ref, wd_ref, gf_ref, out_ref) = (next(it) for _ in range(9))

    def rows_of(j):
        if halo:
            return jnp.concatenate([halos[j][0][...], tiles[j][...], halos[j][1][...]], axis=0)
        return tiles[j][...]

    tm = tiles[0].shape[0]
    x = rows_of(0)
    o = jnp.concatenate([rows_of(j) for j in range(1, 1 + n_parts)], axis=1)
    ext = x.shape[0]
    mod = mod_ref[0]
    x1 = x + mod[2:3] * jnp.dot(o, wo_ref[...], preferred_element_type=F32)
    h = _rms(x1, gn_ref[...]) * (1.0 + mod[4:5]) + mod[3:4]
    row = lax.broadcasted_iota(jnp.int32, (ext, 1), 0)
    if halo:
        t_in_seq = pl.program_id(0) % tiles_per_seq
        keep = ((row >= halo) | (t_in_seq > 0)) & ((row < halo + tm) | (t_in_seq < tiles_per_seq - 1))
        h = jnp.where(keep, h, 0.0)
    h = h.astype(BF16)

    def conv(u, c0):
        cw = cw_ref[:, c0:c0 + FFN_CHUNK]
        up = pltpu.roll(u, 1, 0)
        dn = pltpu.roll(u, ext - 1, 0)
        if not halo:
            up = jnp.where(row == 0, 0.0, up)
            dn = jnp.where(row == ext - 1, 0.0, dn)
        v = cw[0:1] * up + cw[1:2] * u + cw[2:3] * dn + cb_ref[:, c0:c0 + FFN_CHUNK]
        return v[halo:halo + tm]

    def up_proj(c):
        ca, cg = c * FFN_CHUNK, D_FF + c * FFN_CHUNK
        return (jnp.dot(h, wu_ref[:, ca:ca + FFN_CHUNK], preferred_element_type=F32),
                jnp.dot(h, wu_ref[:, cg:cg + FFN_CHUNK], preferred_element_type=F32))

    n_chunks = D_FF // FFN_CHUNK
    acc = jnp.zeros((tm, D_MODEL), F32)
    nxt = up_proj(0)
    for c in range(n_chunks):
        ua, ug = nxt
        if c + 1 < n_chunks:
            nxt = up_proj(c + 1)
        ca = c * FFN_CHUNK
        a = conv(ua, ca)
        g = conv(ug, D_FF + ca)
        act = a / (1.0 + jnp.exp(-a)) * g
        acc = acc + jnp.dot(act.astype(BF16), wd_ref[ca:ca + FFN_CHUNK, :], preferred_element_type=F32)
    x2 = x1[halo:halo + tm] + mod[5:6] * acc
    if final:
        x2 = _rms(x2, gf_ref[...])
    out_ref[...] = x2


def _post(x, o_parts, mod_l, gn2, wo, wu, cw, cb, wd, gf, *, tm, tiles_per_cond, tiles_per_seq, final):
    t = x.shape[0]
    assert sum(o.shape[1] for o in o_parts) == D_MODEL
    halo = HALO if tiles_per_seq > 1 else 0
    tile = lambda i: (i, 0)
    rows = [x] + list(o_parts)
    in_specs = [pl.BlockSpec((tm, a.shape[1]), tile) for a in rows]
    args = list(rows)
    if halo:
        per = tm // halo
        prev = lambda i: (jnp.maximum(i * per - 1, 0), 0)
        nxt = lambda i: (jnp.minimum((i + 1) * per, t // halo - 1), 0)
        for a in rows:
            in_specs += [pl.BlockSpec((halo, a.shape[1]), prev), pl.BlockSpec((halo, a.shape[1]), nxt)]
            args += [a, a]
    in_specs += [
        pl.BlockSpec((1, 6, D_MODEL), lambda i: (i // tiles_per_cond, 0, 0)),
        _const_spec((1, D_MODEL)),
        _const_spec((D_MODEL, D_MODEL)),
        _const_spec((D_MODEL, 2 * D_FF)),
        _const_spec((3, 2 * D_FF)),
        _const_spec((1, 2 * D_FF)),
        _const_spec((D_FF, D_MODEL)),
        _const_spec((1, D_MODEL)),
    ]
    args += [mod_l, gn2, wo, wu, cw, cb, wd, gf]
    return pl.pallas_call(
        functools.partial(_post_kernel, n_parts=len(o_parts), halo=halo, tiles_per_seq=tiles_per_seq, final=final),
        grid=(t // tm,),
        in_specs=in_specs,
        out_specs=pl.BlockSpec((tm, D_MODEL), tile),
        out_shape=jax.ShapeDtypeStruct((t, D_MODEL), F32),
        compiler_params=_cparams(("arbitrary",)),
        name="post",
    )(*args)


def _rope_tables(seq):
    t = jnp.arange(seq)
    rows = (t // GRID_W).astype(F32)[:, None]
    cols = (t % GRID_W).astype(F32)[:, None]

    def tab(half, reps):
        inv = ROPE_THETA ** (-jnp.arange(half, dtype=F32) / half)
        ar, ac = rows * inv[None, :], cols * inv[None, :]
        cos = jnp.concatenate([jnp.cos(ar), jnp.cos(ar), jnp.cos(ac), jnp.cos(ac)], axis=-1)
        sin = jnp.concatenate([-jnp.sin(ar), jnp.sin(ar), -jnp.sin(ac), jnp.sin(ac)], axis=-1)
        return jnp.tile(cos, (1, reps)), jnp.tile(sin, (1, reps))

    cos64, sin64 = tab(HEAD_DIM // 4, LANES // HEAD_DIM)
    cos32, sin32 = tab(C_QK_DIM // 4, LANES // C_QK_DIM)
    return cos64, sin64, cos32, sin32


def _w_in_perm():
    sizes = (A_HEADS * HEAD_DIM, A_KV * HEAD_DIM, A_KV * HEAD_DIM, B_HEADS * HEAD_DIM, B_KV * HEAD_DIM,
             B_KV * HEAD_DIM, C_HEADS * 2 * C_QK_DIM, C_HEADS * 2 * C_QK_DIM, C_HEADS * C_V_DIM)
    offs = np.concatenate([[0], np.cumsum(sizes)])
    qa, ka, va, qb, kb, vb, qc, kc, vc = (np.arange(offs[j], offs[j + 1]) for j in range(9))
    pair = np.concatenate([np.arange(h * HEAD_DIM, (h + 1) * HEAD_DIM) for h in PAIRED_HEADS])
    return np.concatenate([qa[pair], qb[pair], qc, ka, va, kb, vb, kc, vc])


def _w_out_perm():
    pair = np.concatenate([np.arange(h * HEAD_DIM, (h + 1) * HEAD_DIM) for h in PAIRED_HEADS])
    return np.concatenate([pair, A_HEADS * HEAD_DIM + pair, np.arange(768, D_MODEL)])


def _take_runs(w, perm, axis):
    cuts = [0] + [j for j in range(1, len(perm)) if perm[j] != perm[j - 1] + 1] + [len(perm)]
    parts = [lax.slice_in_dim(w, int(perm[a]), int(perm[b - 1]) + 1, axis=axis) for a, b in zip(cuts[:-1], cuts[1:])]
    return jnp.concatenate(parts, axis=axis)


def kernel(x_prompt, x_sample, cache_a_k, cache_a_v, cache_b_k, cache_b_v, cache_c_k, cache_c_v, c, c_ctx, w_ada, b_ada, g_norm1, g_norm2, w_in, g_qa, g_ka, sink_b, lam_q1, lam_k1, lam_q2, lam_k2, g_subln, w_out, w_up, conv_w, conv_b, w_down, g_final):
    n_ctx_req, ctx_len, _ = x_prompt.shape
    n_lat_req, lat_len, _ = x_sample.shape
    past = cache_a_k.shape[2]

    conds = jnp.zeros((8, D_MODEL), F32).at[0].set(c_ctx).at[1:1 + n_lat_req].set(c)
    mod = _modulation(conds, w_ada, b_ada).reshape(DEPTH, 8, 6, D_MODEL)

    w_in_p = _take_runs(w_in, _w_in_perm(), 2).astype(BF16)
    w_out_p = _take_runs(w_out, _w_out_perm(), 1).astype(BF16)
    w_up_b = w_up.astype(BF16)
    w_down_b = w_down.astype(BF16)
    rope_tabs = _rope_tables(lat_len)
    zpad = jnp.zeros((DEPTH, LANES - C_QK_DIM), F32)
    lam_rows = [jnp.concatenate([v, zpad], axis=-1) for v in (lam_q1, lam_k1, lam_q2, lam_k2)]
    lam_par = jnp.concatenate([jnp.stack(lam_rows, axis=1), jnp.zeros((DEPTH, 4, LANES), F32)], axis=1)
    gf = g_final.reshape(1, D_MODEL)

    xc = x_prompt.reshape(n_ctx_req * ctx_len, D_MODEL)
    xs = x_sample.reshape(n_lat_req * lat_len, D_MODEL)
    lat_tm = 512
    caches = []
    for l in range(DEPTH):
        lam_init = 0.8 - 0.6 * math.exp(-0.3 * l)
        gn1 = g_norm1[l].reshape(1, D_MODEL)
        gn2 = g_norm2[l].reshape(1, D_MODEL)
        gq = jnp.tile(g_qa[l], LANES // HEAD_DIM).reshape(1, LANES)
        gk = jnp.tile(g_ka[l], LANES // HEAD_DIM).reshape(1, LANES)
        gsub = jnp.tile(g_subln[l], LANES // C_V_DIM).reshape(1, LANES)
        post_w = (gn2, w_out_p[l], w_up_b[l], conv_w[l], conv_b[l].reshape(1, 2 * D_FF), w_down_b[l], gf)
        final = l == DEPTH - 1

        n_tiles = n_ctx_req
        zc, cache = _in_proj(xc, mod[l, 0:1], gn1, w_in_p[l], gq, gk, None, tm=ctx_len, tiles_per_cond=n_tiles,
                             tiles_per_seq=1, emit_cache=True, emit_t=False)
        caches.append(cache)
        oc = _attn_ctx(zc, sink_b[l], lam_par[l], gsub, seq=ctx_len, lam_init=lam_init)
        xc = _post(xc, [oc], mod[l, 0:1], *post_w, tm=ctx_len, tiles_per_cond=n_tiles, tiles_per_seq=1, final=final)

        per_seq = lat_len // lat_tm
        zs, zts = _in_proj(xs, mod[l, 1:1 + n_lat_req], gn1, w_in_p[l], gq, gk, rope_tabs, tm=lat_tm,
                           tiles_per_cond=per_seq, tiles_per_seq=per_seq, emit_cache=False, emit_t=True)
        flat = lambda a: a[:, l].reshape(n_lat_req, past, -1).astype(BF16)
        flat_t = lambda a: jnp.swapaxes(flat(a), 1, 2)
        oa = _attn_a(zs, zts, flat(cache_a_k), flat_t(cache_a_v), seq=lat_len, tq=128, tk=512)
        ob = _attn_b(zs, zts, flat(cache_b_k), flat_t(cache_b_v), sink_b[l], seq=lat_len, tq=128)
        oc = _attn_c(zs, zts, flat(cache_c_k), flat_t(cache_c_v), lam_par[l], gsub, seq=lat_len, tq=256, tk=512,
                     lam_init=lam_init)
        xs = _post(xs, [oa, ob, oc], mod[l, 1:1 + n_lat_req], *post_w, tm=lat_tm, tiles_per_cond=per_seq,
                   tiles_per_seq=per_seq, final=final)

    def stack(lo, hi, heads):
        parts = [cch[:, lo - Z_KA:hi - Z_KA].reshape(n_ctx_req, ctx_len, heads, -1) for cch in caches]
        return jnp.stack(parts, axis=1)

    y_prompt = xc.reshape(x_prompt.shape)
    y_sample = xs.reshape(x_sample.shape)
    return (y_prompt, y_sample,
            stack(Z_KA, Z_VA, A_KV), stack(Z_VA, Z_KB, A_KV), stack(Z_KB, Z_VB, B_KV), stack(Z_VB, Z_KC, B_KV),
            stack(Z_KC, Z_VC, C_HEADS), stack(Z_VC, D_IN, C_HEADS))
```

```python
import functools
import math

import numpy as np
import jax
import jax.numpy as jnp
from jax import lax
from jax.experimental import pallas as pl
from jax.experimental.pallas import tpu as pltpu

D_MODEL = 1024
DEPTH = 2
GRID_W = 64
HEAD_DIM = 64
A_HEADS = 6
A_KV = 2
B_HEADS = 6
B_KV = 2
C_HEADS = 4
C_QK_DIM = 32
C_V_DIM = 2 * C_QK_DIM
WINDOW = 128
ROPE_THETA = 10000.0
D_FF = 2816
EPS = 1e-6
NEG = -1e30
LOG2E = math.log2(math.e)

LANES = 128
BF16_ROWS = 16
MXU_COLS = 256
VMEM_LIMIT = 56 * 1024 * 1024

Z_QA, Z_QB, Z_QC = 0, 384, 768
Z_KA, Z_VA, Z_KB, Z_VB, Z_KC, Z_VC = 1024, 1152, 1280, 1408, 1536, 1792
D_IN = 2048
CACHE_PIECES = ((Z_KA, 128), (Z_VA, 128), (Z_KB, 128), (Z_VB, 128), (Z_KC, 256), (Z_VC, 256))
T_QA, T_VA, T_QC, T_VC, T_VB, T_QB, T_ROWS = 0, 384, 512, 768, 1024, 1152, 1536
PAIRED_HEADS = (0, 3, 1, 4, 2, 5)
FFN_CHUNK = 256
HALO = BF16_ROWS

F32 = jnp.float32
BF16 = jnp.bfloat16


def _cparams(sem):
    return pltpu.CompilerParams(dimension_semantics=sem, vmem_limit_bytes=VMEM_LIMIT)


def _const_spec(shape):
    nd = len(shape)
    return pl.BlockSpec(shape, lambda *_: (0,) * nd)


def _mod_kernel(c_ref, w_ref, b_ref, o_ref):
    cond = c_ref[...]
    a = cond / (1.0 + jnp.exp(-cond))
    o_ref[0] = jnp.dot(a.astype(BF16), w_ref[0].astype(BF16), preferred_element_type=F32) + b_ref[0]


def _modulation(conds, w_ada, b_ada):
    nb = 1536
    n_out = w_ada.shape[-1]
    return pl.pallas_call(
        _mod_kernel,
        grid=(DEPTH, n_out // nb),
        in_specs=[
            pl.BlockSpec((8, D_MODEL), lambda l, j: (0, 0)),
            pl.BlockSpec((1, D_MODEL, nb), lambda l, j: (l, 0, j)),
            pl.BlockSpec((1, 1, nb), lambda l, j: (l, 0, j)),
        ],
        out_specs=pl.BlockSpec((1, 8, nb), lambda l, j: (l, 0, j)),
        out_shape=jax.ShapeDtypeStruct((DEPTH, 8, n_out), F32),
        compiler_params=_cparams(("arbitrary", "arbitrary")),
        name="modulation",
    )(conds, w_ada, b_ada.reshape(DEPTH, 1, n_out))


def _rms(x, g):
    ms = jnp.mean(x * x, axis=-1, keepdims=True)
    return x * lax.rsqrt(ms + EPS) * g


def _head_rms(x, g, lo):
    ss = x * x
    s_lo = jnp.sum(jnp.where(lo, ss, 0.0), axis=-1, keepdims=True)
    s_hi = jnp.sum(jnp.where(lo, 0.0, ss), axis=-1, keepdims=True)
    inv = jnp.where(lo, lax.rsqrt(s_lo * (1.0 / HEAD_DIM) + EPS), lax.rsqrt(s_hi * (1.0 / HEAD_DIM) + EPS))
    return x * inv * g


def _rope(x, cos, sin, chunk, first):
    sw = jnp.where(first, pltpu.roll(x, LANES - chunk, 1), pltpu.roll(x, chunk, 1))
    return x * cos + sw * sin


def _softmax_init_t(m_ref, l_ref, acc_ref):
    m_ref[...] = jnp.full(m_ref.shape, NEG, F32)
    l_ref[...] = jnp.zeros(l_ref.shape, F32)
    acc_ref[...] = jnp.zeros(acc_ref.shape, F32)


def _scores_t(k, q_scr, s_ref, mx_ref, cs):
    s = jnp.dot(k, q_scr[:, cs], preferred_element_type=F32)
    s_ref[0:k.shape[0], cs] = s
    mx_ref[:, cs] = jnp.max(s, axis=0, keepdims=True)


def _probs_t(rows, s_ref, mx_ref, p_ref, a_ref, m_ref, l_ref, cs):
    s = s_ref[0:rows, cs]
    m_prev = m_ref[:, cs]
    m_new = jnp.maximum(m_prev, mx_ref[:, cs])
    alpha = jnp.exp2(m_prev - m_new)
    p = jnp.exp2(s - m_new)
    l_ref[:, cs] = alpha * l_ref[:, cs] + jnp.sum(p, axis=0, keepdims=True)
    m_ref[:, cs] = m_new
    a_ref[:, cs] = alpha
    p_ref[0:rows, cs] = p.astype(BF16)


def _values_t(vt, p_ref, a_ref, acc_ref, cs):
    pv = jnp.dot(vt, p_ref[0:vt.shape[1], cs], preferred_element_type=F32)
    acc_ref[:, cs] = a_ref[:, cs] * acc_ref[:, cs] + pv


def _attend_pipelined_t(q_scr, kc_ref, vct_ref, k_ref, vt_ref, tk, s_bufs, x_bufs, p_bufs, a_bufs,
                        m_ref, l_ref, acc_ref):
    n_lat = k_ref.shape[0] // tk
    n_ctx = kc_ref.shape[1]
    assert n_ctx <= tk and n_lat % 2 == 0 and n_lat >= 4

    def k_lat(j):
        return k_ref[pl.ds(pl.multiple_of(j * tk, tk), tk), :]

    def v_lat(j):
        return vt_ref[:, pl.ds(pl.multiple_of(j * tk, tk), tk)]

    n = q_scr.shape[1]
    groups = [slice(c, c + MXU_COLS) for c in range(0, n, MXU_COLS)]

    def step(par, k=None, vt=None, probs=tk):
        for cs in groups:
            if probs:
                _probs_t(probs, s_bufs[par], x_bufs[par], p_bufs[par], a_bufs[par], m_ref, l_ref, cs)
            if vt is not None:
                _values_t(vt, p_bufs[1 - par], a_bufs[1 - par], acc_ref, cs)
            if k is not None:
                _scores_t(k, q_scr, s_bufs[1 - par], x_bufs[1 - par], cs)

    step(1, k=kc_ref[0], probs=0)
    step(0, k=k_lat(0), probs=n_ctx)
    step(1, k=k_lat(1), vt=vct_ref[0])

    def pair(jj, carry):
        j = 2 * jj
        step(0, k=k_lat(j + 2), vt=v_lat(j))
        step(1, k=k_lat(j + 3), vt=v_lat(j + 1))
        return carry

    for jj in range(n_lat // 2 - 1):
        pair(jj, 0)
    step(0, vt=v_lat(n_lat - 2))
    step(1, vt=v_lat(n_lat - 1), probs=0)


def _diff_lambda(lam_ref, lam_init):
    f = lambda a, b: jnp.exp(jnp.sum(a * b, axis=-1, keepdims=True))
    return f(lam_ref[0:1], lam_ref[1:2]) - f(lam_ref[2:3], lam_ref[3:4]) + lam_init


def _in_proj_kernel(*refs, use_rope, emit_cache, emit_t):
    it = iter(refs)
    x_ref, mod_ref, gn_ref, w_ref, gq_ref, gk_ref = (next(it) for _ in range(6))
    if use_rope:
        cos64, sin64, cos32, sin32 = (next(it)[...] for _ in range(4))
    prev_ref = next(it) if emit_cache == "final" else None
    z_ref = next(it)
    cache_ref = next(it) if emit_cache == "rows" else None
    final_refs = [next(it) for _ in CACHE_PIECES] if emit_cache == "final" else None
    zt_ref = next(it) if emit_t else None
    if final_refs:
        for ref, (start, width) in zip(final_refs, CACHE_PIECES):
            ref[0, 0] = prev_ref[:, start - Z_KA:start - Z_KA + width]

    x = x_ref[...]
    mod = mod_ref[0]
    h = _rms(x, gn_ref[...]) * (1.0 + mod[1:2]) + mod[0:1]
    z = jnp.dot(h.astype(BF16), w_ref[...], preferred_element_type=F32)

    rows = x.shape[0]
    lane = lax.broadcasted_iota(jnp.int32, (rows, LANES), 1)
    lo = lane < HEAD_DIM
    first16 = (lane & 31) < 16
    first8 = (lane & 15) < 8
    gq = gq_ref[...]
    gk = gk_ref[...]
    q_scale = HEAD_DIM ** -0.5 * LOG2E
    qc_scale = C_QK_DIM ** -0.5 * LOG2E

    def rope64(v):
        return _rope(v, cos64, sin64, 16, first16) if use_rope else v

    def rope32(v):
        return _rope(v, cos32, sin32, 8, first8) if use_rope else v

    def slab(off):
        return z[:, off:off + LANES]

    def put(off, v):
        z_ref[:, off:off + LANES] = v.astype(BF16)

    def put_cache(off, v):
        if cache_ref is not None:
            cache_ref[:, off - Z_KA:off - Z_KA + LANES] = v
        if final_refs:
            for ref, (start, width) in zip(final_refs, CACHE_PIECES):
                if start <= off < start + width:
                    ref[0, DEPTH - 1, :, off - start:off - start + LANES] = v

    def put_t(off, v):
        if emit_t:
            zt_ref[off:off + LANES, :] = v.T.astype(BF16)

    for s in range(3):
        qa = rope64(_head_rms(slab(Z_QA + LANES * s), gq, lo)) * q_scale
        put(Z_QA + LANES * s, qa)
        put_t(T_QA + LANES * s, qa)
        qb = rope64(slab(Z_QB + LANES * s)) * q_scale
        put(Z_QB + LANES * s, qb)
        put_t(T_QB + LANES * s, qb)
    ka = _head_rms(slab(Z_KA), gk, lo)
    put_cache(Z_KA, ka)
    put(Z_KA, rope64(ka))
    kb = slab(Z_KB)
    put_cache(Z_KB, kb)
    put(Z_KB, rope64(kb))
    for off in (Z_VA, Z_VB, Z_VC, Z_VC + LANES):
        put_cache(off, slab(off))
        put(off, slab(off))
    put_t(T_VA, slab(Z_VA))
    put_t(T_VB, slab(Z_VB))
    for s in range(2):
        put_t(T_VC + LANES * s, slab(Z_VC + LANES * s))
        qc = rope32(slab(Z_QC + LANES * s)) * qc_scale
        put(Z_QC + LANES * s, qc)
        put_t(T_QC + LANES * s, qc)
        kc = slab(Z_KC + LANES * s)
        put_cache(Z_KC + LANES * s, kc)
        put(Z_KC + LANES * s, rope32(kc))


def _in_proj(x, mod_l, gn, w_in_p, gq, gk, rope_tabs, *, tm, tiles_per_cond, tiles_per_seq, emit_cache, emit_t,
             prev_cache=None):
    t = x.shape[0]
    use_rope = rope_tabs is not None
    assert (emit_cache == "final") == (prev_cache is not None) and DEPTH == 2
    in_specs = [
        pl.BlockSpec((tm, D_MODEL), lambda i: (i, 0)),
        pl.BlockSpec((1, 6, D_MODEL), lambda i: (i // tiles_per_cond, 0, 0)),
        _const_spec((1, D_MODEL)),
        _const_spec((D_MODEL, D_IN)),
        _const_spec((1, LANES)),
        _const_spec((1, LANES)),
    ]
    args = [x, mod_l, gn, w_in_p, gq, gk]
    if use_rope:
        in_specs += [pl.BlockSpec((tm, LANES), lambda i: (i % tiles_per_seq, 0))] * 4
        args += list(rope_tabs)
    if prev_cache is not None:
        in_specs.append(pl.BlockSpec((tm, D_IN - Z_KA), lambda i: (i, 0)))
        args.append(prev_cache)
    out_shape = [jax.ShapeDtypeStruct((t, D_IN), BF16)]
    out_specs = [pl.BlockSpec((tm, D_IN), lambda i: (i, 0))]
    if emit_cache == "rows":
        out_shape.append(jax.ShapeDtypeStruct((t, D_IN - Z_KA), F32))
        out_specs.append(pl.BlockSpec((tm, D_IN - Z_KA), lambda i: (i, 0)))
    if emit_cache == "final":
        for _, width in CACHE_PIECES:
            out_shape.append(jax.ShapeDtypeStruct((t // tm, DEPTH, tm, width), F32))
            out_specs.append(pl.BlockSpec((1, DEPTH, tm, width), lambda i: (i, 0, 0, 0)))
    if emit_t:
        out_shape.append(jax.ShapeDtypeStruct((T_ROWS, t), BF16))
        out_specs.append(pl.BlockSpec((T_ROWS, tm), lambda i: (0, i)))
    return pl.pallas_call(
        functools.partial(_in_proj_kernel, use_rope=use_rope, emit_cache=emit_cache, emit_t=emit_t),
        grid=(t // tm,),
        in_specs=in_specs,
        out_specs=out_specs,
        out_shape=out_shape,
        compiler_params=_cparams(("arbitrary",)),
        name="in_proj",
    )(*args)


def _gqa_queries_t(qt_ref, lo):
    slabs = [qt_ref[LANES * s:LANES * (s + 1), :] for s in range(3)]
    zero = jnp.zeros_like(slabs[0])
    return jnp.concatenate([jnp.where(lo, s, zero) for s in slabs] + [jnp.where(lo, zero, s) for s in slabs], axis=1)


def _gqa_store_t(ot, o_ref, tq, lo, col0=0):
    for s in range(3):
        slab_t = jnp.where(lo, ot[:, s * tq:(s + 1) * tq], ot[:, (3 + s) * tq:(4 + s) * tq])
        o_ref[:, col0 + LANES * s:col0 + LANES * (s + 1)] = slab_t.T.astype(BF16)


def _diff_queries_t(qt, row):
    zero = jnp.zeros_like(qt)
    return jnp.concatenate([jnp.where((row >= C_QK_DIM * j) & (row < C_QK_DIM * (j + 1)), qt, zero)
                            for j in range(4)], axis=1)


def _diff_output_t(ot, tq, lam, gsub, lam_init, row):
    o_even = ot[:, 0:tq] - lam * ot[:, tq:2 * tq]
    o_odd = ot[:, 2 * tq:3 * tq] - lam * ot[:, 3 * tq:4 * tq]
    oc = jnp.where(row < C_V_DIM, o_even, o_odd).T
    lo = lax.broadcasted_iota(jnp.int32, (tq, LANES), 1) < C_V_DIM
    return (_head_rms(oc, gsub, lo) * (1.0 - lam_init)).astype(BF16)


def _attn_ctx_kernel(sink_ref, lam_ref, gsub_ref, z_ref, zt_ref, o_ref, *, lam_init):
    tq = z_ref.shape[0]
    row = lax.broadcasted_iota(jnp.int32, (LANES, tq), 0)
    lo = row < HEAD_DIM

    def attend(q, k, vt, sink=None):
        s = jnp.dot(k, q, preferred_element_type=F32)
        m = jnp.max(s, axis=0, keepdims=True)
        if sink is not None:
            m = jnp.maximum(m, sink)
        p = jnp.exp2(s - m)
        l = jnp.sum(p, axis=0, keepdims=True)
        if sink is not None:
            l = l + jnp.exp2(sink - m)
        return jnp.dot(vt, p.astype(BF16), preferred_element_type=F32) * (1.0 / l)

    def keys(off):
        return z_ref[:, off:off + LANES]

    def feat(off):
        return zt_ref[off:off + LANES, :]

    _gqa_store_t(attend(_gqa_queries_t(zt_ref.at[T_QA:T_QA + 384], lo), keys(Z_KA), feat(T_VA)), o_ref, tq, lo)
    sink = jnp.concatenate([jnp.full((1, tq), sink_ref[h] * LOG2E, F32) for h in range(B_HEADS)], axis=1)
    _gqa_store_t(attend(_gqa_queries_t(zt_ref.at[T_QB:T_QB + 384], lo), keys(Z_KB), feat(T_VB), sink),
                 o_ref, tq, lo, col0=384)
    lam = _diff_lambda(lam_ref, lam_init)
    for s in range(2):
        ot = attend(_diff_queries_t(feat(T_QC + LANES * s), row), keys(Z_KC + LANES * s), feat(T_VC + LANES * s))
        o_ref[:, 768 + LANES * s:768 + LANES * (s + 1)] = _diff_output_t(ot, tq, lam, gsub_ref[...], lam_init, row)


def _attn_ctx(z, zt, sink, lam_par, gsub, *, seq, lam_init):
    t = z.shape[0]
    return pl.pallas_call(
        functools.partial(_attn_ctx_kernel, lam_init=lam_init),
        grid=(t // seq,),
        in_specs=[
            pl.BlockSpec(memory_space=pltpu.SMEM),
            _const_spec((8, LANES)),
            _const_spec((1, LANES)),
            pl.BlockSpec((seq, D_IN), lambda b: (b, 0)),
            pl.BlockSpec((T_ROWS, seq), lambda b: (0, b)),
        ],
        out_specs=pl.BlockSpec((seq, D_MODEL), lambda b: (b, 0)),
        out_shape=jax.ShapeDtypeStruct((t, D_MODEL), BF16),
        compiler_params=_cparams(("arbitrary",)),
        name="attn_ctx",
    )(sink, lam_par, gsub, z, zt)


def _attn_a_kernel(qt_ref, kc_ref, vct_ref, k_ref, vt_ref, o_ref, q_scr, m_ref, l_ref, acc_ref,
                   s0, s1, x0, x1, p0, p1, a0, a1, *, tk):
    tq = qt_ref.shape[1]
    lo = lax.broadcasted_iota(jnp.int32, (LANES, tq), 0) < HEAD_DIM
    refs = (m_ref, l_ref, acc_ref)
    q_scr[...] = _gqa_queries_t(qt_ref, lo)
    _softmax_init_t(*refs)
    _attend_pipelined_t(q_scr, kc_ref, vct_ref, k_ref, vt_ref, tk, (s0, s1), (x0, x1), (p0, p1), (a0, a1), *refs)
    _gqa_store_t(acc_ref[...] * (1.0 / l_ref[...]), o_ref, tq, lo)


def _keys_major_scratch(n, tk):
    return [pltpu.VMEM((LANES, n), BF16), pltpu.VMEM((1, n), F32), pltpu.VMEM((1, n), F32),
            pltpu.VMEM((LANES, n), F32),
            pltpu.VMEM((tk, n), F32), pltpu.VMEM((tk, n), F32),
            pltpu.VMEM((1, n), F32), pltpu.VMEM((1, n), F32),
            pltpu.VMEM((tk, n), BF16), pltpu.VMEM((tk, n), BF16),
            pltpu.VMEM((1, n), F32), pltpu.VMEM((1, n), F32)]


def _attn_a(z, zt, k_ctx, vt_ctx, *, seq, tq, tk):
    t = z.shape[0]
    nq = seq // tq
    n_ctx = k_ctx.shape[1]
    n = A_HEADS * tq
    return pl.pallas_call(
        functools.partial(_attn_a_kernel, tk=tk),
        grid=(t // seq, nq),
        in_specs=[
            pl.BlockSpec((384, tq), lambda b, i: (T_QA // 384, b * nq + i)),
            pl.BlockSpec((1, n_ctx, LANES), lambda b, i: (b, 0, 0)),
            pl.BlockSpec((1, LANES, n_ctx), lambda b, i: (b, 0, 0)),
            pl.BlockSpec((seq, LANES), lambda b, i: (b, Z_KA // LANES)),
            pl.BlockSpec((LANES, seq), lambda b, i: (T_VA // LANES, b)),
        ],
        out_specs=pl.BlockSpec((tq, 384), lambda b, i: (b * nq + i, 0)),
        out_shape=jax.ShapeDtypeStruct((t, 384), BF16),
        scratch_shapes=_keys_major_scratch(n, tk),
        compiler_params=_cparams(("arbitrary", "arbitrary")),
        name="attn_a",
    )(zt, k_ctx, vt_ctx, z, zt)


def _attn_b_kernel(sink_ref, qt_ref, kc_ref, vct_ref, k_ref, vt_ref, o_ref):
    tq = qt_ref.shape[1]
    seq = k_ref.shape[0]
    n = B_HEADS * tq
    band = tq + 2 * WINDOW
    i = pl.program_id(1)
    lo = lax.broadcasted_iota(jnp.int32, (LANES, tq), 0) < HEAD_DIM
    q = _gqa_queries_t(qt_ref, lo)
    start = pl.multiple_of(jnp.clip(i * tq - WINDOW, 0, seq - band), LANES)
    s_c = jnp.dot(kc_ref[0], q, preferred_element_type=F32)
    s_b = jnp.dot(k_ref[pl.ds(start, band), :], q, preferred_element_type=F32)
    qpos = i * tq + (lax.broadcasted_iota(jnp.int32, (band, n), 1) & (tq - 1))
    kpos = start + lax.broadcasted_iota(jnp.int32, (band, n), 0)
    s_b = jnp.where(jnp.abs(kpos - qpos) <= WINDOW, s_b, NEG)
    sink = jnp.concatenate([jnp.full((1, tq), sink_ref[h] * LOG2E, F32) for h in range(B_HEADS)], axis=1)
    m = jnp.maximum(sink, jnp.maximum(jnp.max(s_c, axis=0, keepdims=True), jnp.max(s_b, axis=0, keepdims=True)))
    p_c = jnp.exp2(s_c - m)
    p_b = jnp.exp2(s_b - m)
    l = jnp.exp2(sink - m) + jnp.sum(p_c, axis=0, keepdims=True) + jnp.sum(p_b, axis=0, keepdims=True)
    acc = (jnp.dot(vct_ref[0], p_c.astype(BF16), preferred_element_type=F32)
           + jnp.dot(vt_ref[:, pl.ds(start, band)], p_b.astype(BF16), preferred_element_type=F32))
    _gqa_store_t(acc * (1.0 / l), o_ref, tq, lo)


def _attn_b(z, zt, k_ctx, vt_ctx, sink, *, seq, tq):
    t = z.shape[0]
    nq = seq // tq
    n_ctx = k_ctx.shape[1]
    return pl.pallas_call(
        _attn_b_kernel,
        grid=(t // seq, nq),
        in_specs=[
            pl.BlockSpec(memory_space=pltpu.SMEM),
            pl.BlockSpec((384, tq), lambda b, i: (T_QB // 384, b * nq + i)),
            pl.BlockSpec((1, n_ctx, LANES), lambda b, i: (b, 0, 0)),
            pl.BlockSpec((1, LANES, n_ctx), lambda b, i: (b, 0, 0)),
            pl.BlockSpec((seq, LANES), lambda b, i: (b, Z_KB // LANES)),
            pl.BlockSpec((LANES, seq), lambda b, i: (T_VB // LANES, b)),
        ],
        out_specs=pl.BlockSpec((tq, 384), lambda b, i: (b * nq + i, 0)),
        out_shape=jax.ShapeDtypeStruct((t, 384), BF16),
        compiler_params=_cparams(("arbitrary", "arbitrary")),
        name="attn_b",
    )(sink, zt, k_ctx, vt_ctx, z, zt)


def _attn_c_kernel(lam_ref, gsub_ref, qt_ref, kc_ref, vct_ref, k_ref, vt_ref, o_ref,
                   q_scr, m_ref, l_ref, acc_ref, s0, s1, x0, x1, p0, p1, a0, a1, *, tk, lam_init):
    tq = qt_ref.shape[1]
    row = lax.broadcasted_iota(jnp.int32, (LANES, tq), 0)
    refs = (m_ref, l_ref, acc_ref)
    q_scr[...] = _diff_queries_t(qt_ref[...], row)
    _softmax_init_t(*refs)
    _attend_pipelined_t(q_scr, kc_ref, vct_ref, k_ref, vt_ref, tk, (s0, s1), (x0, x1), (p0, p1), (a0, a1), *refs)
    lam = _diff_lambda(lam_ref, lam_init)
    ot = acc_ref[...] * (1.0 / l_ref[...])
    o_ref[...] = _diff_output_t(ot, tq, lam, gsub_ref[...], lam_init, row)


def _attn_c(z, zt, k_ctx, vt_ctx, lam_par, gsub, *, seq, tq, tk, lam_init):
    t = z.shape[0]
    nq = seq // tq
    n_ctx = k_ctx.shape[1]
    n = 4 * tq
    return pl.pallas_call(
        functools.partial(_attn_c_kernel, tk=tk, lam_init=lam_init),
        grid=(t // seq, 2, nq),
        in_specs=[
            _const_spec((8, LANES)),
            _const_spec((1, LANES)),
            pl.BlockSpec((LANES, tq), lambda b, s, i: (T_QC // LANES + s, b * nq + i)),
            pl.BlockSpec((1, n_ctx, LANES), lambda b, s, i: (b, 0, s)),
            pl.BlockSpec((1, LANES, n_ctx), lambda b, s, i: (b, s, 0)),
            pl.BlockSpec((seq, LANES), lambda b, s, i: (b, Z_KC // LANES + s)),
            pl.BlockSpec((LANES, seq), lambda b, s, i: (T_VC // LANES + s, b)),
        ],
        out_specs=pl.BlockSpec((tq, LANES), lambda b, s, i: (b * nq + i, s)),
        out_shape=jax.ShapeDtypeStruct((t, C_HEADS * C_V_DIM), BF16),
        scratch_shapes=_keys_major_scratch(n, tk),
        compiler_params=_cparams(("arbitrary", "arbitrary", "arbitrary")),
        name="attn_c",
    )(lam_par, gsub, zt, k_ctx, vt_ctx, z, zt)


def _post_kernel(*refs, n_parts, halo, tiles_per_seq, final):
    it = iter(refs)
    tiles = [next(it) for _ in range(1 + n_parts)]
    halos = [(next(it), next(it)) for _ in range(1 + n_parts)] if halo else None
    (mod_ref, gn_ref, wo_ref, wu_ref, cw_ref, cb_ref, wd_ref, gf_ref, out_ref, act_scr) = (next(it) for _ in range(10))

    def rows_of(j):
        if halo:
            return jnp.concatenate([halos[j][0][...], tiles[j][...], halos[j][1][...]], axis=0)
        return tiles[j][...]

    tm = tiles[0].shape[0]
    x = rows_of(0)
    o = jnp.concatenate([rows_of(j) for j in range(1, 1 + n_parts)], axis=1)
    ext = x.shape[0]
    mod = mod_ref[0]
    x1 = x + mod[2:3] * jnp.dot(o, wo_ref[...], preferred_element_type=F32)
    h = _rms(x1, gn_ref[...]) * (1.0 + mod[4:5]) + mod[3:4]
    row = lax.broadcasted_iota(jnp.int32, (ext, 1), 0)
    if halo:
        t_in_seq = pl.program_id(0) % tiles_per_seq
        keep = ((row >= halo) | (t_in_seq > 0)) & ((row < halo + tm) | (t_in_seq < tiles_per_seq - 1))
        h = jnp.where(keep, h, 0.0)
    h = h.astype(BF16)

    def conv(u, c0):
        cw = cw_ref[:, c0:c0 + FFN_CHUNK]
        up = pltpu.roll(u, 1, 0)
        dn = pltpu.roll(u, ext - 1, 0)
        if not halo:
            up = jnp.where(row == 0, 0.0, up)
            dn = jnp.where(row == ext - 1, 0.0, dn)
        v = cw[0:1] * up + cw[1:2] * u + cw[2:3] * dn + cb_ref[:, c0:c0 + FFN_CHUNK]
        return v[halo:halo + tm]

    def up_proj(c):
        ca, cg = c * FFN_CHUNK, D_FF + c * FFN_CHUNK
        return (jnp.dot(h, wu_ref[:, ca:ca + FFN_CHUNK], preferred_element_type=F32),
                jnp.dot(h, wu_ref[:, cg:cg + FFN_CHUNK], preferred_element_type=F32))

    n_chunks = D_FF // FFN_CHUNK
    nxt = up_proj(0)
    for c in range(n_chunks):
        ua, ug = nxt
        if c + 1 < n_chunks:
            nxt = up_proj(c + 1)
        ca = c * FFN_CHUNK
        a = conv(ua, ca)
        g = conv(ug, D_FF + ca)
        act_scr[:, ca:ca + FFN_CHUNK] = (a / (1.0 + jnp.exp(-a)) * g).astype(BF16)
    x2 = x1[halo:halo + tm] + mod[5:6] * jnp.dot(act_scr[...], wd_ref[...], preferred_element_type=F32)
    if final:
        x2 = _rms(x2, gf_ref[...])
    out_ref[...] = x2


def _post(x, o_parts, mod_l, gn2, wo, wu, cw, cb, wd, gf, *, tm, tiles_per_cond, tiles_per_seq, final):
    t = x.shape[0]
    assert sum(o.shape[1] for o in o_parts) == D_MODEL
    halo = HALO if tiles_per_seq > 1 else 0
    tile = lambda i: (i, 0)
    rows = [x] + list(o_parts)
    in_specs = [pl.BlockSpec((tm, a.shape[1]), tile) for a in rows]
    args = list(rows)
    if halo:
        per = tm // halo
        prev = lambda i: (jnp.maximum(i * per - 1, 0), 0)
        nxt = lambda i: (jnp.minimum((i + 1) * per, t // halo - 1), 0)
        for a in rows:
            in_specs += [pl.BlockSpec((halo, a.shape[1]), prev), pl.BlockSpec((halo, a.shape[1]), nxt)]
            args += [a, a]
    in_specs += [
        pl.BlockSpec((1, 6, D_MODEL), lambda i: (i // tiles_per_cond, 0, 0)),
        _const_spec((1, D_MODEL)),
        _const_spec((D_MODEL, D_MODEL)),
        _const_spec((D_MODEL, 2 * D_FF)),
        _const_spec((3, 2 * D_FF)),
        _const_spec((1, 2 * D_FF)),
        _const_spec((D_FF, D_MODEL)),
        _const_spec((1, D_MODEL)),
    ]
    args += [mod_l, gn2, wo, wu, cw, cb, wd, gf]
    return pl.pallas_call(
        functools.partial(_post_kernel, n_parts=len(o_parts), halo=halo, tiles_per_seq=tiles_per_seq, final=final),
        grid=(t // tm,),
        in_specs=in_specs,
        out_specs=pl.BlockSpec((tm, D_MODEL), tile),
        out_shape=jax.ShapeDtypeStruct((t, D_MODEL), F32),
        scratch_shapes=[pltpu.VMEM((tm, D_FF), BF16)],
        compiler_params=_cparams(("arbitrary",)),
        name="post",
    )(*args)


def _rope_tables(seq):
    t = jnp.arange(seq)
    rows = (t // GRID_W).astype(F32)[:, None]
    cols = (t % GRID_W).astype(F32)[:, None]

    def tab(half, reps):
        inv = ROPE_THETA ** (-jnp.arange(half, dtype=F32) / half)
        ar, ac = rows * inv[None, :], cols * inv[None, :]
        cos = jnp.concatenate([jnp.cos(ar), jnp.cos(ar), jnp.cos(ac), jnp.cos(ac)], axis=-1)
        sin = jnp.concatenate([-jnp.sin(ar), jnp.sin(ar), -jnp.sin(ac), jnp.sin(ac)], axis=-1)
        return jnp.tile(cos, (1, reps)), jnp.tile(sin, (1, reps))

    cos64, sin64 = tab(HEAD_DIM // 4, LANES // HEAD_DIM)
    cos32, sin32 = tab(C_QK_DIM // 4, LANES // C_QK_DIM)
    return cos64, sin64, cos32, sin32


def _w_in_perm():
    sizes = (A_HEADS * HEAD_DIM, A_KV * HEAD_DIM, A_KV * HEAD_DIM, B_HEADS * HEAD_DIM, B_KV * HEAD_DIM,
             B_KV * HEAD_DIM, C_HEADS * 2 * C_QK_DIM, C_HEADS * 2 * C_QK_DIM, C_HEADS * C_V_DIM)
    offs = np.concatenate([[0], np.cumsum(sizes)])
    qa, ka, va, qb, kb, vb, qc, kc, vc = (np.arange(offs[j], offs[j + 1]) for j in range(9))
    pair = np.concatenate([np.arange(h * HEAD_DIM, (h + 1) * HEAD_DIM) for h in PAIRED_HEADS])
    return np.concatenate([qa[pair], qb[pair], qc, ka, va, kb, vb, kc, vc])


def _w_out_perm():
    pair = np.concatenate([np.arange(h * HEAD_DIM, (h + 1) * HEAD_DIM) for h in PAIRED_HEADS])
    return np.concatenate([pair, A_HEADS * HEAD_DIM + pair, np.arange(768, D_MODEL)])


def _take_runs(w, perm, axis):
    cuts = [0] + [j for j in range(1, len(perm)) if perm[j] != perm[j - 1] + 1] + [len(perm)]
    parts = [lax.slice_in_dim(w, int(perm[a]), int(perm[b - 1]) + 1, axis=axis) for a, b in zip(cuts[:-1], cuts[1:])]
    return jnp.concatenate(parts, axis=axis)


def kernel(x_prompt, x_sample, cache_a_k, cache_a_v, cache_b_k, cache_b_v, cache_c_k, cache_c_v, c, c_ctx, w_ada, b_ada, g_norm1, g_norm2, w_in, g_qa, g_ka, sink_b, lam_q1, lam_k1, lam_q2, lam_k2, g_subln, w_out, w_up, conv_w, conv_b, w_down, g_final):
    n_ctx_req, ctx_len, _ = x_prompt.shape
    n_lat_req, lat_len, _ = x_sample.shape
    past = cache_a_k.shape[2]

    conds = jnp.zeros((8, D_MODEL), F32).at[0].set(c_ctx).at[1:1 + n_lat_req].set(c)
    mod = _modulation(conds, w_ada, b_ada).reshape(DEPTH, 8, 6, D_MODEL)

    w_in_p = _take_runs(w_in, _w_in_perm(), 2).astype(BF16)
    w_out_p = _take_runs(w_out, _w_out_perm(), 1).astype(BF16)
    w_up_b = w_up.astype(BF16)
    w_down_b = w_down.astype(BF16)
    rope_tabs = _rope_tables(lat_len)
    zpad = jnp.zeros((DEPTH, LANES - C_QK_DIM), F32)
    lam_rows = [jnp.concatenate([v, zpad], axis=-1) for v in (lam_q1, lam_k1, lam_q2, lam_k2)]
    lam_par = jnp.concatenate([jnp.stack(lam_rows, axis=1), jnp.zeros((DEPTH, 4, LANES), F32)], axis=1)
    gf = g_final.reshape(1, D_MODEL)

    xc = x_prompt.reshape(n_ctx_req * ctx_len, D_MODEL)
    xs = x_sample.reshape(n_lat_req * lat_len, D_MODEL)
    lat_tm = 512
    caches = []
    for l in range(DEPTH):
        lam_init = 0.8 - 0.6 * math.exp(-0.3 * l)
        gn1 = g_norm1[l].reshape(1, D_MODEL)
        gn2 = g_norm2[l].reshape(1, D_MODEL)
        gq = jnp.tile(g_qa[l], LANES // HEAD_DIM).reshape(1, LANES)
        gk = jnp.tile(g_ka[l], LANES // HEAD_DIM).reshape(1, LANES)
        gsub = jnp.tile(g_subln[l], LANES // C_V_DIM).reshape(1, LANES)
        post_w = (gn2, w_out_p[l], w_up_b[l], conv_w[l], conv_b[l].reshape(1, 2 * D_FF), w_down_b[l], gf)
        final = l == DEPTH - 1

        n_tiles = n_ctx_req
        zc, *caches, ztc = _in_proj(xc, mod[l, 0:1], gn1, w_in_p[l], gq, gk, None, tm=ctx_len, tiles_per_cond=n_tiles,
                                    tiles_per_seq=1, emit_cache="final" if final else "rows", emit_t=True,
                                    prev_cache=caches[0] if final else None)
        oc = _attn_ctx(zc, ztc, sink_b[l], lam_par[l], gsub, seq=ctx_len, lam_init=lam_init)
        xc = _post(xc, [oc], mod[l, 0:1], *post_w, tm=ctx_len, tiles_per_cond=n_tiles, tiles_per_seq=1, final=final)

        per_seq = lat_len // lat_tm
        zs, zts = _in_proj(xs, mod[l, 1:1 + n_lat_req], gn1, w_in_p[l], gq, gk, rope_tabs, tm=lat_tm,
                           tiles_per_cond=per_seq, tiles_per_seq=per_seq, emit_cache=None, emit_t=True)
        flat = lambda a: a[:, l].reshape(n_lat_req, past, -1).astype(BF16)
        flat_t = lambda a: jnp.swapaxes(flat(a), 1, 2)
        oa = _attn_a(zs, zts, flat(cache_a_k), flat_t(cache_a_v), seq=lat_len, tq=128, tk=512)
        ob = _attn_b(zs, zts, flat(cache_b_k), flat_t(cache_b_v), sink_b[l], seq=lat_len, tq=128)
        oc = _attn_c(zs, zts, flat(cache_c_k), flat_t(cache_c_v), lam_par[l], gsub, seq=lat_len, tq=256, tk=512,
                     lam_init=lam_init)
        xs = _post(xs, [oa, ob, oc], mod[l, 1:1 + n_lat_req], *post_w, tm=lat_tm, tiles_per_cond=per_seq,
                   tiles_per_seq=per_seq, final=final)

    heads = (A_KV, A_KV, B_KV, B_KV, C_HEADS, C_HEADS)
    new_caches = tuple(cch.reshape(n_ctx_req, DEPTH, ctx_len, h, -1) for cch, h in zip(caches, heads))
    return (xc.reshape(x_prompt.shape), xs.reshape(x_sample.shape)) + new_caches
```

```python
import functools
import math

import numpy as np
import jax
import jax.numpy as jnp
from jax import lax
from jax.experimental import pallas as pl
from jax.experimental.pallas import tpu as pltpu

D_MODEL = 1024
DEPTH = 2
GRID_W = 64
HEAD_DIM = 64
A_HEADS = 6
A_KV = 2
B_HEADS = 6
B_KV = 2
C_HEADS = 4
C_QK_DIM = 32
C_V_DIM = 2 * C_QK_DIM
WINDOW = 128
ROPE_THETA = 10000.0
D_FF = 2816
EPS = 1e-6
NEG = -1e30
LOG2E = math.log2(math.e)

LANES = 128
BF16_ROWS = 16
MXU_COLS = 256
VMEM_LIMIT = 56 * 1024 * 1024

Z_QA, Z_QB, Z_QC = 0, 384, 768
Z_KA, Z_VA, Z_KB, Z_VB, Z_KC, Z_VC = 1024, 1152, 1280, 1408, 1536, 1792
D_IN = 2048
CACHE_PIECES = ((Z_KA, 128), (Z_VA, 128), (Z_KB, 128), (Z_VB, 128), (Z_KC, 256), (Z_VC, 256))
T_QA, T_VA, T_QC, T_VC, T_VB, T_QB, T_ROWS = 0, 384, 512, 768, 1024, 1152, 1536
PAIRED_HEADS = (0, 3, 1, 4, 2, 5)
FFN_CHUNK = 256
HALO = BF16_ROWS

F32 = jnp.float32
BF16 = jnp.bfloat16


def _cparams(sem):
    return pltpu.CompilerParams(dimension_semantics=sem, vmem_limit_bytes=VMEM_LIMIT)


def _const_spec(shape):
    nd = len(shape)
    return pl.BlockSpec(shape, lambda *_: (0,) * nd)


def _mod_kernel(c_ref, w_ref, b_ref, o_ref):
    cond = c_ref[...]
    a = cond / (1.0 + jnp.exp(-cond))
    o_ref[0] = jnp.dot(a.astype(BF16), w_ref[0].astype(BF16), preferred_element_type=F32) + b_ref[0]


def _modulation(conds, w_ada, b_ada):
    nb = 1536
    n_out = w_ada.shape[-1]
    return pl.pallas_call(
        _mod_kernel,
        grid=(DEPTH, n_out // nb),
        in_specs=[
            pl.BlockSpec((8, D_MODEL), lambda l, j: (0, 0)),
            pl.BlockSpec((1, D_MODEL, nb), lambda l, j: (l, 0, j)),
            pl.BlockSpec((1, 1, nb), lambda l, j: (l, 0, j)),
        ],
        out_specs=pl.BlockSpec((1, 8, nb), lambda l, j: (l, 0, j)),
        out_shape=jax.ShapeDtypeStruct((DEPTH, 8, n_out), F32),
        compiler_params=_cparams(("arbitrary", "arbitrary")),
        name="modulation",
    )(conds, w_ada, b_ada.reshape(DEPTH, 1, n_out))


def _rms(x, g):
    ms = jnp.mean(x * x, axis=-1, keepdims=True)
    return x * lax.rsqrt(ms + EPS) * g


def _head_rms(x, g, lo):
    ss = x * x
    s_lo = jnp.sum(jnp.where(lo, ss, 0.0), axis=-1, keepdims=True)
    s_hi = jnp.sum(jnp.where(lo, 0.0, ss), axis=-1, keepdims=True)
    inv = jnp.where(lo, lax.rsqrt(s_lo * (1.0 / HEAD_DIM) + EPS), lax.rsqrt(s_hi * (1.0 / HEAD_DIM) + EPS))
    return x * inv * g


def _rope(x, cos, sin, chunk, first):
    sw = jnp.where(first, pltpu.roll(x, LANES - chunk, 1), pltpu.roll(x, chunk, 1))
    return x * cos + sw * sin


def _softmax_init_t(m_ref, acc_ref):
    m_ref[...] = jnp.full(m_ref.shape, NEG, F32)
    acc_ref[...] = jnp.zeros(acc_ref.shape, F32)


def _scores_t(k, q_scr, s_ref, mx_ref, cs):
    s = jnp.dot(k, q_scr[:, cs], preferred_element_type=F32)
    s_ref[0:k.shape[0], cs] = s
    mx_ref[:, cs] = jnp.max(s, axis=0, keepdims=True)


def _probs_t(rows, s_ref, mx_ref, p_ref, a_ref, m_ref, cs):
    s = s_ref[0:rows, cs]
    m_prev = m_ref[:, cs]
    m_new = jnp.maximum(m_prev, mx_ref[:, cs])
    a_ref[:, cs] = jnp.exp2(m_prev - m_new)
    m_ref[:, cs] = m_new
    p_ref[0:rows, cs] = jnp.exp2(s - m_new).astype(BF16)


def _values_t(vt, p_ref, a_ref, acc_ref, cs):
    ones = jnp.ones((BF16_ROWS, vt.shape[1]), BF16)
    pv = jnp.dot(jnp.concatenate([vt, ones], axis=0), p_ref[0:vt.shape[1], cs], preferred_element_type=F32)
    acc_ref[:, cs] = a_ref[:, cs] * acc_ref[:, cs] + pv


def _softmax_result_t(acc_ref):
    return acc_ref[0:LANES] * (1.0 / acc_ref[LANES:LANES + 1])


def _attend_pipelined_t(q_scr, kc_ref, vct_ref, k_ref, vt_ref, tk, s_bufs, x_bufs, p_bufs, a_bufs, m_ref, acc_ref):
    n_lat = k_ref.shape[0] // tk
    n_ctx = kc_ref.shape[1]
    assert n_ctx <= tk and n_lat % 2 == 0 and n_lat >= 4

    def k_lat(j):
        return k_ref[pl.ds(pl.multiple_of(j * tk, tk), tk), :]

    def v_lat(j):
        return vt_ref[:, pl.ds(pl.multiple_of(j * tk, tk), tk)]

    n = q_scr.shape[1]
    groups = [slice(c, c + MXU_COLS) for c in range(0, n, MXU_COLS)]

    def step(par, k=None, vt=None, probs=tk):
        for cs in groups:
            if probs:
                _probs_t(probs, s_bufs[par], x_bufs[par], p_bufs[par], a_bufs[par], m_ref, cs)
            if vt is not None:
                _values_t(vt, p_bufs[1 - par], a_bufs[1 - par], acc_ref, cs)
            if k is not None:
                _scores_t(k, q_scr, s_bufs[1 - par], x_bufs[1 - par], cs)

    step(1, k=kc_ref[0], probs=0)
    step(0, k=k_lat(0), probs=n_ctx)
    step(1, k=k_lat(1), vt=vct_ref[0])

    def pair(jj, carry):
        j = 2 * jj
        step(0, k=k_lat(j + 2), vt=v_lat(j))
        step(1, k=k_lat(j + 3), vt=v_lat(j + 1))
        return carry

    for jj in range(n_lat // 2 - 1):
        pair(jj, 0)
    step(0, vt=v_lat(n_lat - 2))
    step(1, vt=v_lat(n_lat - 1), probs=0)


def _diff_lambda(lam_ref, lam_init):
    f = lambda a, b: jnp.exp(jnp.sum(a * b, axis=-1, keepdims=True))
    return f(lam_ref[0:1], lam_ref[1:2]) - f(lam_ref[2:3], lam_ref[3:4]) + lam_init


def _in_proj_kernel(*refs, use_rope, emit_cache, emit_t):
    it = iter(refs)
    x_ref, mod_ref, gn_ref, w_ref, gq_ref, gk_ref = (next(it) for _ in range(6))
    if use_rope:
        cos64, sin64, cos32, sin32 = (next(it)[...] for _ in range(4))
    prev_ref = next(it) if emit_cache == "final" else None
    z_ref = next(it)
    cache_ref = next(it) if emit_cache == "rows" else None
    final_refs = [next(it) for _ in CACHE_PIECES] if emit_cache == "final" else None
    zt_ref = next(it) if emit_t else None
    if final_refs:
        for ref, (start, width) in zip(final_refs, CACHE_PIECES):
            ref[0, 0] = prev_ref[:, start - Z_KA:start - Z_KA + width]

    x = x_ref[...]
    mod = mod_ref[0]
    h = _rms(x, gn_ref[...]) * (1.0 + mod[1:2]) + mod[0:1]
    z = jnp.dot(h.astype(BF16), w_ref[...], preferred_element_type=F32)

    rows = x.shape[0]
    lane = lax.broadcasted_iota(jnp.int32, (rows, LANES), 1)
    lo = lane < HEAD_DIM
    first16 = (lane & 31) < 16
    first8 = (lane & 15) < 8
    gq = gq_ref[...]
    gk = gk_ref[...]
    q_scale = HEAD_DIM ** -0.5 * LOG2E
    qc_scale = C_QK_DIM ** -0.5 * LOG2E

    def rope64(v):
        return _rope(v, cos64, sin64, 16, first16) if use_rope else v

    def rope32(v):
        return _rope(v, cos32, sin32, 8, first8) if use_rope else v

    def slab(off):
        return z[:, off:off + LANES]

    def put(off, v):
        z_ref[:, off:off + LANES] = v.astype(BF16)

    def put_cache(off, v):
        if cache_ref is not None:
            cache_ref[:, off - Z_KA:off - Z_KA + LANES] = v
        if final_refs:
            for ref, (start, width) in zip(final_refs, CACHE_PIECES):
                if start <= off < start + width:
                    ref[0, DEPTH - 1, :, off - start:off - start + LANES] = v

    def put_t(off, v):
        if emit_t:
            zt_ref[off:off + LANES, :] = v.T.astype(BF16)

    for s in range(3):
        qa = rope64(_head_rms(slab(Z_QA + LANES * s), gq, lo)) * q_scale
        put(Z_QA + LANES * s, qa)
        put_t(T_QA + LANES * s, qa)
        qb = rope64(slab(Z_QB + LANES * s)) * q_scale
        put(Z_QB + LANES * s, qb)
        put_t(T_QB + LANES * s, qb)
    ka = _head_rms(slab(Z_KA), gk, lo)
    put_cache(Z_KA, ka)
    put(Z_KA, rope64(ka))
    kb = slab(Z_KB)
    put_cache(Z_KB, kb)
    put(Z_KB, rope64(kb))
    for off in (Z_VA, Z_VB, Z_VC, Z_VC + LANES):
        put_cache(off, slab(off))
        put(off, slab(off))
    put_t(T_VA, slab(Z_VA))
    put_t(T_VB, slab(Z_VB))
    for s in range(2):
        put_t(T_VC + LANES * s, slab(Z_VC + LANES * s))
        qc = rope32(slab(Z_QC + LANES * s)) * qc_scale
        put(Z_QC + LANES * s, qc)
        put_t(T_QC + LANES * s, qc)
        kc = slab(Z_KC + LANES * s)
        put_cache(Z_KC + LANES * s, kc)
        put(Z_KC + LANES * s, rope32(kc))


def _in_proj(x, mod_l, gn, w_in_p, gq, gk, rope_tabs, *, tm, tiles_per_cond, tiles_per_seq, emit_cache, emit_t,
             prev_cache=None):
    t = x.shape[0]
    use_rope = rope_tabs is not None
    assert (emit_cache == "final") == (prev_cache is not None) and DEPTH == 2
    in_specs = [
        pl.BlockSpec((tm, D_MODEL), lambda i: (i, 0)),
        pl.BlockSpec((1, 6, D_MODEL), lambda i: (i // tiles_per_cond, 0, 0)),
        _const_spec((1, D_MODEL)),
        _const_spec((D_MODEL, D_IN)),
        _const_spec((1, LANES)),
        _const_spec((1, LANES)),
    ]
    args = [x, mod_l, gn, w_in_p, gq, gk]
    if use_rope:
        in_specs += [pl.BlockSpec((tm, LANES), lambda i: (i % tiles_per_seq, 0))] * 4
        args += list(rope_tabs)
    if prev_cache is not None:
        in_specs.append(pl.BlockSpec((tm, D_IN - Z_KA), lambda i: (i, 0)))
        args.append(prev_cache)
    out_shape = [jax.ShapeDtypeStruct((t, D_IN), BF16)]
    out_specs = [pl.BlockSpec((tm, D_IN), lambda i: (i, 0))]
    if emit_cache == "rows":
        out_shape.append(jax.ShapeDtypeStruct((t, D_IN - Z_KA), F32))
        out_specs.append(pl.BlockSpec((tm, D_IN - Z_KA), lambda i: (i, 0)))
    if emit_cache == "final":
        for _, width in CACHE_PIECES:
            out_shape.append(jax.ShapeDtypeStruct((t // tm, DEPTH, tm, width), F32))
            out_specs.append(pl.BlockSpec((1, DEPTH, tm, width), lambda i: (i, 0, 0, 0)))
    if emit_t:
        out_shape.append(jax.ShapeDtypeStruct((T_ROWS, t), BF16))
        out_specs.append(pl.BlockSpec((T_ROWS, tm), lambda i: (0, i)))
    return pl.pallas_call(
        functools.partial(_in_proj_kernel, use_rope=use_rope, emit_cache=emit_cache, emit_t=emit_t),
        grid=(t // tm,),
        in_specs=in_specs,
        out_specs=out_specs,
        out_shape=out_shape,
        compiler_params=_cparams(("arbitrary",)),
        name="in_proj",
    )(*args)


def _gqa_queries_t(qt_ref, lo):
    slabs = [qt_ref[LANES * s:LANES * (s + 1), :] for s in range(3)]
    zero = jnp.zeros_like(slabs[0])
    return jnp.concatenate([jnp.where(lo, s, zero) for s in slabs] + [jnp.where(lo, zero, s) for s in slabs], axis=1)


def _gqa_store_t(ot, o_ref, tq, lo, col0=0):
    for s in range(3):
        slab_t = jnp.where(lo, ot[:, s * tq:(s + 1) * tq], ot[:, (3 + s) * tq:(4 + s) * tq])
        o_ref[:, col0 + LANES * s:col0 + LANES * (s + 1)] = slab_t.T.astype(BF16)


def _diff_queries_t(qt, row):
    zero = jnp.zeros_like(qt)
    return jnp.concatenate([jnp.where((row >= C_QK_DIM * j) & (row < C_QK_DIM * (j + 1)), qt, zero)
                            for j in range(4)], axis=1)


def _diff_output_t(ot, tq, lam, gsub, lam_init, row):
    o_even = ot[:, 0:tq] - lam * ot[:, tq:2 * tq]
    o_odd = ot[:, 2 * tq:3 * tq] - lam * ot[:, 3 * tq:4 * tq]
    oc = jnp.where(row < C_V_DIM, o_even, o_odd).T
    lo = lax.broadcasted_iota(jnp.int32, (tq, LANES), 1) < C_V_DIM
    return (_head_rms(oc, gsub, lo) * (1.0 - lam_init)).astype(BF16)


def _attn_ctx_kernel(sink_ref, lam_ref, gsub_ref, z_ref, zt_ref, o_ref, *, lam_init):
    tq = z_ref.shape[0]
    row = lax.broadcasted_iota(jnp.int32, (LANES, tq), 0)
    lo = row < HEAD_DIM

    def attend(q, k, vt, sink=None):
        s = jnp.dot(k, q, preferred_element_type=F32)
        m = jnp.max(s, axis=0, keepdims=True)
        if sink is not None:
            m = jnp.maximum(m, sink)
        p = jnp.exp2(s - m)
        l = jnp.sum(p, axis=0, keepdims=True)
        if sink is not None:
            l = l + jnp.exp2(sink - m)
        return jnp.dot(vt, p.astype(BF16), preferred_element_type=F32) * (1.0 / l)

    def keys(off):
        return z_ref[:, off:off + LANES]

    def feat(off):
        return zt_ref[off:off + LANES, :]

    _gqa_store_t(attend(_gqa_queries_t(zt_ref.at[T_QA:T_QA + 384], lo), keys(Z_KA), feat(T_VA)), o_ref, tq, lo)
    sink = jnp.concatenate([jnp.full((1, tq), sink_ref[h] * LOG2E, F32) for h in range(B_HEADS)], axis=1)
    _gqa_store_t(attend(_gqa_queries_t(zt_ref.at[T_QB:T_QB + 384], lo), keys(Z_KB), feat(T_VB), sink),
                 o_ref, tq, lo, col0=384)
    lam = _diff_lambda(lam_ref, lam_init)
    for s in range(2):
        ot = attend(_diff_queries_t(feat(T_QC + LANES * s), row), keys(Z_KC + LANES * s), feat(T_VC + LANES * s))
        o_ref[:, 768 + LANES * s:768 + LANES * (s + 1)] = _diff_output_t(ot, tq, lam, gsub_ref[...], lam_init, row)


def _attn_ctx(z, zt, sink, lam_par, gsub, *, seq, lam_init):
    t = z.shape[0]
    return pl.pallas_call(
        functools.partial(_attn_ctx_kernel, lam_init=lam_init),
        grid=(t // seq,),
        in_specs=[
            pl.BlockSpec(memory_space=pltpu.SMEM),
            _const_spec((8, LANES)),
            _const_spec((1, LANES)),
            pl.BlockSpec((seq, D_IN), lambda b: (b, 0)),
            pl.BlockSpec((T_ROWS, seq), lambda b: (0, b)),
        ],
        out_specs=pl.BlockSpec((seq, D_MODEL), lambda b: (b, 0)),
        out_shape=jax.ShapeDtypeStruct((t, D_MODEL), BF16),
        compiler_params=_cparams(("arbitrary",)),
        name="attn_ctx",
    )(sink, lam_par, gsub, z, zt)


def _attn_a_kernel(qt_ref, kc_ref, vct_ref, k_ref, vt_ref, o_ref, q_scr, m_ref, acc_ref,
                   s0, s1, x0, x1, p0, p1, a0, a1, *, tk):
    tq = qt_ref.shape[1]
    lo = lax.broadcasted_iota(jnp.int32, (LANES, tq), 0) < HEAD_DIM
    refs = (m_ref, acc_ref)
    q_scr[...] = _gqa_queries_t(qt_ref, lo)
    _softmax_init_t(*refs)
    _attend_pipelined_t(q_scr, kc_ref, vct_ref, k_ref, vt_ref, tk, (s0, s1), (x0, x1), (p0, p1), (a0, a1), *refs)
    _gqa_store_t(_softmax_result_t(acc_ref), o_ref, tq, lo)


def _keys_major_scratch(n, tk):
    return [pltpu.VMEM((LANES, n), BF16), pltpu.VMEM((1, n), F32),
            pltpu.VMEM((LANES + BF16_ROWS, n), F32),
            pltpu.VMEM((tk, n), F32), pltpu.VMEM((tk, n), F32),
            pltpu.VMEM((1, n), F32), pltpu.VMEM((1, n), F32),
            pltpu.VMEM((tk, n), BF16), pltpu.VMEM((tk, n), BF16),
            pltpu.VMEM((1, n), F32), pltpu.VMEM((1, n), F32)]


def _attn_a(z, zt, k_ctx, vt_ctx, *, seq, tq, tk):
    t = z.shape[0]
    nq = seq // tq
    n_ctx = k_ctx.shape[1]
    n = A_HEADS * tq
    return pl.pallas_call(
        functools.partial(_attn_a_kernel, tk=tk),
        grid=(t // seq, nq),
        in_specs=[
            pl.BlockSpec((384, tq), lambda b, i: (T_QA // 384, b * nq + i)),
            pl.BlockSpec((1, n_ctx, LANES), lambda b, i: (b, 0, 0)),
            pl.BlockSpec((1, LANES, n_ctx), lambda b, i: (b, 0, 0)),
            pl.BlockSpec((seq, LANES), lambda b, i: (b, Z_KA // LANES)),
            pl.BlockSpec((LANES, seq), lambda b, i: (T_VA // LANES, b)),
        ],
        out_specs=pl.BlockSpec((tq, 384), lambda b, i: (b * nq + i, 0)),
        out_shape=jax.ShapeDtypeStruct((t, 384), BF16),
        scratch_shapes=_keys_major_scratch(n, tk),
        compiler_params=_cparams(("arbitrary", "arbitrary")),
        name="attn_a",
    )(zt, k_ctx, vt_ctx, z, zt)


def _attn_b_kernel(sink_ref, qt_ref, kc_ref, vct_ref, k_ref, vt_ref, o_ref):
    tq = qt_ref.shape[1]
    seq = k_ref.shape[0]
    n = B_HEADS * tq
    band = tq + 2 * WINDOW
    i = pl.program_id(1)
    lo = lax.broadcasted_iota(jnp.int32, (LANES, tq), 0) < HEAD_DIM
    q = _gqa_queries_t(qt_ref, lo)
    start = pl.multiple_of(jnp.clip(i * tq - WINDOW, 0, seq - band), LANES)
    s_c = jnp.dot(kc_ref[0], q, preferred_element_type=F32)
    s_b = jnp.dot(k_ref[pl.ds(start, band), :], q, preferred_element_type=F32)
    qpos = i * tq + lax.broadcasted_iota(jnp.int32, (band, tq), 1)
    kpos = start + lax.broadcasted_iota(jnp.int32, (band, tq), 0)
    inside = jnp.abs(kpos - qpos) <= WINDOW
    s_b = jnp.concatenate([jnp.where(inside, s_b[:, h * tq:(h + 1) * tq], NEG) for h in range(B_HEADS)], axis=1)
    sink = jnp.concatenate([jnp.full((1, tq), sink_ref[h] * LOG2E, F32) for h in range(B_HEADS)], axis=1)
    m = jnp.maximum(sink, jnp.maximum(jnp.max(s_c, axis=0, keepdims=True), jnp.max(s_b, axis=0, keepdims=True)))
    p_c = jnp.exp2(s_c - m)
    p_b = jnp.exp2(s_b - m)
    l = jnp.exp2(sink - m) + jnp.sum(p_c, axis=0, keepdims=True) + jnp.sum(p_b, axis=0, keepdims=True)
    acc = (jnp.dot(vct_ref[0], p_c.astype(BF16), preferred_element_type=F32)
           + jnp.dot(vt_ref[:, pl.ds(start, band)], p_b.astype(BF16), preferred_element_type=F32))
    _gqa_store_t(acc * (1.0 / l), o_ref, tq, lo)


def _attn_b(z, zt, k_ctx, vt_ctx, sink, *, seq, tq):
    t = z.shape[0]
    nq = seq // tq
    n_ctx = k_ctx.shape[1]
    return pl.pallas_call(
        _attn_b_kernel,
        grid=(t // seq, nq),
        in_specs=[
            pl.BlockSpec(memory_space=pltpu.SMEM),
            pl.BlockSpec((384, tq), lambda b, i: (T_QB // 384, b * nq + i)),
            pl.BlockSpec((1, n_ctx, LANES), lambda b, i: (b, 0, 0)),
            pl.BlockSpec((1, LANES, n_ctx), lambda b, i: (b, 0, 0)),
            pl.BlockSpec((seq, LANES), lambda b, i: (b, Z_KB // LANES)),
            pl.BlockSpec((LANES, seq), lambda b, i: (T_VB // LANES, b)),
        ],
        out_specs=pl.BlockSpec((tq, 384), lambda b, i: (b * nq + i, 0)),
        out_shape=jax.ShapeDtypeStruct((t, 384), BF16),
        compiler_params=_cparams(("arbitrary", "arbitrary")),
        name="attn_b",
    )(sink, zt, k_ctx, vt_ctx, z, zt)


def _attn_c_kernel(lam_ref, gsub_ref, qt_ref, kc_ref, vct_ref, k_ref, vt_ref, o_ref,
                   q_scr, m_ref, acc_ref, s0, s1, x0, x1, p0, p1, a0, a1, *, tk, lam_init):
    tq = qt_ref.shape[1]
    row = lax.broadcasted_iota(jnp.int32, (LANES, tq), 0)
    refs = (m_ref, acc_ref)
    q_scr[...] = _diff_queries_t(qt_ref[...], row)
    _softmax_init_t(*refs)
    _attend_pipelined_t(q_scr, kc_ref, vct_ref, k_ref, vt_ref, tk, (s0, s1), (x0, x1), (p0, p1), (a0, a1), *refs)
    lam = _diff_lambda(lam_ref, lam_init)
    o_ref[...] = _diff_output_t(_softmax_result_t(acc_ref), tq, lam, gsub_ref[...], lam_init, row)


def _attn_c(z, zt, k_ctx, vt_ctx, lam_par, gsub, *, seq, tq, tk, lam_init):
    t = z.shape[0]
    nq = seq // tq
    n_ctx = k_ctx.shape[1]
    n = 4 * tq
    return pl.pallas_call(
        functools.partial(_attn_c_kernel, tk=tk, lam_init=lam_init),
        grid=(t // seq, 2, nq),
        in_specs=[
            _const_spec((8, LANES)),
            _const_spec((1, LANES)),
            pl.BlockSpec((LANES, tq), lambda b, s, i: (T_QC // LANES + s, b * nq + i)),
            pl.BlockSpec((1, n_ctx, LANES), lambda b, s, i: (b, 0, s)),
            pl.BlockSpec((1, LANES, n_ctx), lambda b, s, i: (b, s, 0)),
            pl.BlockSpec((seq, LANES), lambda b, s, i: (b, Z_KC // LANES + s)),
            pl.BlockSpec((LANES, seq), lambda b, s, i: (T_VC // LANES + s, b)),
        ],
        out_specs=pl.BlockSpec((tq, LANES), lambda b, s, i: (b * nq + i, s)),
        out_shape=jax.ShapeDtypeStruct((t, C_HEADS * C_V_DIM), BF16),
        scratch_shapes=_keys_major_scratch(n, tk),
        compiler_params=_cparams(("arbitrary", "arbitrary", "arbitrary")),
        name="attn_c",
    )(lam_par, gsub, zt, k_ctx, vt_ctx, z, zt)


def _post_kernel(*refs, n_parts, halo, tiles_per_seq, final):
    it = iter(refs)
    tiles = [next(it) for _ in range(1 + n_parts)]
    halos = [(next(it), next(it)) for _ in range(1 + n_parts)] if halo else None
    (mod_ref, gn_ref, wo_ref, wu_ref, cw_ref, cb_ref, wd_ref, gf_ref, out_ref, act_scr) = (next(it) for _ in range(10))

    def rows_of(j):
        if halo:
            return jnp.concatenate([halos[j][0][...], tiles[j][...], halos[j][1][...]], axis=0)
        return tiles[j][...]

    tm = tiles[0].shape[0]
    x = rows_of(0)
    o = jnp.concatenate([rows_of(j) for j in range(1, 1 + n_parts)], axis=1)
    ext = x.shape[0]
    mod = mod_ref[0]
    x1 = x + mod[2:3] * jnp.dot(o, wo_ref[...], preferred_element_type=F32)
    h = _rms(x1, gn_ref[...]) * (1.0 + mod[4:5]) + mod[3:4]
    row = lax.broadcasted_iota(jnp.int32, (ext, 1), 0)
    if halo:
        t_in_seq = pl.program_id(0) % tiles_per_seq
        keep = ((row >= halo) | (t_in_seq > 0)) & ((row < halo + tm) | (t_in_seq < tiles_per_seq - 1))
        h = jnp.where(keep, h, 0.0)
    h = h.astype(BF16)

    def conv(u, c0):
        cw = cw_ref[:, c0:c0 + FFN_CHUNK]
        up = pltpu.roll(u, 1, 0)
        dn = pltpu.roll(u, ext - 1, 0)
        if not halo:
            up = jnp.where(row == 0, 0.0, up)
            dn = jnp.where(row == ext - 1, 0.0, dn)
        v = cw[0:1] * up + cw[1:2] * u + cw[2:3] * dn + cb_ref[:, c0:c0 + FFN_CHUNK]
        return v[halo:halo + tm]

    def up_proj(c):
        ca, cg = c * FFN_CHUNK, D_FF + c * FFN_CHUNK
        return (jnp.dot(h, wu_ref[:, ca:ca + FFN_CHUNK], preferred_element_type=F32),
                jnp.dot(h, wu_ref[:, cg:cg + FFN_CHUNK], preferred_element_type=F32))

    n_chunks = D_FF // FFN_CHUNK
    nxt = up_proj(0)
    for c in range(n_chunks):
        ua, ug = nxt
        if c + 1 < n_chunks:
            nxt = up_proj(c + 1)
        ca = c * FFN_CHUNK
        a = conv(ua, ca)
        g = conv(ug, D_FF + ca)
        act_scr[:, ca:ca + FFN_CHUNK] = (a / (1.0 + jnp.exp(-a)) * g).astype(BF16)
    x2 = x1[halo:halo + tm] + mod[5:6] * jnp.dot(act_scr[...], wd_ref[...], preferred_element_type=F32)
    if final:
        x2 = _rms(x2, gf_ref[...])
    out_ref[...] = x2


def _post(x, o_parts, mod_l, gn2, wo, wu, cw, cb, wd, gf, *, tm, tiles_per_cond, tiles_per_seq, final):
    t = x.shape[0]
    assert sum(o.shape[1] for o in o_parts) == D_MODEL
    halo = HALO if tiles_per_seq > 1 else 0
    tile = lambda i: (i, 0)
    rows = [x] + list(o_parts)
    in_specs = [pl.BlockSpec((tm, a.shape[1]), tile) for a in rows]
    args = list(rows)
    if halo:
        per = tm // halo
        prev = lambda i: (jnp.maximum(i * per - 1, 0), 0)
        nxt = lambda i: (jnp.minimum((i + 1) * per, t // halo - 1), 0)
        for a in rows:
            in_specs += [pl.BlockSpec((halo, a.shape[1]), prev), pl.BlockSpec((halo, a.shape[1]), nxt)]
            args += [a, a]
    in_specs += [
        pl.BlockSpec((1, 6, D_MODEL), lambda i: (i // tiles_per_cond, 0, 0)),
        _const_spec((1, D_MODEL)),
        _const_spec((D_MODEL, D_MODEL)),
        _const_spec((D_MODEL, 2 * D_FF)),
        _const_spec((3, 2 * D_FF)),
        _const_spec((1, 2 * D_FF)),
        _const_spec((D_FF, D_MODEL)),
        _const_spec((1, D_MODEL)),
    ]
    args += [mod_l, gn2, wo, wu, cw, cb, wd, gf]
    return pl.pallas_call(
        functools.partial(_post_kernel, n_parts=len(o_parts), halo=halo, tiles_per_seq=tiles_per_seq, final=final),
        grid=(t // tm,),
        in_specs=in_specs,
        out_specs=pl.BlockSpec((tm, D_MODEL), tile),
        out_shape=jax.ShapeDtypeStruct((t, D_MODEL), F32),
        scratch_shapes=[pltpu.VMEM((tm, D_FF), BF16)],
        compiler_params=_cparams(("arbitrary",)),
        name="post",
    )(*args)


def _rope_tables(seq):
    t = jnp.arange(seq)
    rows = (t // GRID_W).astype(F32)[:, None]
    cols = (t % GRID_W).astype(F32)[:, None]

    def tab(half, reps):
        inv = ROPE_THETA ** (-jnp.arange(half, dtype=F32) / half)
        ar, ac = rows * inv[None, :], cols * inv[None, :]
        cos = jnp.concatenate([jnp.cos(ar), jnp.cos(ar), jnp.cos(ac), jnp.cos(ac)], axis=-1)
        sin = jnp.concatenate([-jnp.sin(ar), jnp.sin(ar), -jnp.sin(ac), jnp.sin(ac)], axis=-1)
        return jnp.tile(cos, (1, reps)), jnp.tile(sin, (1, reps))

    cos64, sin64 = tab(HEAD_DIM // 4, LANES // HEAD_DIM)
    cos32, sin32 = tab(C_QK_DIM // 4, LANES // C_QK_DIM)
    return cos64, sin64, cos32, sin32


def _w_in_perm():
    sizes = (A_HEADS * HEAD_DIM, A_KV * HEAD_DIM, A_KV * HEAD_DIM, B_HEADS * HEAD_DIM, B_KV * HEAD_DIM,
             B_KV * HEAD_DIM, C_HEADS * 2 * C_QK_DIM, C_HEADS * 2 * C_QK_DIM, C_HEADS * C_V_DIM)
    offs = np.concatenate([[0], np.cumsum(sizes)])
    qa, ka, va, qb, kb, vb, qc, kc, vc = (np.arange(offs[j], offs[j + 1]) for j in range(9))
    pair = np.concatenate([np.arange(h * HEAD_DIM, (h + 1) * HEAD_DIM) for h in PAIRED_HEADS])
    return np.concatenate([qa[pair], qb[pair], qc, ka, va, kb, vb, kc, vc])


def _w_out_perm():
    pair = np.concatenate([np.arange(h * HEAD_DIM, (h + 1) * HEAD_DIM) for h in PAIRED_HEADS])
    return np.concatenate([pair, A_HEADS * HEAD_DIM + pair, np.arange(768, D_MODEL)])


def _take_runs(w, perm, axis):
    cuts = [0] + [j for j in range(1, len(perm)) if perm[j] != perm[j - 1] + 1] + [len(perm)]
    parts = [lax.slice_in_dim(w, int(perm[a]), int(perm[b - 1]) + 1, axis=axis) for a, b in zip(cuts[:-1], cuts[1:])]
    return jnp.concatenate(parts, axis=axis)


def kernel(x_prompt, x_sample, cache_a_k, cache_a_v, cache_b_k, cache_b_v, cache_c_k, cache_c_v, c, c_ctx, w_ada, b_ada, g_norm1, g_norm2, w_in, g_qa, g_ka, sink_b, lam_q1, lam_k1, lam_q2, lam_k2, g_subln, w_out, w_up, conv_w, conv_b, w_down, g_final):
    n_ctx_req, ctx_len, _ = x_prompt.shape
    n_lat_req, lat_len, _ = x_sample.shape
    past = cache_a_k.shape[2]

    conds = jnp.zeros((8, D_MODEL), F32).at[0].set(c_ctx).at[1:1 + n_lat_req].set(c)
    mod = _modulation(conds, w_ada, b_ada).reshape(DEPTH, 8, 6, D_MODEL)

    w_in_p = [_take_runs(w_in[l], _w_in_perm(), 1).astype(BF16) for l in range(DEPTH)]
    w_out_p = [_take_runs(w_out[l], _w_out_perm(), 0).astype(BF16) for l in range(DEPTH)]
    w_up_b = [w_up[l].astype(BF16) for l in range(DEPTH)]
    w_down_b = [w_down[l].astype(BF16) for l in range(DEPTH)]
    rope_tabs = _rope_tables(lat_len)
    zpad = jnp.zeros((DEPTH, LANES - C_QK_DIM), F32)
    lam_rows = [jnp.concatenate([v, zpad], axis=-1) for v in (lam_q1, lam_k1, lam_q2, lam_k2)]
    lam_par = jnp.concatenate([jnp.stack(lam_rows, axis=1), jnp.zeros((DEPTH, 4, LANES), F32)], axis=1)
    gf = g_final.reshape(1, D_MODEL)

    xc = x_prompt.reshape(n_ctx_req * ctx_len, D_MODEL)
    xs = x_sample.reshape(n_lat_req * lat_len, D_MODEL)
    lat_tm = 512
    caches = []
    for l in range(DEPTH):
        lam_init = 0.8 - 0.6 * math.exp(-0.3 * l)
        gn1 = g_norm1[l].reshape(1, D_MODEL)
        gn2 = g_norm2[l].reshape(1, D_MODEL)
        gq = jnp.tile(g_qa[l], LANES // HEAD_DIM).reshape(1, LANES)
        gk = jnp.tile(g_ka[l], LANES // HEAD_DIM).reshape(1, LANES)
        gsub = jnp.tile(g_subln[l], LANES // C_V_DIM).reshape(1, LANES)
        post_w = (gn2, w_out_p[l], w_up_b[l], conv_w[l], conv_b[l].reshape(1, 2 * D_FF), w_down_b[l], gf)
        final = l == DEPTH - 1

        n_tiles = n_ctx_req
        zc, *caches, ztc = _in_proj(xc, mod[l, 0:1], gn1, w_in_p[l], gq, gk, None, tm=ctx_len, tiles_per_cond=n_tiles,
                                    tiles_per_seq=1, emit_cache="final" if final else "rows", emit_t=True,
                                    prev_cache=caches[0] if final else None)
        oc = _attn_ctx(zc, ztc, sink_b[l], lam_par[l], gsub, seq=ctx_len, lam_init=lam_init)
        xc = _post(xc, [oc], mod[l, 0:1], *post_w, tm=ctx_len, tiles_per_cond=n_tiles, tiles_per_seq=1, final=final)

        per_seq = lat_len // lat_tm
        zs, zts = _in_proj(xs, mod[l, 1:1 + n_lat_req], gn1, w_in_p[l], gq, gk, rope_tabs, tm=lat_tm,
                           tiles_per_cond=per_seq, tiles_per_seq=per_seq, emit_cache=None, emit_t=True)
        flat = lambda a: a[:, l].reshape(n_lat_req, past, -1).astype(BF16)
        flat_t = lambda a: jnp.swapaxes(flat(a), 1, 2)
        oa = _attn_a(zs, zts, flat(cache_a_k), flat_t(cache_a_v), seq=lat_len, tq=128, tk=512)
        ob = _attn_b(zs, zts, flat(cache_b_k), flat_t(cache_b_v), sink_b[l], seq=lat_len, tq=128)
        oc = _attn_c(zs, zts, flat(cache_c_k), flat_t(cache_c_v), lam_par[l], gsub, seq=lat_len, tq=256, tk=512,
                     lam_init=lam_init)
        xs = _post(xs, [oa, ob, oc], mod[l, 1:1 + n_lat_req], *post_w, tm=lat_tm, tiles_per_cond=per_seq,
                   tiles_per_seq=per_seq, final=final)

    heads = (A_KV, A_KV, B_KV, B_KV, C_HEADS, C_HEADS)
    new_caches = tuple(cch.reshape(n_ctx_req, DEPTH, ctx_len, h, -1) for cch, h in zip(caches, heads))
    return (xc.reshape(x_prompt.shape), xs.reshape(x_sample.shape)) + new_caches
```

```python
import functools
import math

import numpy as np
import jax
import jax.numpy as jnp
from jax import lax
from jax.experimental import pallas as pl
from jax.experimental.pallas import tpu as pltpu

D_MODEL = 1024
DEPTH = 2
GRID_W = 64
HEAD_DIM = 64
A_HEADS = 6
A_KV = 2
B_HEADS = 6
B_KV = 2
C_HEADS = 4
C_QK_DIM = 32
C_V_DIM = 2 * C_QK_DIM
WINDOW = 128
ROPE_THETA = 10000.0
D_FF = 2816
EPS = 1e-6
NEG = -1e30
LOG2E = math.log2(math.e)

LANES = 128
BF16_ROWS = 16
MXU_COLS = 256
VMEM_LIMIT = 56 * 1024 * 1024

Z_QA, Z_QB, Z_QC = 0, 384, 768
Z_KA, Z_VA, Z_KB, Z_VB, Z_KC, Z_VC = 1024, 1152, 1280, 1408, 1536, 1792
D_IN = 2048
CACHE_PIECES = ((Z_KA, 128), (Z_VA, 128), (Z_KB, 128), (Z_VB, 128), (Z_KC, 256), (Z_VC, 256))
K_A, K_B, K_C, K_COLS = 0, 128, 256, 512
T_QA, T_VA, T_QC, T_VC, T_VB, T_QB, T_ROWS = 0, 384, 512, 768, 1024, 1152, 1536
PAIRED_HEADS = (0, 3, 1, 4, 2, 5)
FFN_CHUNK = 256
HALO = BF16_ROWS

F32 = jnp.float32
BF16 = jnp.bfloat16


def _cparams(sem):
    return pltpu.CompilerParams(dimension_semantics=sem, vmem_limit_bytes=VMEM_LIMIT)


def _const_spec(shape):
    nd = len(shape)
    return pl.BlockSpec(shape, lambda *_: (0,) * nd)


def _layer_spec(shape, layer):
    nd = len(shape)
    return pl.BlockSpec((1,) + tuple(shape), lambda *_: (layer,) + (0,) * nd)


def _mod_kernel(c_ref, w_ref, b_ref, o_ref):
    cond = c_ref[...]
    a = cond / (1.0 + jnp.exp(-cond))
    o_ref[0] = jnp.dot(a.astype(BF16), w_ref[0].astype(BF16), preferred_element_type=F32) + b_ref[0]


def _modulation(conds, w_ada, b_ada):
    nb = 1536
    n_out = w_ada.shape[-1]
    return pl.pallas_call(
        _mod_kernel,
        grid=(DEPTH, n_out // nb),
        in_specs=[
            pl.BlockSpec((8, D_MODEL), lambda l, j: (0, 0)),
            pl.BlockSpec((1, D_MODEL, nb), lambda l, j: (l, 0, j)),
            pl.BlockSpec((1, 1, nb), lambda l, j: (l, 0, j)),
        ],
        out_specs=pl.BlockSpec((1, 8, nb), lambda l, j: (l, 0, j)),
        out_shape=jax.ShapeDtypeStruct((DEPTH, 8, n_out), F32),
        compiler_params=_cparams(("arbitrary", "arbitrary")),
        name="modulation",
    )(conds, w_ada, b_ada.reshape(DEPTH, 1, n_out))


def _rms(x, g):
    ms = jnp.mean(x * x, axis=-1, keepdims=True)
    return x * lax.rsqrt(ms + EPS) * g


def _head_rms(x, g, lo):
    ss = x * x
    s_lo = jnp.sum(jnp.where(lo, ss, 0.0), axis=-1, keepdims=True)
    s_hi = jnp.sum(jnp.where(lo, 0.0, ss), axis=-1, keepdims=True)
    inv = jnp.where(lo, lax.rsqrt(s_lo * (1.0 / HEAD_DIM) + EPS), lax.rsqrt(s_hi * (1.0 / HEAD_DIM) + EPS))
    return x * inv * g


def _rope(x, cos, sin, chunk, first):
    sw = jnp.where(first, pltpu.roll(x, LANES - chunk, 1), pltpu.roll(x, chunk, 1))
    return x * cos + sw * sin


def _softmax_init_t(m_ref, acc_ref):
    m_ref[...] = jnp.full(m_ref.shape, NEG, F32)
    acc_ref[...] = jnp.zeros(acc_ref.shape, F32)


def _scores_t(k, q_scr, s_ref, mx_ref, cs):
    s = jnp.dot(k, q_scr[:, cs], preferred_element_type=F32)
    s_ref[0:k.shape[0], cs] = s
    mx_ref[:, cs] = jnp.max(s, axis=0, keepdims=True)


def _probs_t(rows, s_ref, mx_ref, p_ref, a_ref, m_ref, cs):
    s = s_ref[0:rows, cs]
    m_prev = m_ref[:, cs]
    m_new = jnp.maximum(m_prev, mx_ref[:, cs])
    a_ref[:, cs] = jnp.exp2(m_prev - m_new)
    m_ref[:, cs] = m_new
    p_ref[0:rows, cs] = jnp.exp2(s - m_new).astype(BF16)


def _values_t(vt, p_ref, a_ref, acc_ref, cs):
    ones = jnp.ones((BF16_ROWS, vt.shape[1]), BF16)
    pv = jnp.dot(jnp.concatenate([vt, ones], axis=0), p_ref[0:vt.shape[1], cs], preferred_element_type=F32)
    acc_ref[:, cs] = a_ref[:, cs] * acc_ref[:, cs] + pv


def _softmax_result_t(acc_ref):
    return acc_ref[0:LANES] * (1.0 / acc_ref[LANES:LANES + 1])


def _attend_pipelined_t(q_scr, kc_ref, vct_ref, k_ref, vt_ref, tk, s_bufs, x_bufs, p_bufs, a_bufs, m_ref, acc_ref):
    n_lat = k_ref.shape[0] // tk
    n_ctx = kc_ref.shape[1]
    assert n_ctx <= tk and n_lat % 2 == 0 and n_lat >= 4

    def k_lat(j):
        return k_ref[pl.ds(pl.multiple_of(j * tk, tk), tk), :]

    def v_lat(j):
        return vt_ref[:, pl.ds(pl.multiple_of(j * tk, tk), tk)]

    n = q_scr.shape[1]
    groups = [slice(c, c + MXU_COLS) for c in range(0, n, MXU_COLS)]

    def step(par, k=None, vt=None, probs=tk):
        for cs in groups:
            if probs:
                _probs_t(probs, s_bufs[par], x_bufs[par], p_bufs[par], a_bufs[par], m_ref, cs)
            if vt is not None:
                _values_t(vt, p_bufs[1 - par], a_bufs[1 - par], acc_ref, cs)
            if k is not None:
                _scores_t(k, q_scr, s_bufs[1 - par], x_bufs[1 - par], cs)

    step(1, k=kc_ref[0], probs=0)
    step(0, k=k_lat(0), probs=n_ctx)
    step(1, k=k_lat(1), vt=vct_ref[0])

    def pair(jj, carry):
        j = 2 * jj
        step(0, k=k_lat(j + 2), vt=v_lat(j))
        step(1, k=k_lat(j + 3), vt=v_lat(j + 1))
        return carry

    for jj in range(n_lat // 2 - 1):
        pair(jj, 0)
    step(0, vt=v_lat(n_lat - 2))
    step(1, vt=v_lat(n_lat - 1), probs=0)


def _diff_lambda(lam_ref, lam_init):
    f = lambda a, b: jnp.exp(jnp.sum(a * b, axis=-1, keepdims=True))
    return f(lam_ref[0:1], lam_ref[1:2]) - f(lam_ref[2:3], lam_ref[3:4]) + lam_init


def _swap_row_chunks(x, chunk):
    parts = []
    for r in range(0, x.shape[0], 2 * chunk):
        parts += [x[r + chunk:r + 2 * chunk], x[r:r + chunk]]
    return jnp.concatenate(parts, axis=0)


def _in_proj_kernel(*refs, use_rope, emit_cache):
    it = iter(refs)
    x_ref, mod_ref, gn_ref, wt_ref, wk_ref, gq_ref, gk_ref = (next(it) for _ in range(7))
    cos64 = sin64 = cos32 = sin32 = cos64t = sin64t = cos32t = sin32t = None
    if use_rope:
        cos64, sin64, cos32, sin32, cos64t, sin64t, cos32t, sin32t = (next(it)[...] for _ in range(8))
    prev_ref = next(it) if emit_cache == "final" else None
    zk_ref = next(it)
    cache_ref = next(it) if emit_cache == "rows" else None
    final_refs = [next(it) for _ in CACHE_PIECES] if emit_cache == "final" else None
    zt_ref = next(it)
    if final_refs:
        for ref, (start, width) in zip(final_refs, CACHE_PIECES):
            ref[0, 0] = prev_ref[:, start - Z_KA:start - Z_KA + width]

    tm = x_ref.shape[0]
    sub = min(tm, MXU_COLS)
    mod = mod_ref[0]
    lane = lax.broadcasted_iota(jnp.int32, (sub, LANES), 1)
    lo = lane < HEAD_DIM
    first16 = (lane & 31) < 16
    first8 = (lane & 15) < 8
    q_scale = HEAD_DIM ** -0.5 * LOG2E
    qc_scale = C_QK_DIM ** -0.5 * LOG2E

    def matmuls(r0):
        h = (_rms(x_ref[r0:r0 + sub], gn_ref[...]) * (1.0 + mod[1:2]) + mod[0:1]).astype(BF16)
        zk = jnp.dot(h, wk_ref[0], preferred_element_type=F32)
        zt = lax.dot_general(wt_ref[0], h, (((1,), (1,)), ((), ())), preferred_element_type=F32)
        return zk, zt

    def put_cache(r0, off, v):
        if cache_ref is not None:
            cache_ref[r0:r0 + sub, off - Z_KA:off - Z_KA + LANES] = v
        if final_refs:
            for ref, (start, width) in zip(final_refs, CACHE_PIECES):
                if start <= off < start + width:
                    ref[0, DEPTH - 1, r0:r0 + sub, off - start:off - start + LANES] = v

    def finish_keys(r0, zk):
        def tab(t):
            return t[r0:r0 + sub] if use_rope else None

        def rope64(v):
            return _rope(v, tab(cos64), tab(sin64), 16, first16) if use_rope else v

        def rope32(v):
            return _rope(v, tab(cos32), tab(sin32), 8, first8) if use_rope else v

        def put_k(off, v):
            zk_ref[r0:r0 + sub, off:off + LANES] = v.astype(BF16)

        ka = _head_rms(zk[:, K_A:K_A + LANES], gk_ref[...], lo)
        put_cache(r0, Z_KA, ka)
        put_k(K_A, rope64(ka))
        kb = zk[:, K_B:K_B + LANES]
        put_cache(r0, Z_KB, kb)
        put_k(K_B, rope64(kb))
        for s in range(2):
            kc = zk[:, K_C + LANES * s:K_C + LANES * (s + 1)]
            put_cache(r0, Z_KC + LANES * s, kc)
            put_k(K_C + LANES * s, rope32(kc))

    def finish_features(r0, zt):
        def t_slab(off):
            return zt[off:off + LANES]

        def put_t(off, v):
            zt_ref[off:off + LANES, r0:r0 + sub] = v.astype(BF16)

        def rope_t(v, cos_t, sin_t, chunk):
            if not use_rope:
                return v
            return v * cos_t[:, r0:r0 + sub] + _swap_row_chunks(v, chunk) * sin_t[:, r0:r0 + sub]

        def head_rms_t(v):
            ss = v * v
            halves = []
            for r in (0, HEAD_DIM):
                inv = lax.rsqrt(jnp.sum(ss[r:r + HEAD_DIM], axis=0, keepdims=True) * (1.0 / HEAD_DIM) + EPS)
                halves.append(v[r:r + HEAD_DIM] * inv)
            return jnp.concatenate(halves, axis=0) * gq_ref[:, 0:sub]

        for s in range(3):
            put_t(T_QA + LANES * s, rope_t(head_rms_t(t_slab(T_QA + LANES * s)), cos64t, sin64t, 16) * q_scale)
            put_t(T_QB + LANES * s, rope_t(t_slab(T_QB + LANES * s), cos64t, sin64t, 16) * q_scale)
        for s in range(2):
            put_t(T_QC + LANES * s, rope_t(t_slab(T_QC + LANES * s), cos32t, sin32t, 8) * qc_scale)
        for t_off, z_off in ((T_VA, Z_VA), (T_VB, Z_VB), (T_VC, Z_VC), (T_VC + LANES, Z_VC + LANES)):
            put_t(t_off, t_slab(t_off))
            if emit_cache:
                put_cache(r0, z_off, t_slab(t_off).T)

    starts = list(range(0, tm, sub))
    prods = [matmuls(r0) for r0 in starts]
    for r0, (zk, _) in zip(starts, prods):
        finish_keys(r0, zk)
    for r0, (_, zt) in zip(starts, prods):
        finish_features(r0, zt)


def _in_proj(x, mod_l, gn, w_t, w_k, gq_t, gk, rope_tabs, *, layer, tm, tiles_per_cond, tiles_per_seq, emit_cache,
             prev_cache=None):
    t = x.shape[0]
    use_rope = rope_tabs is not None
    assert (emit_cache == "final") == (prev_cache is not None) and DEPTH == 2
    in_specs = [
        pl.BlockSpec((tm, D_MODEL), lambda i: (i, 0)),
        pl.BlockSpec((1, 6, D_MODEL), lambda i: (i // tiles_per_cond, 0, 0)),
        _const_spec((1, D_MODEL)),
        _layer_spec((T_ROWS, D_MODEL), layer),
        _layer_spec((D_MODEL, K_COLS), layer),
        _const_spec((LANES, tm)),
        _const_spec((1, LANES)),
    ]
    args = [x, mod_l, gn, w_t, w_k, gq_t, gk]
    if use_rope:
        in_specs += [pl.BlockSpec((tm, LANES), lambda i: (i % tiles_per_seq, 0))] * 4
        in_specs += [pl.BlockSpec((LANES, tm), lambda i: (0, i % tiles_per_seq))] * 4
        args += list(rope_tabs)
    if prev_cache is not None:
        in_specs.append(pl.BlockSpec((tm, D_IN - Z_KA), lambda i: (i, 0)))
        args.append(prev_cache)
    out_shape = [jax.ShapeDtypeStruct((t, K_COLS), BF16)]
    out_specs = [pl.BlockSpec((tm, K_COLS), lambda i: (i, 0))]
    if emit_cache == "rows":
        out_shape.append(jax.ShapeDtypeStruct((t, D_IN - Z_KA), F32))
        out_specs.append(pl.BlockSpec((tm, D_IN - Z_KA), lambda i: (i, 0)))
    if emit_cache == "final":
        for _, width in CACHE_PIECES:
            out_shape.append(jax.ShapeDtypeStruct((t // tm, DEPTH, tm, width), F32))
            out_specs.append(pl.BlockSpec((1, DEPTH, tm, width), lambda i: (i, 0, 0, 0)))
    out_shape.append(jax.ShapeDtypeStruct((T_ROWS, t), BF16))
    out_specs.append(pl.BlockSpec((T_ROWS, tm), lambda i: (0, i)))
    return pl.pallas_call(
        functools.partial(_in_proj_kernel, use_rope=use_rope, emit_cache=emit_cache),
        grid=(t // tm,),
        in_specs=in_specs,
        out_specs=out_specs,
        out_shape=out_shape,
        compiler_params=_cparams(("arbitrary",)),
        name="in_proj",
    )(*args)


def _gqa_queries_t(qt_ref, lo):
    slabs = [qt_ref[LANES * s:LANES * (s + 1), :] for s in range(3)]
    zero = jnp.zeros_like(slabs[0])
    return jnp.concatenate([jnp.where(lo, s, zero) for s in slabs] + [jnp.where(lo, zero, s) for s in slabs], axis=1)


def _gqa_store_t(ot, o_ref, tq, lo, col0=0):
    for s in range(3):
        slab_t = jnp.where(lo, ot[:, s * tq:(s + 1) * tq], ot[:, (3 + s) * tq:(4 + s) * tq])
        o_ref[:, col0 + LANES * s:col0 + LANES * (s + 1)] = slab_t.T.astype(BF16)


def _diff_queries_t(qt, row):
    zero = jnp.zeros_like(qt)
    return jnp.concatenate([jnp.where((row >= C_QK_DIM * j) & (row < C_QK_DIM * (j + 1)), qt, zero)
                            for j in range(4)], axis=1)


def _diff_output_t(ot, tq, lam, gsub, lam_init, row):
    o_even = ot[:, 0:tq] - lam * ot[:, tq:2 * tq]
    o_odd = ot[:, 2 * tq:3 * tq] - lam * ot[:, 3 * tq:4 * tq]
    oc = jnp.where(row < C_V_DIM, o_even, o_odd).T
    lo = lax.broadcasted_iota(jnp.int32, (tq, LANES), 1) < C_V_DIM
    return (_head_rms(oc, gsub, lo) * (1.0 - lam_init)).astype(BF16)


def _attn_ctx_kernel(sink_ref, lam_ref, gsub_ref, z_ref, zt_ref, o_ref, *, lam_init):
    tq = z_ref.shape[0]
    row = lax.broadcasted_iota(jnp.int32, (LANES, tq), 0)
    lo = row < HEAD_DIM

    def attend(q, k, vt, sink=None):
        s = jnp.dot(k, q, preferred_element_type=F32)
        m = jnp.max(s, axis=0, keepdims=True)
        if sink is not None:
            m = jnp.maximum(m, sink)
        p = jnp.exp2(s - m)
        l = jnp.sum(p, axis=0, keepdims=True)
        if sink is not None:
            l = l + jnp.exp2(sink - m)
        return jnp.dot(vt, p.astype(BF16), preferred_element_type=F32) * (1.0 / l)

    def keys(off):
        return z_ref[:, off:off + LANES]

    def feat(off):
        return zt_ref[off:off + LANES, :]

    _gqa_store_t(attend(_gqa_queries_t(zt_ref.at[T_QA:T_QA + 384], lo), keys(K_A), feat(T_VA)), o_ref, tq, lo)
    sink = jnp.concatenate([jnp.full((1, tq), sink_ref[h] * LOG2E, F32) for h in range(B_HEADS)], axis=1)
    _gqa_store_t(attend(_gqa_queries_t(zt_ref.at[T_QB:T_QB + 384], lo), keys(K_B), feat(T_VB), sink),
                 o_ref, tq, lo, col0=384)
    lam = _diff_lambda(lam_ref, lam_init)
    for s in range(2):
        ot = attend(_diff_queries_t(feat(T_QC + LANES * s), row), keys(K_C + LANES * s), feat(T_VC + LANES * s))
        o_ref[:, 768 + LANES * s:768 + LANES * (s + 1)] = _diff_output_t(ot, tq, lam, gsub_ref[...], lam_init, row)


def _attn_ctx(z, zt, sink, lam_par, gsub, *, seq, lam_init):
    t = z.shape[0]
    return pl.pallas_call(
        functools.partial(_attn_ctx_kernel, lam_init=lam_init),
        grid=(t // seq,),
        in_specs=[
            pl.BlockSpec(memory_space=pltpu.SMEM),
            _const_spec((8, LANES)),
            _const_spec((1, LANES)),
            pl.BlockSpec((seq, K_COLS), lambda b: (b, 0)),
            pl.BlockSpec((T_ROWS, seq), lambda b: (0, b)),
        ],
        out_specs=pl.BlockSpec((seq, D_MODEL), lambda b: (b, 0)),
        out_shape=jax.ShapeDtypeStruct((t, D_MODEL), BF16),
        compiler_params=_cparams(("arbitrary",)),
        name="attn_ctx",
    )(sink, lam_par, gsub, z, zt)


def _attn_a_kernel(qt_ref, kc_ref, vct_ref, k_ref, vt_ref, o_ref, q_scr, m_ref, acc_ref,
                   s0, s1, x0, x1, p0, p1, a0, a1, *, tk):
    tq = qt_ref.shape[1]
    lo = lax.broadcasted_iota(jnp.int32, (LANES, tq), 0) < HEAD_DIM
    refs = (m_ref, acc_ref)
    q_scr[...] = _gqa_queries_t(qt_ref, lo)
    _softmax_init_t(*refs)
    _attend_pipelined_t(q_scr, kc_ref, vct_ref, k_ref, vt_ref, tk, (s0, s1), (x0, x1), (p0, p1), (a0, a1), *refs)
    _gqa_store_t(_softmax_result_t(acc_ref), o_ref, tq, lo)


def _keys_major_scratch(n, tk):
    return [pltpu.VMEM((LANES, n), BF16), pltpu.VMEM((1, n), F32),
            pltpu.VMEM((LANES + BF16_ROWS, n), F32),
            pltpu.VMEM((tk, n), F32), pltpu.VMEM((tk, n), F32),
            pltpu.VMEM((1, n), F32), pltpu.VMEM((1, n), F32),
            pltpu.VMEM((tk, n), BF16), pltpu.VMEM((tk, n), BF16),
            pltpu.VMEM((1, n), F32), pltpu.VMEM((1, n), F32)]


def _attn_a(z, zt, k_ctx, vt_ctx, *, seq, tq, tk):
    t = z.shape[0]
    nq = seq // tq
    n_ctx = k_ctx.shape[1]
    n = A_HEADS * tq
    return pl.pallas_call(
        functools.partial(_attn_a_kernel, tk=tk),
        grid=(t // seq, nq),
        in_specs=[
            pl.BlockSpec((384, tq), lambda b, i: (T_QA // 384, b * nq + i)),
            pl.BlockSpec((1, n_ctx, LANES), lambda b, i: (b, 0, 0)),
            pl.BlockSpec((1, LANES, n_ctx), lambda b, i: (b, 0, 0)),
            pl.BlockSpec((seq, LANES), lambda b, i: (b, K_A // LANES)),
            pl.BlockSpec((LANES, seq), lambda b, i: (T_VA // LANES, b)),
        ],
        out_specs=pl.BlockSpec((tq, 384), lambda b, i: (b * nq + i, 0)),
        out_shape=jax.ShapeDtypeStruct((t, 384), BF16),
        scratch_shapes=_keys_major_scratch(n, tk),
        compiler_params=_cparams(("arbitrary", "arbitrary")),
        name="attn_a",
    )(zt, k_ctx, vt_ctx, z, zt)


def _attn_b_kernel(sink_ref, qt_ref, kc_ref, vct_ref, k_ref, vt_ref, o_ref):
    tq = qt_ref.shape[1]
    seq = k_ref.shape[0]
    n = B_HEADS * tq
    band = tq + 2 * WINDOW
    i = pl.program_id(1)
    lo = lax.broadcasted_iota(jnp.int32, (LANES, tq), 0) < HEAD_DIM
    q = _gqa_queries_t(qt_ref, lo)
    start = pl.multiple_of(jnp.clip(i * tq - WINDOW, 0, seq - band), LANES)
    s_c = jnp.dot(kc_ref[0], q, preferred_element_type=F32)
    s_b = jnp.dot(k_ref[pl.ds(start, band), :], q, preferred_element_type=F32)
    qpos = i * tq + lax.broadcasted_iota(jnp.int32, (band, tq), 1)
    kpos = start + lax.broadcasted_iota(jnp.int32, (band, tq), 0)
    inside = jnp.abs(kpos - qpos) <= WINDOW
    s_b = jnp.concatenate([jnp.where(inside, s_b[:, h * tq:(h + 1) * tq], NEG) for h in range(B_HEADS)], axis=1)
    sink = jnp.concatenate([jnp.full((1, tq), sink_ref[h] * LOG2E, F32) for h in range(B_HEADS)], axis=1)
    m = jnp.maximum(sink, jnp.maximum(jnp.max(s_c, axis=0, keepdims=True), jnp.max(s_b, axis=0, keepdims=True)))
    p_c = jnp.exp2(s_c - m)
    p_b = jnp.exp2(s_b - m)
    l = jnp.exp2(sink - m) + jnp.sum(p_c, axis=0, keepdims=True) + jnp.sum(p_b, axis=0, keepdims=True)
    acc = (jnp.dot(vct_ref[0], p_c.astype(BF16), preferred_element_type=F32)
           + jnp.dot(vt_ref[:, pl.ds(start, band)], p_b.astype(BF16), preferred_element_type=F32))
    _gqa_store_t(acc * (1.0 / l), o_ref, tq, lo)


def _attn_b(z, zt, k_ctx, vt_ctx, sink, *, seq, tq):
    t = z.shape[0]
    nq = seq // tq
    n_ctx = k_ctx.shape[1]
    return pl.pallas_call(
        _attn_b_kernel,
        grid=(t // seq, nq),
        in_specs=[
            pl.BlockSpec(memory_space=pltpu.SMEM),
            pl.BlockSpec((384, tq), lambda b, i: (T_QB // 384, b * nq + i)),
            pl.BlockSpec((1, n_ctx, LANES), lambda b, i: (b, 0, 0)),
            pl.BlockSpec((1, LANES, n_ctx), lambda b, i: (b, 0, 0)),
            pl.BlockSpec((seq, LANES), lambda b, i: (b, K_B // LANES)),
            pl.BlockSpec((LANES, seq), lambda b, i: (T_VB // LANES, b)),
        ],
        out_specs=pl.BlockSpec((tq, 384), lambda b, i: (b * nq + i, 0)),
        out_shape=jax.ShapeDtypeStruct((t, 384), BF16),
        compiler_params=_cparams(("arbitrary", "arbitrary")),
        name="attn_b",
    )(sink, zt, k_ctx, vt_ctx, z, zt)


def _attn_c_kernel(lam_ref, gsub_ref, qt_ref, kc_ref, vct_ref, k_ref, vt_ref, o_ref,
                   q_scr, m_ref, acc_ref, s0, s1, x0, x1, p0, p1, a0, a1, *, tk, lam_init):
    tq = qt_ref.shape[1]
    row = lax.broadcasted_iota(jnp.int32, (LANES, tq), 0)
    refs = (m_ref, acc_ref)
    q_scr[...] = _diff_queries_t(qt_ref[...], row)
    _softmax_init_t(*refs)
    _attend_pipelined_t(q_scr, kc_ref, vct_ref, k_ref, vt_ref, tk, (s0, s1), (x0, x1), (p0, p1), (a0, a1), *refs)
    lam = _diff_lambda(lam_ref, lam_init)
    o_ref[...] = _diff_output_t(_softmax_result_t(acc_ref), tq, lam, gsub_ref[...], lam_init, row)


def _attn_c(z, zt, k_ctx, vt_ctx, lam_par, gsub, *, seq, tq, tk, lam_init):
    t = z.shape[0]
    nq = seq // tq
    n_ctx = k_ctx.shape[1]
    n = 4 * tq
    return pl.pallas_call(
        functools.partial(_attn_c_kernel, tk=tk, lam_init=lam_init),
        grid=(t // seq, 2, nq),
        in_specs=[
            _const_spec((8, LANES)),
            _const_spec((1, LANES)),
            pl.BlockSpec((LANES, tq), lambda b, s, i: (T_QC // LANES + s, b * nq + i)),
            pl.BlockSpec((1, n_ctx, LANES), lambda b, s, i: (b, 0, s)),
            pl.BlockSpec((1, LANES, n_ctx), lambda b, s, i: (b, s, 0)),
            pl.BlockSpec((seq, LANES), lambda b, s, i: (b, K_C // LANES + s)),
            pl.BlockSpec((LANES, seq), lambda b, s, i: (T_VC // LANES + s, b)),
        ],
        out_specs=pl.BlockSpec((tq, LANES), lambda b, s, i: (b * nq + i, s)),
        out_shape=jax.ShapeDtypeStruct((t, C_HEADS * C_V_DIM), BF16),
        scratch_shapes=_keys_major_scratch(n, tk),
        compiler_params=_cparams(("arbitrary", "arbitrary", "arbitrary")),
        name="attn_c",
    )(lam_par, gsub, zt, k_ctx, vt_ctx, z, zt)


def _post_kernel(*refs, n_parts, halo, tiles_per_seq, final):
    it = iter(refs)
    tiles = [next(it) for _ in range(1 + n_parts)]
    halos = [(next(it), next(it)) for _ in range(1 + n_parts)] if halo else None
    (mod_ref, gn_ref, wo_ref, wu_ref, cw_ref, cb_ref, wd_ref, gf_ref, out_ref, act_scr) = (next(it) for _ in range(10))

    def rows_of(j):
        if halo:
            return jnp.concatenate([halos[j][0][...], tiles[j][...], halos[j][1][...]], axis=0)
        return tiles[j][...]

    tm = tiles[0].shape[0]
    x = rows_of(0)
    o = jnp.concatenate([rows_of(j) for j in range(1, 1 + n_parts)], axis=1)
    ext = x.shape[0]
    mod = mod_ref[0]
    x1 = x + mod[2:3] * jnp.dot(o, wo_ref[0], preferred_element_type=F32)
    h = _rms(x1, gn_ref[...]) * (1.0 + mod[4:5]) + mod[3:4]
    row = lax.broadcasted_iota(jnp.int32, (ext, 1), 0)
    if halo:
        t_in_seq = pl.program_id(0) % tiles_per_seq
        keep = ((row >= halo) | (t_in_seq > 0)) & ((row < halo + tm) | (t_in_seq < tiles_per_seq - 1))
        h = jnp.where(keep, h, 0.0)
    h = h.astype(BF16)

    def conv(u, c0):
        cw = cw_ref[:, c0:c0 + FFN_CHUNK]
        up = pltpu.roll(u, 1, 0)
        dn = pltpu.roll(u, ext - 1, 0)
        if not halo:
            up = jnp.where(row == 0, 0.0, up)
            dn = jnp.where(row == ext - 1, 0.0, dn)
        v = cw[0:1] * up + cw[1:2] * u + cw[2:3] * dn + cb_ref[:, c0:c0 + FFN_CHUNK]
        return v[halo:halo + tm]

    def up_proj(c):
        ca, cg = c * FFN_CHUNK, D_FF + c * FFN_CHUNK
        return (jnp.dot(h, wu_ref[0, :, ca:ca + FFN_CHUNK], preferred_element_type=F32),
                jnp.dot(h, wu_ref[0, :, cg:cg + FFN_CHUNK], preferred_element_type=F32))

    n_chunks = D_FF // FFN_CHUNK
    nxt = up_proj(0)
    for c in range(n_chunks):
        ua, ug = nxt
        if c + 1 < n_chunks:
            nxt = up_proj(c + 1)
        ca = c * FFN_CHUNK
        a = conv(ua, ca)
        g = conv(ug, D_FF + ca)
        act_scr[:, ca:ca + FFN_CHUNK] = (a / (1.0 + jnp.exp(-a)) * g).astype(BF16)
    x2 = x1[halo:halo + tm] + mod[5:6] * jnp.dot(act_scr[...], wd_ref[0], preferred_element_type=F32)
    if final:
        x2 = _rms(x2, gf_ref[...])
    out_ref[...] = x2


def _post(x, o_parts, mod_l, gn2, wo, wu, cw, cb, wd, gf, *, layer, tm, tiles_per_cond, tiles_per_seq):
    t = x.shape[0]
    final = layer == DEPTH - 1
    assert sum(o.shape[1] for o in o_parts) == D_MODEL
    halo = HALO if tiles_per_seq > 1 else 0
    tile = lambda i: (i, 0)
    rows = [x] + list(o_parts)
    in_specs = [pl.BlockSpec((tm, a.shape[1]), tile) for a in rows]
    args = list(rows)
    if halo:
        per = tm // halo
        prev = lambda i: (jnp.maximum(i * per - 1, 0), 0)
        nxt = lambda i: (jnp.minimum((i + 1) * per, t // halo - 1), 0)
        for a in rows:
            in_specs += [pl.BlockSpec((halo, a.shape[1]), prev), pl.BlockSpec((halo, a.shape[1]), nxt)]
            args += [a, a]
    in_specs += [
        pl.BlockSpec((1, 6, D_MODEL), lambda i: (i // tiles_per_cond, 0, 0)),
        _const_spec((1, D_MODEL)),
        _layer_spec((D_MODEL, D_MODEL), layer),
        _layer_spec((D_MODEL, 2 * D_FF), layer),
        _const_spec((3, 2 * D_FF)),
        _const_spec((1, 2 * D_FF)),
        _layer_spec((D_FF, D_MODEL), layer),
        _const_spec((1, D_MODEL)),
    ]
    args += [mod_l, gn2, wo, wu, cw, cb, wd, gf]
    return pl.pallas_call(
        functools.partial(_post_kernel, n_parts=len(o_parts), halo=halo, tiles_per_seq=tiles_per_seq, final=final),
        grid=(t // tm,),
        in_specs=in_specs,
        out_specs=pl.BlockSpec((tm, D_MODEL), tile),
        out_shape=jax.ShapeDtypeStruct((t, D_MODEL), F32),
        scratch_shapes=[pltpu.VMEM((tm, D_FF), BF16)],
        compiler_params=_cparams(("arbitrary",)),
        name="post",
    )(*args)


def _rope_tables(seq):
    t = jnp.arange(seq)
    rows = (t // GRID_W).astype(F32)[:, None]
    cols = (t % GRID_W).astype(F32)[:, None]

    def tab(half, reps):
        inv = ROPE_THETA ** (-jnp.arange(half, dtype=F32) / half)
        ar, ac = rows * inv[None, :], cols * inv[None, :]
        cos = jnp.concatenate([jnp.cos(ar), jnp.cos(ar), jnp.cos(ac), jnp.cos(ac)], axis=-1)
        sin = jnp.concatenate([-jnp.sin(ar), jnp.sin(ar), -jnp.sin(ac), jnp.sin(ac)], axis=-1)
        return jnp.tile(cos, (1, reps)), jnp.tile(sin, (1, reps))

    tabs = tab(HEAD_DIM // 4, LANES // HEAD_DIM) + tab(C_QK_DIM // 4, LANES // C_QK_DIM)
    return tabs + tuple(t.T for t in tabs)


def _w_in_perms():
    sizes = (A_HEADS * HEAD_DIM, A_KV * HEAD_DIM, A_KV * HEAD_DIM, B_HEADS * HEAD_DIM, B_KV * HEAD_DIM,
             B_KV * HEAD_DIM, C_HEADS * 2 * C_QK_DIM, C_HEADS * 2 * C_QK_DIM, C_HEADS * C_V_DIM)
    offs = np.concatenate([[0], np.cumsum(sizes)])
    qa, ka, va, qb, kb, vb, qc, kc, vc = (np.arange(offs[j], offs[j + 1]) for j in range(9))
    pair = np.concatenate([np.arange(h * HEAD_DIM, (h + 1) * HEAD_DIM) for h in PAIRED_HEADS])
    return np.concatenate([qa[pair], va, qc, vc, vb, qb[pair]]), np.concatenate([ka, kb, kc])


def _w_out_perm():
    pair = np.concatenate([np.arange(h * HEAD_DIM, (h + 1) * HEAD_DIM) for h in PAIRED_HEADS])
    return np.concatenate([pair, A_HEADS * HEAD_DIM + pair, np.arange(768, D_MODEL)])


def _take_runs(w, perm, axis):
    cuts = [0] + [j for j in range(1, len(perm)) if perm[j] != perm[j - 1] + 1] + [len(perm)]
    parts = [lax.slice_in_dim(w, int(perm[a]), int(perm[b - 1]) + 1, axis=axis) for a, b in zip(cuts[:-1], cuts[1:])]
    return jnp.concatenate(parts, axis=axis)


def kernel(x_prompt, x_sample, cache_a_k, cache_a_v, cache_b_k, cache_b_v, cache_c_k, cache_c_v, c, c_ctx, w_ada, b_ada, g_norm1, g_norm2, w_in, g_qa, g_ka, sink_b, lam_q1, lam_k1, lam_q2, lam_k2, g_subln, w_out, w_up, conv_w, conv_b, w_down, g_final):
    n_ctx_req, ctx_len, _ = x_prompt.shape
    n_lat_req, lat_len, _ = x_sample.shape
    past = cache_a_k.shape[2]

    conds = jnp.zeros((8, D_MODEL), F32).at[0].set(c_ctx).at[1:1 + n_lat_req].set(c)
    mod = _modulation(conds, w_ada, b_ada).reshape(DEPTH, 8, 6, D_MODEL)

    perm_t, perm_k = _w_in_perms()
    w_t = jnp.swapaxes(_take_runs(w_in, perm_t, 2), 1, 2).astype(BF16)
    w_k = _take_runs(w_in, perm_k, 2).astype(BF16)
    w_out_p = _take_runs(w_out, _w_out_perm(), 1).astype(BF16)
    w_up_b = w_up.astype(BF16)
    w_down_b = w_down.astype(BF16)
    rope_tabs = _rope_tables(lat_len)
    zpad = jnp.zeros((DEPTH, LANES - C_QK_DIM), F32)
    lam_rows = [jnp.concatenate([v, zpad], axis=-1) for v in (lam_q1, lam_k1, lam_q2, lam_k2)]
    lam_par = jnp.concatenate([jnp.stack(lam_rows, axis=1), jnp.zeros((DEPTH, 4, LANES), F32)], axis=1)
    gf = g_final.reshape(1, D_MODEL)

    xc = x_prompt.reshape(n_ctx_req * ctx_len, D_MODEL)
    xs = x_sample.reshape(n_lat_req * lat_len, D_MODEL)
    lat_tm = 512
    caches = []
    for l in range(DEPTH):
        lam_init = 0.8 - 0.6 * math.exp(-0.3 * l)
        gn1 = g_norm1[l].reshape(1, D_MODEL)
        gn2 = g_norm2[l].reshape(1, D_MODEL)
        gq_col = jnp.tile(g_qa[l], LANES // HEAD_DIM)[:, None]
        gk = jnp.tile(g_ka[l], LANES // HEAD_DIM).reshape(1, LANES)
        gsub = jnp.tile(g_subln[l], LANES // C_V_DIM).reshape(1, LANES)
        post_w = (gn2, w_out_p, w_up_b, conv_w[l], conv_b[l].reshape(1, 2 * D_FF), w_down_b, gf)
        final = l == DEPTH - 1

        n_tiles = n_ctx_req
        zc, *caches, ztc = _in_proj(xc, mod[l, 0:1], gn1, w_t, w_k, jnp.broadcast_to(gq_col, (LANES, ctx_len)), gk,
                                    None, layer=l, tm=ctx_len, tiles_per_cond=n_tiles, tiles_per_seq=1,
                                    emit_cache="final" if final else "rows",
                                    prev_cache=caches[0] if final else None)
        oc = _attn_ctx(zc, ztc, sink_b[l], lam_par[l], gsub, seq=ctx_len, lam_init=lam_init)
        xc = _post(xc, [oc], mod[l, 0:1], *post_w, layer=l, tm=ctx_len, tiles_per_cond=n_tiles, tiles_per_seq=1)

        per_seq = lat_len // lat_tm
        zs, zts = _in_proj(xs, mod[l, 1:1 + n_lat_req], gn1, w_t, w_k, jnp.broadcast_to(gq_col, (LANES, lat_tm)), gk,
                           rope_tabs, layer=l, tm=lat_tm, tiles_per_cond=per_seq, tiles_per_seq=per_seq,
                           emit_cache=None)
        flat = lambda a: a[:, l].reshape(n_lat_req, past, -1).astype(BF16)
        flat_t = lambda a: jnp.swapaxes(flat(a), 1, 2)
        oa = _attn_a(zs, zts, flat(cache_a_k), flat_t(cache_a_v), seq=lat_len, tq=128, tk=512)
        ob = _attn_b(zs, zts, flat(cache_b_k), flat_t(cache_b_v), sink_b[l], seq=lat_len, tq=128)
        oc = _attn_c(zs, zts, flat(cache_c_k), flat_t(cache_c_v), lam_par[l], gsub, seq=lat_len, tq=256, tk=512,
                     lam_init=lam_init)
        xs = _post(xs, [oa, ob, oc], mod[l, 1:1 + n_lat_req], *post_w, layer=l, tm=lat_tm, tiles_per_cond=per_seq,
                   tiles_per_seq=per_seq)

    heads = (A_KV, A_KV, B_KV, B_KV, C_HEADS, C_HEADS)
    new_caches = tuple(cch.reshape(n_ctx_req, DEPTH, ctx_len, h, -1) for cch, h in zip(caches, heads))
    return (xc.reshape(x_prompt.shape), xs.reshape(x_sample.shape)) + new_caches
```

```python
import functools
import math

import numpy as np
import jax
import jax.numpy as jnp
from jax import lax
from jax.experimental import pallas as pl
from jax.experimental.pallas import tpu as pltpu

D_MODEL = 1024
DEPTH = 2
GRID_W = 64
HEAD_DIM = 64
A_HEADS = 6
A_KV = 2
B_HEADS = 6
B_KV = 2
C_HEADS = 4
C_QK_DIM = 32
C_V_DIM = 2 * C_QK_DIM
WINDOW = 128
ROPE_THETA = 10000.0
D_FF = 2816
EPS = 1e-6
NEG = -1e30
LOG2E = math.log2(math.e)

LANES = 128
BF16_ROWS = 16
MXU_COLS = 256
VMEM_LIMIT = 56 * 1024 * 1024

Z_QA, Z_QB, Z_QC = 0, 384, 768
Z_KA, Z_VA, Z_KB, Z_VB, Z_KC, Z_VC = 1024, 1152, 1280, 1408, 1536, 1792
D_IN = 2048
CACHE_PIECES = ((Z_KA, 128), (Z_VA, 128), (Z_KB, 128), (Z_VB, 128), (Z_KC, 256), (Z_VC, 256))
K_A, K_B, K_C, K_COLS = 0, 128, 256, 512
T_QA, T_VA, T_QC, T_VC, T_VB, T_QB, T_ROWS = 0, 384, 512, 768, 1024, 1152, 1536
PAIRED_HEADS = (0, 3, 1, 4, 2, 5)
FFN_CHUNK = 256
HALO = BF16_ROWS

F32 = jnp.float32
BF16 = jnp.bfloat16


def _cparams(sem):
    return pltpu.CompilerParams(dimension_semantics=sem, vmem_limit_bytes=VMEM_LIMIT)


def _const_spec(shape):
    nd = len(shape)
    return pl.BlockSpec(shape, lambda *_: (0,) * nd)


def _layer_spec(shape, layer):
    nd = len(shape)
    return pl.BlockSpec((1,) + tuple(shape), lambda *_: (layer,) + (0,) * nd)


def _mod_kernel(c_ref, w_ref, b_ref, o_ref):
    cond = c_ref[...]
    a = cond / (1.0 + jnp.exp(-cond))
    o_ref[0] = jnp.dot(a.astype(BF16), w_ref[0].astype(BF16), preferred_element_type=F32) + b_ref[0]


def _modulation(conds, w_ada, b_ada):
    nb = 1536
    n_out = w_ada.shape[-1]
    return pl.pallas_call(
        _mod_kernel,
        grid=(DEPTH, n_out // nb),
        in_specs=[
            pl.BlockSpec((8, D_MODEL), lambda l, j: (0, 0)),
            pl.BlockSpec((1, D_MODEL, nb), lambda l, j: (l, 0, j)),
            pl.BlockSpec((1, 1, nb), lambda l, j: (l, 0, j)),
        ],
        out_specs=pl.BlockSpec((1, 8, nb), lambda l, j: (l, 0, j)),
        out_shape=jax.ShapeDtypeStruct((DEPTH, 8, n_out), F32),
        compiler_params=_cparams(("arbitrary", "arbitrary")),
        name="modulation",
    )(conds, w_ada, b_ada.reshape(DEPTH, 1, n_out))


def _rms(x, g):
    ms = jnp.mean(x * x, axis=-1, keepdims=True)
    return x * lax.rsqrt(ms + EPS) * g


def _head_rms(x, g, lo):
    ss = x * x
    s_lo = jnp.sum(jnp.where(lo, ss, 0.0), axis=-1, keepdims=True)
    s_hi = jnp.sum(jnp.where(lo, 0.0, ss), axis=-1, keepdims=True)
    inv = jnp.where(lo, lax.rsqrt(s_lo * (1.0 / HEAD_DIM) + EPS), lax.rsqrt(s_hi * (1.0 / HEAD_DIM) + EPS))
    return x * inv * g


def _rope(x, cos, sin, chunk, first):
    sw = jnp.where(first, pltpu.roll(x, LANES - chunk, 1), pltpu.roll(x, chunk, 1))
    return x * cos + sw * sin


def _softmax_init_t(m_ref, acc_ref):
    m_ref[...] = jnp.full(m_ref.shape, NEG, F32)
    acc_ref[...] = jnp.zeros(acc_ref.shape, F32)


def _scores_t(k, q_scr, s_ref, mx_ref, cs):
    s = jnp.dot(k, q_scr[:, cs], preferred_element_type=F32)
    s_ref[0:k.shape[0], cs] = s
    mx_ref[:, cs] = jnp.max(s, axis=0, keepdims=True)


def _probs_t(rows, s_ref, mx_ref, p_ref, a_ref, m_ref, cs):
    s = s_ref[0:rows, cs]
    m_prev = m_ref[:, cs]
    m_new = jnp.maximum(m_prev, mx_ref[:, cs])
    a_ref[:, cs] = jnp.exp2(m_prev - m_new)
    m_ref[:, cs] = m_new
    p_ref[0:rows, cs] = jnp.exp2(s - m_new).astype(BF16)


def _values_t(vt, p_ref, a_ref, acc_ref, cs):
    ones = jnp.ones((BF16_ROWS, vt.shape[1]), BF16)
    pv = jnp.dot(jnp.concatenate([vt, ones], axis=0), p_ref[0:vt.shape[1], cs], preferred_element_type=F32)
    acc_ref[:, cs] = a_ref[:, cs] * acc_ref[:, cs] + pv


def _softmax_result_t(acc_ref):
    return acc_ref[0:LANES] * (1.0 / acc_ref[LANES:LANES + 1])


def _attend_pipelined_t(q_scr, kc_ref, vct_ref, k_ref, vt_ref, tk, s_bufs, x_bufs, p_bufs, a_bufs, m_ref, acc_ref):
    n_lat = k_ref.shape[0] // tk
    n_ctx = kc_ref.shape[1]
    assert n_ctx <= tk and n_lat % 2 == 0 and n_lat >= 4

    def k_lat(j):
        return k_ref[pl.ds(pl.multiple_of(j * tk, tk), tk), :]

    def v_lat(j):
        return vt_ref[:, pl.ds(pl.multiple_of(j * tk, tk), tk)]

    n = q_scr.shape[1]
    groups = [slice(c, c + MXU_COLS) for c in range(0, n, MXU_COLS)]

    def step(par, k=None, vt=None, probs=tk):
        for cs in groups:
            if probs:
                _probs_t(probs, s_bufs[par], x_bufs[par], p_bufs[par], a_bufs[par], m_ref, cs)
            if vt is not None:
                _values_t(vt, p_bufs[1 - par], a_bufs[1 - par], acc_ref, cs)
            if k is not None:
                _scores_t(k, q_scr, s_bufs[1 - par], x_bufs[1 - par], cs)

    step(1, k=kc_ref[0], probs=0)
    step(0, k=k_lat(0), probs=n_ctx)
    step(1, k=k_lat(1), vt=vct_ref[0])

    def pair(jj, carry):
        j = 2 * jj
        step(0, k=k_lat(j + 2), vt=v_lat(j))
        step(1, k=k_lat(j + 3), vt=v_lat(j + 1))
        return carry

    for jj in range(n_lat // 2 - 1):
        pair(jj, 0)
    step(0, vt=v_lat(n_lat - 2))
    step(1, vt=v_lat(n_lat - 1), probs=0)


def _diff_lambda(lam_ref, lam_init):
    f = lambda a, b: jnp.exp(jnp.sum(a * b, axis=-1, keepdims=True))
    return f(lam_ref[0:1], lam_ref[1:2]) - f(lam_ref[2:3], lam_ref[3:4]) + lam_init


def _swap_row_chunks(x, chunk):
    parts = []
    for r in range(0, x.shape[0], 2 * chunk):
        parts += [x[r + chunk:r + 2 * chunk], x[r:r + chunk]]
    return jnp.concatenate(parts, axis=0)


def _in_proj_kernel(*refs, use_rope, emit_cache):
    it = iter(refs)
    x_ref, mod_ref, gn_ref, wt_ref, wk_ref, gq_ref, gk_ref = (next(it) for _ in range(7))
    cos64 = sin64 = cos32 = sin32 = cos64t = sin64t = cos32t = sin32t = None
    if use_rope:
        cos64, sin64, cos32, sin32, cos64t, sin64t, cos32t, sin32t = (next(it)[...] for _ in range(8))
    prev_ref = next(it) if emit_cache == "final" else None
    zk_ref = next(it)
    cache_ref = next(it) if emit_cache == "rows" else None
    final_refs = [next(it) for _ in CACHE_PIECES] if emit_cache == "final" else None
    zt_ref = next(it)
    if final_refs:
        for ref, (start, width) in zip(final_refs, CACHE_PIECES):
            ref[0, 0] = prev_ref[:, start - Z_KA:start - Z_KA + width]

    tm = x_ref.shape[0]
    sub = min(tm, MXU_COLS)
    mod = mod_ref[0]
    lane = lax.broadcasted_iota(jnp.int32, (sub, LANES), 1)
    lo = lane < HEAD_DIM
    first16 = (lane & 31) < 16
    first8 = (lane & 15) < 8
    q_scale = HEAD_DIM ** -0.5 * LOG2E
    qc_scale = C_QK_DIM ** -0.5 * LOG2E

    def matmuls(r0):
        h = (_rms(x_ref[r0:r0 + sub], gn_ref[...]) * (1.0 + mod[1:2]) + mod[0:1]).astype(BF16)
        zk = jnp.dot(h, wk_ref[0], preferred_element_type=F32)
        zt = lax.dot_general(wt_ref[0], h, (((1,), (1,)), ((), ())), preferred_element_type=F32)
        return zk, zt

    def put_cache(r0, off, v):
        if cache_ref is not None:
            cache_ref[r0:r0 + sub, off - Z_KA:off - Z_KA + LANES] = v
        if final_refs:
            for ref, (start, width) in zip(final_refs, CACHE_PIECES):
                if start <= off < start + width:
                    ref[0, DEPTH - 1, r0:r0 + sub, off - start:off - start + LANES] = v

    def finish_keys(r0, zk):
        def tab(t):
            return t[r0:r0 + sub] if use_rope else None

        def rope64(v):
            return _rope(v, tab(cos64), tab(sin64), 16, first16) if use_rope else v

        def rope32(v):
            return _rope(v, tab(cos32), tab(sin32), 8, first8) if use_rope else v

        def put_k(off, v):
            zk_ref[r0:r0 + sub, off:off + LANES] = v.astype(BF16)

        ka = _head_rms(zk[:, K_A:K_A + LANES], gk_ref[...], lo)
        put_cache(r0, Z_KA, ka)
        put_k(K_A, rope64(ka))
        kb = zk[:, K_B:K_B + LANES]
        put_cache(r0, Z_KB, kb)
        put_k(K_B, rope64(kb))
        for s in range(2):
            kc = zk[:, K_C + LANES * s:K_C + LANES * (s + 1)]
            put_cache(r0, Z_KC + LANES * s, kc)
            put_k(K_C + LANES * s, rope32(kc))

    def finish_features(r0, zt):
        def t_slab(off):
            return zt[off:off + LANES]

        def put_t(off, v):
            zt_ref[off:off + LANES, r0:r0 + sub] = v.astype(BF16)

        def rope_t(v, cos_t, sin_t, chunk):
            if not use_rope:
                return v
            return v * cos_t[:, r0:r0 + sub] + _swap_row_chunks(v, chunk) * sin_t[:, r0:r0 + sub]

        def head_rms_t(v):
            ss = v * v
            halves = []
            for r in (0, HEAD_DIM):
                inv = lax.rsqrt(jnp.sum(ss[r:r + HEAD_DIM], axis=0, keepdims=True) * (1.0 / HEAD_DIM) + EPS)
                halves.append(v[r:r + HEAD_DIM] * inv)
            return jnp.concatenate(halves, axis=0) * gq_ref[:, 0:sub]

        for s in range(3):
            put_t(T_QA + LANES * s, rope_t(head_rms_t(t_slab(T_QA + LANES * s)), cos64t, sin64t, 16) * q_scale)
            put_t(T_QB + LANES * s, rope_t(t_slab(T_QB + LANES * s), cos64t, sin64t, 16) * q_scale)
        for s in range(2):
            put_t(T_QC + LANES * s, rope_t(t_slab(T_QC + LANES * s), cos32t, sin32t, 8) * qc_scale)
        for t_off, z_off in ((T_VA, Z_VA), (T_VB, Z_VB), (T_VC, Z_VC), (T_VC + LANES, Z_VC + LANES)):
            put_t(t_off, t_slab(t_off))
            if emit_cache:
                put_cache(r0, z_off, t_slab(t_off).T)

    starts = list(range(0, tm, sub))
    prods = [matmuls(r0) for r0 in starts]
    for r0, (zk, _) in zip(starts, prods):
        finish_keys(r0, zk)
    for r0, (_, zt) in zip(starts, prods):
        finish_features(r0, zt)


def _in_proj(x, mod_l, gn, w_t, w_k, gq_t, gk, rope_tabs, *, layer, tm, tiles_per_cond, tiles_per_seq, emit_cache,
             prev_cache=None):
    t = x.shape[0]
    use_rope = rope_tabs is not None
    assert (emit_cache == "final") == (prev_cache is not None) and DEPTH == 2
    in_specs = [
        pl.BlockSpec((tm, D_MODEL), lambda i: (i, 0)),
        pl.BlockSpec((1, 6, D_MODEL), lambda i: (i // tiles_per_cond, 0, 0)),
        _const_spec((1, D_MODEL)),
        _layer_spec((T_ROWS, D_MODEL), layer),
        _layer_spec((D_MODEL, K_COLS), layer),
        _const_spec((LANES, tm)),
        _const_spec((1, LANES)),
    ]
    args = [x, mod_l, gn, w_t, w_k, gq_t, gk]
    if use_rope:
        in_specs += [pl.BlockSpec((tm, LANES), lambda i: (i % tiles_per_seq, 0))] * 4
        in_specs += [pl.BlockSpec((LANES, tm), lambda i: (0, i % tiles_per_seq))] * 4
        args += list(rope_tabs)
    if prev_cache is not None:
        in_specs.append(pl.BlockSpec((tm, D_IN - Z_KA), lambda i: (i, 0)))
        args.append(prev_cache)
    out_shape = [jax.ShapeDtypeStruct((t, K_COLS), BF16)]
    out_specs = [pl.BlockSpec((tm, K_COLS), lambda i: (i, 0))]
    if emit_cache == "rows":
        out_shape.append(jax.ShapeDtypeStruct((t, D_IN - Z_KA), F32))
        out_specs.append(pl.BlockSpec((tm, D_IN - Z_KA), lambda i: (i, 0)))
    if emit_cache == "final":
        for _, width in CACHE_PIECES:
            out_shape.append(jax.ShapeDtypeStruct((t // tm, DEPTH, tm, width), F32))
            out_specs.append(pl.BlockSpec((1, DEPTH, tm, width), lambda i: (i, 0, 0, 0)))
    out_shape.append(jax.ShapeDtypeStruct((T_ROWS, t), BF16))
    out_specs.append(pl.BlockSpec((T_ROWS, tm), lambda i: (0, i)))
    return pl.pallas_call(
        functools.partial(_in_proj_kernel, use_rope=use_rope, emit_cache=emit_cache),
        grid=(t // tm,),
        in_specs=in_specs,
        out_specs=out_specs,
        out_shape=out_shape,
        compiler_params=_cparams(("arbitrary",)),
        name="in_proj",
    )(*args)


def _gqa_queries_t(qt_ref, lo):
    slabs = [qt_ref[LANES * s:LANES * (s + 1), :] for s in range(3)]
    zero = jnp.zeros_like(slabs[0])
    return jnp.concatenate([jnp.where(lo, s, zero) for s in slabs] + [jnp.where(lo, zero, s) for s in slabs], axis=1)


def _gqa_store_t(ot, o_ref, tq, lo, col0=0):
    for s in range(3):
        slab_t = jnp.where(lo, ot[:, s * tq:(s + 1) * tq], ot[:, (3 + s) * tq:(4 + s) * tq])
        o_ref[:, col0 + LANES * s:col0 + LANES * (s + 1)] = slab_t.T.astype(BF16)


def _diff_queries_t(qt, row):
    zero = jnp.zeros_like(qt)
    return jnp.concatenate([jnp.where((row >= C_QK_DIM * j) & (row < C_QK_DIM * (j + 1)), qt, zero)
                            for j in range(4)], axis=1)


def _diff_output_t(ot, tq, lam, gsub, lam_init, row):
    o_even = ot[:, 0:tq] - lam * ot[:, tq:2 * tq]
    o_odd = ot[:, 2 * tq:3 * tq] - lam * ot[:, 3 * tq:4 * tq]
    oc = jnp.where(row < C_V_DIM, o_even, o_odd).T
    lo = lax.broadcasted_iota(jnp.int32, (tq, LANES), 1) < C_V_DIM
    return (_head_rms(oc, gsub, lo) * (1.0 - lam_init)).astype(BF16)


def _attn_ctx_kernel(sink_ref, lam_ref, gsub_ref, z_ref, zt_ref, o_ref, *, lam_init):
    tq = z_ref.shape[0]
    row = lax.broadcasted_iota(jnp.int32, (LANES, tq), 0)
    lo = row < HEAD_DIM

    def attend(q, k, vt, sink=None):
        s = jnp.dot(k, q, preferred_element_type=F32)
        m = jnp.max(s, axis=0, keepdims=True)
        if sink is not None:
            m = jnp.maximum(m, sink)
        p = jnp.exp2(s - m)
        l = jnp.sum(p, axis=0, keepdims=True)
        if sink is not None:
            l = l + jnp.exp2(sink - m)
        return jnp.dot(vt, p.astype(BF16), preferred_element_type=F32) * (1.0 / l)

    def keys(off):
        return z_ref[:, off:off + LANES]

    def feat(off):
        return zt_ref[off:off + LANES, :]

    _gqa_store_t(attend(_gqa_queries_t(zt_ref.at[T_QA:T_QA + 384], lo), keys(K_A), feat(T_VA)), o_ref, tq, lo)
    sink = jnp.concatenate([jnp.full((1, tq), sink_ref[h] * LOG2E, F32) for h in range(B_HEADS)], axis=1)
    _gqa_store_t(attend(_gqa_queries_t(zt_ref.at[T_QB:T_QB + 384], lo), keys(K_B), feat(T_VB), sink),
                 o_ref, tq, lo, col0=384)
    lam = _diff_lambda(lam_ref, lam_init)
    for s in range(2):
        ot = attend(_diff_queries_t(feat(T_QC + LANES * s), row), keys(K_C + LANES * s), feat(T_VC + LANES * s))
        o_ref[:, 768 + LANES * s:768 + LANES * (s + 1)] = _diff_output_t(ot, tq, lam, gsub_ref[...], lam_init, row)


def _attn_ctx(z, zt, sink, lam_par, gsub, *, seq, lam_init):
    t = z.shape[0]
    return pl.pallas_call(
        functools.partial(_attn_ctx_kernel, lam_init=lam_init),
        grid=(t // seq,),
        in_specs=[
            pl.BlockSpec(memory_space=pltpu.SMEM),
            _const_spec((8, LANES)),
            _const_spec((1, LANES)),
            pl.BlockSpec((seq, K_COLS), lambda b: (b, 0)),
            pl.BlockSpec((T_ROWS, seq), lambda b: (0, b)),
        ],
        out_specs=pl.BlockSpec((seq, D_MODEL), lambda b: (b, 0)),
        out_shape=jax.ShapeDtypeStruct((t, D_MODEL), BF16),
        compiler_params=_cparams(("arbitrary",)),
        name="attn_ctx",
    )(sink, lam_par, gsub, z, zt)


def _attn_a_kernel(qt_ref, kc_ref, vct_ref, k_ref, vt_ref, o_ref, q_scr, m_ref, acc_ref,
                   s0, s1, x0, x1, p0, p1, a0, a1, *, tk):
    tq = qt_ref.shape[1]
    lo = lax.broadcasted_iota(jnp.int32, (LANES, tq), 0) < HEAD_DIM
    refs = (m_ref, acc_ref)
    q_scr[...] = _gqa_queries_t(qt_ref, lo)
    _softmax_init_t(*refs)
    _attend_pipelined_t(q_scr, kc_ref, vct_ref, k_ref, vt_ref, tk, (s0, s1), (x0, x1), (p0, p1), (a0, a1), *refs)
    _gqa_store_t(_softmax_result_t(acc_ref), o_ref, tq, lo)


def _keys_major_scratch(n, tk):
    return [pltpu.VMEM((LANES, n), BF16), pltpu.VMEM((1, n), F32),
            pltpu.VMEM((LANES + BF16_ROWS, n), F32),
            pltpu.VMEM((tk, n), F32), pltpu.VMEM((tk, n), F32),
            pltpu.VMEM((1, n), F32), pltpu.VMEM((1, n), F32),
            pltpu.VMEM((tk, n), BF16), pltpu.VMEM((tk, n), BF16),
            pltpu.VMEM((1, n), F32), pltpu.VMEM((1, n), F32)]


def _attn_a(z, zt, k_ctx, vt_ctx, *, seq, tq, tk):
    t = z.shape[0]
    nq = seq // tq
    n_ctx = k_ctx.shape[1]
    n = A_HEADS * tq
    return pl.pallas_call(
        functools.partial(_attn_a_kernel, tk=tk),
        grid=(t // seq, nq),
        in_specs=[
            pl.BlockSpec((384, tq), lambda b, i: (T_QA // 384, b * nq + i)),
            pl.BlockSpec((1, n_ctx, LANES), lambda b, i: (b, 0, 0)),
            pl.BlockSpec((1, LANES, n_ctx), lambda b, i: (b, 0, 0)),
            pl.BlockSpec((seq, LANES), lambda b, i: (b, K_A // LANES)),
            pl.BlockSpec((LANES, seq), lambda b, i: (T_VA // LANES, b)),
        ],
        out_specs=pl.BlockSpec((tq, 384), lambda b, i: (b * nq + i, 0)),
        out_shape=jax.ShapeDtypeStruct((t, 384), BF16),
        scratch_shapes=_keys_major_scratch(n, tk),
        compiler_params=_cparams(("arbitrary", "arbitrary")),
        name="attn_a",
    )(zt, k_ctx, vt_ctx, z, zt)


def _attn_b_kernel(sink_ref, qt_ref, kc_ref, vct_ref, k_ref, vt_ref, o_ref):
    tq = qt_ref.shape[1]
    seq = k_ref.shape[0]
    n = B_HEADS * tq
    band = tq + 2 * WINDOW
    i = pl.program_id(1)
    lo = lax.broadcasted_iota(jnp.int32, (LANES, tq), 0) < HEAD_DIM
    q = _gqa_queries_t(qt_ref, lo)
    start = pl.multiple_of(jnp.clip(i * tq - WINDOW, 0, seq - band), LANES)
    s_c = jnp.dot(kc_ref[0], q, preferred_element_type=F32)
    s_b = jnp.dot(k_ref[pl.ds(start, band), :], q, preferred_element_type=F32)
    qpos = i * tq + lax.broadcasted_iota(jnp.int32, (band, tq), 1)
    kpos = start + lax.broadcasted_iota(jnp.int32, (band, tq), 0)
    inside = jnp.abs(kpos - qpos) <= WINDOW
    s_b = jnp.concatenate([jnp.where(inside, s_b[:, h * tq:(h + 1) * tq], NEG) for h in range(B_HEADS)], axis=1)
    sink = jnp.concatenate([jnp.full((1, tq), sink_ref[h] * LOG2E, F32) for h in range(B_HEADS)], axis=1)
    m = jnp.maximum(sink, jnp.maximum(jnp.max(s_c, axis=0, keepdims=True), jnp.max(s_b, axis=0, keepdims=True)))
    p_c = jnp.exp2(s_c - m)
    p_b = jnp.exp2(s_b - m)
    l = jnp.exp2(sink - m) + jnp.sum(p_c, axis=0, keepdims=True) + jnp.sum(p_b, axis=0, keepdims=True)
    acc = (jnp.dot(vct_ref[0], p_c.astype(BF16), preferred_element_type=F32)
           + jnp.dot(vt_ref[:, pl.ds(start, band)], p_b.astype(BF16), preferred_element_type=F32))
    _gqa_store_t(acc * (1.0 / l), o_ref, tq, lo)


def _attn_b(z, zt, k_ctx, vt_ctx, sink, *, seq, tq):
    t = z.shape[0]
    nq = seq // tq
    n_ctx = k_ctx.shape[1]
    return pl.pallas_call(
        _attn_b_kernel,
        grid=(t // seq, nq),
        in_specs=[
            pl.BlockSpec(memory_space=pltpu.SMEM),
            pl.BlockSpec((384, tq), lambda b, i: (T_QB // 384, b * nq + i)),
            pl.BlockSpec((1, n_ctx, LANES), lambda b, i: (b, 0, 0)),
            pl.BlockSpec((1, LANES, n_ctx), lambda b, i: (b, 0, 0)),
            pl.BlockSpec((seq, LANES), lambda b, i: (b, K_B // LANES)),
            pl.BlockSpec((LANES, seq), lambda b, i: (T_VB // LANES, b)),
        ],
        out_specs=pl.BlockSpec((tq, 384), lambda b, i: (b * nq + i, 0)),
        out_shape=jax.ShapeDtypeStruct((t, 384), BF16),
        compiler_params=_cparams(("arbitrary", "arbitrary")),
        name="attn_b",
    )(sink, zt, k_ctx, vt_ctx, z, zt)


def _attn_c_kernel(lam_ref, gsub_ref, qt_ref, kc_ref, vct_ref, k_ref, vt_ref, o_ref,
                   q_scr, m_ref, acc_ref, s0, s1, x0, x1, p0, p1, a0, a1, *, tk, lam_init):
    tq = qt_ref.shape[1]
    row = lax.broadcasted_iota(jnp.int32, (LANES, tq), 0)
    refs = (m_ref, acc_ref)
    q_scr[...] = _diff_queries_t(qt_ref[...], row)
    _softmax_init_t(*refs)
    _attend_pipelined_t(q_scr, kc_ref, vct_ref, k_ref, vt_ref, tk, (s0, s1), (x0, x1), (p0, p1), (a0, a1), *refs)
    lam = _diff_lambda(lam_ref, lam_init)
    o_ref[...] = _diff_output_t(_softmax_result_t(acc_ref), tq, lam, gsub_ref[...], lam_init, row)


def _attn_c(z, zt, k_ctx, vt_ctx, lam_par, gsub, *, seq, tq, tk, lam_init):
    t = z.shape[0]
    nq = seq // tq
    n_ctx = k_ctx.shape[1]
    n = 4 * tq
    return pl.pallas_call(
        functools.partial(_attn_c_kernel, tk=tk, lam_init=lam_init),
        grid=(t // seq, 2, nq),
        in_specs=[
            _const_spec((8, LANES)),
            _const_spec((1, LANES)),
            pl.BlockSpec((LANES, tq), lambda b, s, i: (T_QC // LANES + s, b * nq + i)),
            pl.BlockSpec((1, n_ctx, LANES), lambda b, s, i: (b, 0, s)),
            pl.BlockSpec((1, LANES, n_ctx), lambda b, s, i: (b, s, 0)),
            pl.BlockSpec((seq, LANES), lambda b, s, i: (b, K_C // LANES + s)),
            pl.BlockSpec((LANES, seq), lambda b, s, i: (T_VC // LANES + s, b)),
        ],
        out_specs=pl.BlockSpec((tq, LANES), lambda b, s, i: (b * nq + i, s)),
        out_shape=jax.ShapeDtypeStruct((t, C_HEADS * C_V_DIM), BF16),
        scratch_shapes=_keys_major_scratch(n, tk),
        compiler_params=_cparams(("arbitrary", "arbitrary", "arbitrary")),
        name="attn_c",
    )(lam_par, gsub, zt, k_ctx, vt_ctx, z, zt)


def _post_kernel(*refs, n_parts, halo, tiles_per_seq, final):
    it = iter(refs)
    tiles = [next(it) for _ in range(1 + n_parts)]
    halos = [(next(it), next(it)) for _ in range(1 + n_parts)] if halo else None
    (mod_ref, gn_ref, wo_ref, wu_ref, cw_ref, cb_ref, wd_ref, gf_ref, out_ref, act_scr) = (next(it) for _ in range(10))

    def rows_of(j):
        if halo:
            return jnp.concatenate([halos[j][0][...], tiles[j][...], halos[j][1][...]], axis=0)
        return tiles[j][...]

    tm = tiles[0].shape[0]
    x = rows_of(0)
    o = jnp.concatenate([rows_of(j) for j in range(1, 1 + n_parts)], axis=1)
    ext = x.shape[0]
    mod = mod_ref[0]
    x1 = x + mod[2:3] * jnp.dot(o, wo_ref[0], preferred_element_type=F32)
    h = _rms(x1, gn_ref[...]) * (1.0 + mod[4:5]) + mod[3:4]
    row = lax.broadcasted_iota(jnp.int32, (ext, 1), 0)
    if halo:
        t_in_seq = pl.program_id(0) % tiles_per_seq
        keep = ((row >= halo) | (t_in_seq > 0)) & ((row < halo + tm) | (t_in_seq < tiles_per_seq - 1))
        h = jnp.where(keep, h, 0.0)
    h = h.astype(BF16)

    def conv(u, c0):
        cw = cw_ref[:, c0:c0 + FFN_CHUNK]
        up = pltpu.roll(u, 1, 0)
        dn = pltpu.roll(u, ext - 1, 0)
        if not halo:
            up = jnp.where(row == 0, 0.0, up)
            dn = jnp.where(row == ext - 1, 0.0, dn)
        v = cw[0:1] * up + cw[1:2] * u + cw[2:3] * dn + cb_ref[:, c0:c0 + FFN_CHUNK]
        return v[halo:halo + tm]

    def up_proj(c):
        ca, cg = c * FFN_CHUNK, D_FF + c * FFN_CHUNK
        return (jnp.dot(h, wu_ref[0, :, ca:ca + FFN_CHUNK], preferred_element_type=F32),
                jnp.dot(h, wu_ref[0, :, cg:cg + FFN_CHUNK], preferred_element_type=F32))

    n_chunks = D_FF // FFN_CHUNK
    nxt = up_proj(0)
    for c in range(n_chunks):
        ua, ug = nxt
        if c + 1 < n_chunks:
            nxt = up_proj(c + 1)
        ca = c * FFN_CHUNK
        a = conv(ua, ca)
        g = conv(ug, D_FF + ca)
        act_scr[:, ca:ca + FFN_CHUNK] = (a / (1.0 + jnp.exp(-a)) * g).astype(BF16)
    x2 = x1[halo:halo + tm] + mod[5:6] * jnp.dot(act_scr[...], wd_ref[0], preferred_element_type=F32)
    if final:
        x2 = _rms(x2, gf_ref[...])
    out_ref[...] = x2


def _post(x, o_parts, mod_l, gn2, wo, wu, cw, cb, wd, gf, *, layer, tm, tiles_per_cond, tiles_per_seq):
    t = x.shape[0]
    final = layer == DEPTH - 1
    assert sum(o.shape[1] for o in o_parts) == D_MODEL
    halo = HALO if tiles_per_seq > 1 else 0
    tile = lambda i: (i, 0)
    rows = [x] + list(o_parts)
    in_specs = [pl.BlockSpec((tm, a.shape[1]), tile) for a in rows]
    args = list(rows)
    if halo:
        per = tm // halo
        prev = lambda i: (jnp.maximum(i * per - 1, 0), 0)
        nxt = lambda i: (jnp.minimum((i + 1) * per, t // halo - 1), 0)
        for a in rows:
            in_specs += [pl.BlockSpec((halo, a.shape[1]), prev), pl.BlockSpec((halo, a.shape[1]), nxt)]
            args += [a, a]
    in_specs += [
        pl.BlockSpec((1, 6, D_MODEL), lambda i: (i // tiles_per_cond, 0, 0)),
        _const_spec((1, D_MODEL)),
        _layer_spec((D_MODEL, D_MODEL), layer),
        _layer_spec((D_MODEL, 2 * D_FF), layer),
        _const_spec((3, 2 * D_FF)),
        _const_spec((1, 2 * D_FF)),
        _layer_spec((D_FF, D_MODEL), layer),
        _const_spec((1, D_MODEL)),
    ]
    args += [mod_l, gn2, wo, wu, cw, cb, wd, gf]
    return pl.pallas_call(
        functools.partial(_post_kernel, n_parts=len(o_parts), halo=halo, tiles_per_seq=tiles_per_seq, final=final),
        grid=(t // tm,),
        in_specs=in_specs,
        out_specs=pl.BlockSpec((tm, D_MODEL), tile),
        out_shape=jax.ShapeDtypeStruct((t, D_MODEL), F32),
        scratch_shapes=[pltpu.VMEM((tm, D_FF), BF16)],
        compiler_params=_cparams(("arbitrary",)),
        name="post",
    )(*args)


def _rope_tables(seq):
    t = np.arange(seq)
    rows = (t // GRID_W).astype(np.float32)[:, None]
    cols = (t % GRID_W).astype(np.float32)[:, None]

    def tab(half, reps):
        inv = np.float32(ROPE_THETA) ** (-np.arange(half, dtype=np.float32) / np.float32(half))
        ar, ac = rows * inv[None, :], cols * inv[None, :]
        cos = np.concatenate([np.cos(ar), np.cos(ar), np.cos(ac), np.cos(ac)], axis=-1)
        sin = np.concatenate([-np.sin(ar), np.sin(ar), -np.sin(ac), np.sin(ac)], axis=-1)
        return np.tile(cos, (1, reps)).astype(np.float32), np.tile(sin, (1, reps)).astype(np.float32)

    tabs = tab(HEAD_DIM // 4, LANES // HEAD_DIM) + tab(C_QK_DIM // 4, LANES // C_QK_DIM)
    tabs = tabs + tuple(np.ascontiguousarray(t.T) for t in tabs)
    return tuple(jnp.asarray(t) for t in tabs)


def _w_in_perms():
    sizes = (A_HEADS * HEAD_DIM, A_KV * HEAD_DIM, A_KV * HEAD_DIM, B_HEADS * HEAD_DIM, B_KV * HEAD_DIM,
             B_KV * HEAD_DIM, C_HEADS * 2 * C_QK_DIM, C_HEADS * 2 * C_QK_DIM, C_HEADS * C_V_DIM)
    offs = np.concatenate([[0], np.cumsum(sizes)])
    qa, ka, va, qb, kb, vb, qc, kc, vc = (np.arange(offs[j], offs[j + 1]) for j in range(9))
    pair = np.concatenate([np.arange(h * HEAD_DIM, (h + 1) * HEAD_DIM) for h in PAIRED_HEADS])
    return np.concatenate([qa[pair], va, qc, vc, vb, qb[pair]]), np.concatenate([ka, kb, kc])


def _w_out_perm():
    pair = np.concatenate([np.arange(h * HEAD_DIM, (h + 1) * HEAD_DIM) for h in PAIRED_HEADS])
    return np.concatenate([pair, A_HEADS * HEAD_DIM + pair, np.arange(768, D_MODEL)])


def _take_runs(w, perm, axis):
    cuts = [0] + [j for j in range(1, len(perm)) if perm[j] != perm[j - 1] + 1] + [len(perm)]
    parts = [lax.slice_in_dim(w, int(perm[a]), int(perm[b - 1]) + 1, axis=axis) for a, b in zip(cuts[:-1], cuts[1:])]
    return jnp.concatenate(parts, axis=axis)


def kernel(x_prompt, x_sample, cache_a_k, cache_a_v, cache_b_k, cache_b_v, cache_c_k, cache_c_v, c, c_ctx, w_ada, b_ada, g_norm1, g_norm2, w_in, g_qa, g_ka, sink_b, lam_q1, lam_k1, lam_q2, lam_k2, g_subln, w_out, w_up, conv_w, conv_b, w_down, g_final):
    n_ctx_req, ctx_len, _ = x_prompt.shape
    n_lat_req, lat_len, _ = x_sample.shape
    past = cache_a_k.shape[2]

    conds = jnp.zeros((8, D_MODEL), F32).at[0].set(c_ctx).at[1:1 + n_lat_req].set(c)
    mod = _modulation(conds, w_ada, b_ada).reshape(DEPTH, 8, 6, D_MODEL)

    perm_t, perm_k = _w_in_perms()
    w_t = jnp.swapaxes(_take_runs(w_in, perm_t, 2), 1, 2).astype(BF16)
    w_k = _take_runs(w_in, perm_k, 2).astype(BF16)
    w_out_p = _take_runs(w_out, _w_out_perm(), 1).astype(BF16)
    w_up_b = w_up.astype(BF16)
    w_down_b = w_down.astype(BF16)
    rope_tabs = _rope_tables(lat_len)
    zpad = jnp.zeros((DEPTH, LANES - C_QK_DIM), F32)
    lam_rows = [jnp.concatenate([v, zpad], axis=-1) for v in (lam_q1, lam_k1, lam_q2, lam_k2)]
    lam_par = jnp.concatenate([jnp.stack(lam_rows, axis=1), jnp.zeros((DEPTH, 4, LANES), F32)], axis=1)
    gf = g_final.reshape(1, D_MODEL)

    xc = x_prompt.reshape(n_ctx_req * ctx_len, D_MODEL)
    xs = x_sample.reshape(n_lat_req * lat_len, D_MODEL)
    lat_tm = 512
    caches = []
    for l in range(DEPTH):
        lam_init = 0.8 - 0.6 * math.exp(-0.3 * l)
        gn1 = g_norm1[l].reshape(1, D_MODEL)
        gn2 = g_norm2[l].reshape(1, D_MODEL)
        gq_col = jnp.tile(g_qa[l], LANES // HEAD_DIM)[:, None]
        gk = jnp.tile(g_ka[l], LANES // HEAD_DIM).reshape(1, LANES)
        gsub = jnp.tile(g_subln[l], LANES // C_V_DIM).reshape(1, LANES)
        post_w = (gn2, w_out_p, w_up_b, conv_w[l], conv_b[l].reshape(1, 2 * D_FF), w_down_b, gf)
        final = l == DEPTH - 1

        n_tiles = n_ctx_req
        zc, *caches, ztc = _in_proj(xc, mod[l, 0:1], gn1, w_t, w_k, jnp.broadcast_to(gq_col, (LANES, ctx_len)), gk,
                                    None, layer=l, tm=ctx_len, tiles_per_cond=n_tiles, tiles_per_seq=1,
                                    emit_cache="final" if final else "rows",
                                    prev_cache=caches[0] if final else None)
        oc = _attn_ctx(zc, ztc, sink_b[l], lam_par[l], gsub, seq=ctx_len, lam_init=lam_init)
        xc = _post(xc, [oc], mod[l, 0:1], *post_w, layer=l, tm=ctx_len, tiles_per_cond=n_tiles, tiles_per_seq=1)

        per_seq = lat_len // lat_tm
        zs, zts = _in_proj(xs, mod[l, 1:1 + n_lat_req], gn1, w_t, w_k, jnp.broadcast_to(gq_col, (LANES, lat_tm)), gk,
                           rope_tabs, layer=l, tm=lat_tm, tiles_per_cond=per_seq, tiles_per_seq=per_seq,
                           emit_cache=None)
        flat = lambda a: a[:, l].reshape(n_lat_req, past, -1).astype(BF16)
        flat_t = lambda a: jnp.swapaxes(flat(a), 1, 2)
        oa = _attn_a(zs, zts, flat(cache_a_k), flat_t(cache_a_v), seq=lat_len, tq=512, tk=512)
        ob = _attn_b(zs, zts, flat(cache_b_k), flat_t(cache_b_v), sink_b[l], seq=lat_len, tq=256)
        oc = _attn_c(zs, zts, flat(cache_c_k), flat_t(cache_c_v), lam_par[l], gsub, seq=lat_len, tq=512, tk=512,
                     lam_init=lam_init)
        xs = _post(xs, [oa, ob, oc], mod[l, 1:1 + n_lat_req], *post_w, layer=l, tm=lat_tm, tiles_per_cond=per_seq,
                   tiles_per_seq=per_seq)

    heads = (A_KV, A_KV, B_KV, B_KV, C_HEADS, C_HEADS)
    new_caches = tuple(cch.reshape(n_ctx_req, DEPTH, ctx_len, h, -1) for cch, h in zip(caches, heads))
    return (xc.reshape(x_prompt.shape), xs.reshape(x_sample.shape)) + new_caches
```

```python
import functools
import math

import numpy as np
import jax
import jax.numpy as jnp
from jax import lax
from jax.experimental import pallas as pl
from jax.experimental.pallas import tpu as pltpu

D_MODEL = 1024
DEPTH = 2
GRID_W = 64
HEAD_DIM = 64
A_HEADS = 6
A_KV = 2
B_HEADS = 6
B_KV = 2
C_HEADS = 4
C_QK_DIM = 32
C_V_DIM = 2 * C_QK_DIM
WINDOW = 128
ROPE_THETA = 10000.0
D_FF = 2816
EPS = 1e-6
NEG = -1e30
LOG2E = math.log2(math.e)

LANES = 128
BF16_ROWS = 16
MXU_COLS = 256
VMEM_LIMIT = 56 * 1024 * 1024

GQA_COLS = A_HEADS * HEAD_DIM
MIX_C0 = 2 * GQA_COLS
Z_KA, Z_VA, Z_KB, Z_VB, Z_KC, Z_VC, CACHE_COLS = 0, 128, 256, 384, 512, 768, 1024
CACHE_PIECES = ((Z_KA, 128), (Z_VA, 128), (Z_KB, 128), (Z_VB, 128), (Z_KC, 256), (Z_VC, 256))
K_A, K_B, K_C, K_COLS = 0, 128, 256, 512
T_QA, T_VA, T_QC, T_VC, T_VB, T_QB, T_ROWS = 0, 384, 512, 768, 1024, 1152, 1536
PAIRED_HEADS = (0, 3, 1, 4, 2, 5)
FFN_CHUNK = MXU_COLS
HALO = BF16_ROWS
LATENT_TM = 512
ATTN_A_TQ, ATTN_B_TQ, ATTN_C_TQ, ATTN_TK = 512, 256, 512, 512
MOD_COLS = 1536

F32 = jnp.float32
BF16 = jnp.bfloat16


def _cparams(sem):
    return pltpu.CompilerParams(dimension_semantics=sem, vmem_limit_bytes=VMEM_LIMIT)


def _const_spec(shape):
    nd = len(shape)
    return pl.BlockSpec(shape, lambda *_: (0,) * nd)


def _layer_spec(shape, layer):
    nd = len(shape)
    return pl.BlockSpec((1,) + tuple(shape), lambda *_: (layer,) + (0,) * nd)


def _mod_kernel(c_ref, w_ref, b_ref, o_ref):
    cond = c_ref[...]
    a = cond / (1.0 + jnp.exp(-cond))
    o_ref[0] = jnp.dot(a.astype(BF16), w_ref[0].astype(BF16), preferred_element_type=F32) + b_ref[0]


def _modulation(conds, w_ada, b_ada):
    nb = MOD_COLS
    n_out = w_ada.shape[-1]
    return pl.pallas_call(
        _mod_kernel,
        grid=(DEPTH, n_out // nb),
        in_specs=[
            pl.BlockSpec((8, D_MODEL), lambda l, j: (0, 0)),
            pl.BlockSpec((1, D_MODEL, nb), lambda l, j: (l, 0, j)),
            pl.BlockSpec((1, 1, nb), lambda l, j: (l, 0, j)),
        ],
        out_specs=pl.BlockSpec((1, 8, nb), lambda l, j: (l, 0, j)),
        out_shape=jax.ShapeDtypeStruct((DEPTH, 8, n_out), F32),
        compiler_params=_cparams(("arbitrary", "arbitrary")),
        name="modulation",
    )(conds, w_ada, b_ada.reshape(DEPTH, 1, n_out))


def _rms(x, g):
    ms = jnp.mean(x * x, axis=-1, keepdims=True)
    return x * lax.rsqrt(ms + EPS) * g


def _head_rms(x, g, lo):
    ss = x * x
    s_lo = jnp.sum(jnp.where(lo, ss, 0.0), axis=-1, keepdims=True)
    s_hi = jnp.sum(jnp.where(lo, 0.0, ss), axis=-1, keepdims=True)
    inv = jnp.where(lo, lax.rsqrt(s_lo * (1.0 / HEAD_DIM) + EPS), lax.rsqrt(s_hi * (1.0 / HEAD_DIM) + EPS))
    return x * inv * g


def _rope(x, cos, sin, chunk, first):
    sw = jnp.where(first, pltpu.roll(x, LANES - chunk, 1), pltpu.roll(x, chunk, 1))
    return x * cos + sw * sin


def _softmax_init_t(m_ref, acc_ref):
    m_ref[...] = jnp.full(m_ref.shape, NEG, F32)
    acc_ref[...] = jnp.zeros(acc_ref.shape, F32)


def _scores_t(k, q_scr, s_ref, mx_ref, cs):
    s = jnp.dot(k, q_scr[:, cs], preferred_element_type=F32)
    s_ref[0:k.shape[0], cs] = s
    mx_ref[:, cs] = jnp.max(s, axis=0, keepdims=True)


def _probs_t(rows, s_ref, mx_ref, p_ref, a_ref, m_ref, cs):
    s = s_ref[0:rows, cs]
    m_prev = m_ref[:, cs]
    m_new = jnp.maximum(m_prev, mx_ref[:, cs])
    a_ref[:, cs] = jnp.exp2(m_prev - m_new)
    m_ref[:, cs] = m_new
    p_ref[0:rows, cs] = jnp.exp2(s - m_new).astype(BF16)


def _pv_and_sum(vt, p):
    ones = jnp.ones((BF16_ROWS, vt.shape[1]), BF16)
    return jnp.dot(jnp.concatenate([vt, ones], axis=0), p, preferred_element_type=F32)


def _values_t(vt, p_ref, a_ref, acc_ref, cs):
    acc_ref[:, cs] = a_ref[:, cs] * acc_ref[:, cs] + _pv_and_sum(vt, p_ref[0:vt.shape[1], cs])


def _softmax_result_t(acc_ref):
    return acc_ref[0:LANES] * (1.0 / acc_ref[LANES:LANES + 1])


def _attend_pipelined_t(q_scr, kc_ref, vct_ref, k_ref, vt_ref, tk, s_bufs, x_bufs, p_bufs, a_bufs, m_ref, acc_ref):
    n_lat = k_ref.shape[0] // tk
    n_ctx = kc_ref.shape[1]
    assert n_ctx <= tk and n_lat % 2 == 0 and n_lat >= 4

    def k_lat(j):
        return k_ref[pl.ds(pl.multiple_of(j * tk, tk), tk), :]

    def v_lat(j):
        return vt_ref[:, pl.ds(pl.multiple_of(j * tk, tk), tk)]

    n = q_scr.shape[1]
    groups = [slice(c, c + MXU_COLS) for c in range(0, n, MXU_COLS)]

    def step(par, k=None, vt=None, probs=tk):
        for cs in groups:
            if probs:
                _probs_t(probs, s_bufs[par], x_bufs[par], p_bufs[par], a_bufs[par], m_ref, cs)
            if vt is not None:
                _values_t(vt, p_bufs[1 - par], a_bufs[1 - par], acc_ref, cs)
            if k is not None:
                _scores_t(k, q_scr, s_bufs[1 - par], x_bufs[1 - par], cs)

    step(1, k=kc_ref[0], probs=0)
    step(0, k=k_lat(0), probs=n_ctx)
    step(1, k=k_lat(1), vt=vct_ref[0])

    def pair(jj, carry):
        j = 2 * jj
        step(0, k=k_lat(j + 2), vt=v_lat(j))
        step(1, k=k_lat(j + 3), vt=v_lat(j + 1))
        return carry

    for jj in range(n_lat // 2 - 1):
        pair(jj, 0)
    step(0, vt=v_lat(n_lat - 2))
    step(1, vt=v_lat(n_lat - 1), probs=0)


def _diff_lambda(lam_ref, lam_init):
    f = lambda a, b: jnp.exp(jnp.sum(a * b, axis=-1, keepdims=True))
    return f(lam_ref[0:1], lam_ref[1:2]) - f(lam_ref[2:3], lam_ref[3:4]) + lam_init


def _swap_row_chunks(x, chunk):
    parts = []
    for r in range(0, x.shape[0], 2 * chunk):
        parts += [x[r + chunk:r + 2 * chunk], x[r:r + chunk]]
    return jnp.concatenate(parts, axis=0)


def _in_proj_kernel(*refs, use_rope, emit_cache):
    it = iter(refs)
    x_ref, mod_ref, gn_ref, wt_ref, wk_ref, gq_ref, gk_ref = (next(it) for _ in range(7))
    cos64 = sin64 = cos32 = sin32 = cos64t = sin64t = cos32t = sin32t = None
    if use_rope:
        cos64, sin64, cos32, sin32, cos64t, sin64t, cos32t, sin32t = (next(it)[...] for _ in range(8))
    prev_ref = next(it) if emit_cache == "final" else None
    zk_ref = next(it)
    cache_ref = next(it) if emit_cache == "rows" else None
    final_refs = [next(it) for _ in CACHE_PIECES] if emit_cache == "final" else None
    zt_ref = next(it)
    if final_refs:
        for ref, (start, width) in zip(final_refs, CACHE_PIECES):
            ref[0, 0] = prev_ref[:, start:start + width]

    tm = x_ref.shape[0]
    sub = min(tm, MXU_COLS)
    mod = mod_ref[0]
    lane = lax.broadcasted_iota(jnp.int32, (sub, LANES), 1)
    lo = lane < HEAD_DIM
    first16 = (lane & 31) < 16
    first8 = (lane & 15) < 8
    q_scale = HEAD_DIM ** -0.5 * LOG2E
    qc_scale = C_QK_DIM ** -0.5 * LOG2E

    def matmuls(r0):
        h = (_rms(x_ref[r0:r0 + sub], gn_ref[...]) * (1.0 + mod[1:2]) + mod[0:1]).astype(BF16)
        zk = jnp.dot(h, wk_ref[0], preferred_element_type=F32)
        zt = lax.dot_general(wt_ref[0], h, (((1,), (1,)), ((), ())), preferred_element_type=F32)
        return zk, zt

    def put_cache(r0, off, v):
        if cache_ref is not None:
            cache_ref[r0:r0 + sub, off:off + LANES] = v
        if final_refs:
            for ref, (start, width) in zip(final_refs, CACHE_PIECES):
                if start <= off < start + width:
                    ref[0, DEPTH - 1, r0:r0 + sub, off - start:off - start + LANES] = v

    def finish_keys(r0, zk):
        def tab(t):
            return t[r0:r0 + sub] if use_rope else None

        def rope64(v):
            return _rope(v, tab(cos64), tab(sin64), 16, first16) if use_rope else v

        def rope32(v):
            return _rope(v, tab(cos32), tab(sin32), 8, first8) if use_rope else v

        def put_k(off, v):
            zk_ref[r0:r0 + sub, off:off + LANES] = v.astype(BF16)

        ka = _head_rms(zk[:, K_A:K_A + LANES], gk_ref[...], lo)
        put_cache(r0, Z_KA, ka)
        put_k(K_A, rope64(ka))
        kb = zk[:, K_B:K_B + LANES]
        put_cache(r0, Z_KB, kb)
        put_k(K_B, rope64(kb))
        for s in range(2):
            kc = zk[:, K_C + LANES * s:K_C + LANES * (s + 1)]
            put_cache(r0, Z_KC + LANES * s, kc)
            put_k(K_C + LANES * s, rope32(kc))

    def finish_features(r0, zt):
        def t_slab(off):
            return zt[off:off + LANES]

        def put_t(off, v):
            zt_ref[off:off + LANES, r0:r0 + sub] = v.astype(BF16)

        def rope_t(v, cos_t, sin_t, chunk):
            if not use_rope:
                return v
            return v * cos_t[:, r0:r0 + sub] + _swap_row_chunks(v, chunk) * sin_t[:, r0:r0 + sub]

        def head_rms_t(v):
            ss = v * v
            halves = []
            for r in (0, HEAD_DIM):
                inv = lax.rsqrt(jnp.sum(ss[r:r + HEAD_DIM], axis=0, keepdims=True) * (1.0 / HEAD_DIM) + EPS)
                halves.append(v[r:r + HEAD_DIM] * inv)
            return jnp.concatenate(halves, axis=0) * gq_ref[:, 0:sub]

        for s in range(3):
            put_t(T_QA + LANES * s, rope_t(head_rms_t(t_slab(T_QA + LANES * s)), cos64t, sin64t, 16) * q_scale)
            put_t(T_QB + LANES * s, rope_t(t_slab(T_QB + LANES * s), cos64t, sin64t, 16) * q_scale)
        for s in range(2):
            put_t(T_QC + LANES * s, rope_t(t_slab(T_QC + LANES * s), cos32t, sin32t, 8) * qc_scale)
        for t_off, z_off in ((T_VA, Z_VA), (T_VB, Z_VB), (T_VC, Z_VC), (T_VC + LANES, Z_VC + LANES)):
            put_t(t_off, t_slab(t_off))
            if emit_cache:
                put_cache(r0, z_off, t_slab(t_off).T)

    starts = list(range(0, tm, sub))
    prods = [matmuls(r0) for r0 in starts]
    for r0, (zk, _) in zip(starts, prods):
        finish_keys(r0, zk)
    for r0, (_, zt) in zip(starts, prods):
        finish_features(r0, zt)


def _in_proj(x, mod_l, gn, w_t, w_k, gq_t, gk, rope_tabs, *, layer, tm, tiles_per_cond, tiles_per_seq, emit_cache,
             prev_cache=None):
    t = x.shape[0]
    use_rope = rope_tabs is not None
    assert (emit_cache == "final") == (prev_cache is not None) and DEPTH == 2
    in_specs = [
        pl.BlockSpec((tm, D_MODEL), lambda i: (i, 0)),
        pl.BlockSpec((1, 6, D_MODEL), lambda i: (i // tiles_per_cond, 0, 0)),
        _const_spec((1, D_MODEL)),
        _layer_spec((T_ROWS, D_MODEL), layer),
        _layer_spec((D_MODEL, K_COLS), layer),
        _const_spec((LANES, tm)),
        _const_spec((1, LANES)),
    ]
    args = [x, mod_l, gn, w_t, w_k, gq_t, gk]
    if use_rope:
        in_specs += [pl.BlockSpec((tm, LANES), lambda i: (i % tiles_per_seq, 0))] * 4
        in_specs += [pl.BlockSpec((LANES, tm), lambda i: (0, i % tiles_per_seq))] * 4
        args += list(rope_tabs)
    if prev_cache is not None:
        in_specs.append(pl.BlockSpec((tm, CACHE_COLS), lambda i: (i, 0)))
        args.append(prev_cache)
    out_shape = [jax.ShapeDtypeStruct((t, K_COLS), BF16)]
    out_specs = [pl.BlockSpec((tm, K_COLS), lambda i: (i, 0))]
    if emit_cache == "rows":
        out_shape.append(jax.ShapeDtypeStruct((t, CACHE_COLS), F32))
        out_specs.append(pl.BlockSpec((tm, CACHE_COLS), lambda i: (i, 0)))
    if emit_cache == "final":
        for _, width in CACHE_PIECES:
            out_shape.append(jax.ShapeDtypeStruct((t // tm, DEPTH, tm, width), F32))
            out_specs.append(pl.BlockSpec((1, DEPTH, tm, width), lambda i: (i, 0, 0, 0)))
    out_shape.append(jax.ShapeDtypeStruct((T_ROWS, t), BF16))
    out_specs.append(pl.BlockSpec((T_ROWS, tm), lambda i: (0, i)))
    return pl.pallas_call(
        functools.partial(_in_proj_kernel, use_rope=use_rope, emit_cache=emit_cache),
        grid=(t // tm,),
        in_specs=in_specs,
        out_specs=out_specs,
        out_shape=out_shape,
        compiler_params=_cparams(("arbitrary",)),
        name="in_proj",
    )(*args)


def _gqa_queries_t(qt_ref, lo):
    slabs = [qt_ref[LANES * s:LANES * (s + 1), :] for s in range(3)]
    zero = jnp.zeros_like(slabs[0])
    return jnp.concatenate([jnp.where(lo, s, zero) for s in slabs] + [jnp.where(lo, zero, s) for s in slabs], axis=1)


def _gqa_store_t(ot, o_ref, tq, lo, col0=0):
    for s in range(3):
        slab_t = jnp.where(lo, ot[:, s * tq:(s + 1) * tq], ot[:, (3 + s) * tq:(4 + s) * tq])
        o_ref[:, col0 + LANES * s:col0 + LANES * (s + 1)] = slab_t.T.astype(BF16)


def _diff_queries_t(qt, row):
    zero = jnp.zeros_like(qt)
    return jnp.concatenate([jnp.where((row >= C_QK_DIM * j) & (row < C_QK_DIM * (j + 1)), qt, zero)
                            for j in range(4)], axis=1)


def _diff_output_t(ot, tq, lam, gsub, lam_init, row):
    o_even = ot[:, 0:tq] - lam * ot[:, tq:2 * tq]
    o_odd = ot[:, 2 * tq:3 * tq] - lam * ot[:, 3 * tq:4 * tq]
    oc = jnp.where(row < C_V_DIM, o_even, o_odd).T
    lo = lax.broadcasted_iota(jnp.int32, (tq, LANES), 1) < C_V_DIM
    return (_head_rms(oc, gsub, lo) * (1.0 - lam_init)).astype(BF16)


def _attn_ctx_kernel(sink_ref, lam_ref, gsub_ref, z_ref, zt_ref, o_ref, *, lam_init):
    tq = z_ref.shape[0]
    row = lax.broadcasted_iota(jnp.int32, (LANES, tq), 0)
    lo = row < HEAD_DIM

    def attend(q, k, vt, sink=None):
        s = jnp.dot(k, q, preferred_element_type=F32)
        m = jnp.max(s, axis=0, keepdims=True)
        if sink is not None:
            m = jnp.maximum(m, sink)
        acc = _pv_and_sum(vt, jnp.exp2(s - m).astype(BF16))
        l = acc[LANES:LANES + 1]
        if sink is not None:
            l = l + jnp.exp2(sink - m)
        return acc[0:LANES] * (1.0 / l)

    def keys(off):
        return z_ref[:, off:off + LANES]

    def feat(off):
        return zt_ref[off:off + LANES, :]

    _gqa_store_t(attend(_gqa_queries_t(zt_ref.at[T_QA:T_QA + GQA_COLS], lo), keys(K_A), feat(T_VA)), o_ref, tq, lo)
    sink = jnp.concatenate([jnp.full((1, tq), sink_ref[h] * LOG2E, F32) for h in range(B_HEADS)], axis=1)
    _gqa_store_t(attend(_gqa_queries_t(zt_ref.at[T_QB:T_QB + GQA_COLS], lo), keys(K_B), feat(T_VB), sink),
                 o_ref, tq, lo, col0=GQA_COLS)
    lam = _diff_lambda(lam_ref, lam_init)
    for s in range(2):
        ot = attend(_diff_queries_t(feat(T_QC + LANES * s), row), keys(K_C + LANES * s), feat(T_VC + LANES * s))
        c0 = MIX_C0 + LANES * s
        o_ref[:, c0:c0 + LANES] = _diff_output_t(ot, tq, lam, gsub_ref[...], lam_init, row)


def _attn_ctx(z, zt, sink, lam_par, gsub, *, seq, lam_init):
    t = z.shape[0]
    return pl.pallas_call(
        functools.partial(_attn_ctx_kernel, lam_init=lam_init),
        grid=(t // seq,),
        in_specs=[
            pl.BlockSpec(memory_space=pltpu.SMEM),
            _const_spec((8, LANES)),
            _const_spec((1, LANES)),
            pl.BlockSpec((seq, K_COLS), lambda b: (b, 0)),
            pl.BlockSpec((T_ROWS, seq), lambda b: (0, b)),
        ],
        out_specs=pl.BlockSpec((seq, D_MODEL), lambda b: (b, 0)),
        out_shape=jax.ShapeDtypeStruct((t, D_MODEL), BF16),
        compiler_params=_cparams(("arbitrary",)),
        name="attn_ctx",
    )(sink, lam_par, gsub, z, zt)


def _attn_a_kernel(qt_ref, kc_ref, vct_ref, k_ref, vt_ref, o_ref, q_scr, m_ref, acc_ref,
                   s0, s1, x0, x1, p0, p1, a0, a1, *, tk):
    tq = qt_ref.shape[1]
    lo = lax.broadcasted_iota(jnp.int32, (LANES, tq), 0) < HEAD_DIM
    refs = (m_ref, acc_ref)
    q_scr[...] = _gqa_queries_t(qt_ref, lo)
    _softmax_init_t(*refs)
    _attend_pipelined_t(q_scr, kc_ref, vct_ref, k_ref, vt_ref, tk, (s0, s1), (x0, x1), (p0, p1), (a0, a1), *refs)
    _gqa_store_t(_softmax_result_t(acc_ref), o_ref, tq, lo)


def _keys_major_scratch(n, tk):
    return [pltpu.VMEM((LANES, n), BF16), pltpu.VMEM((1, n), F32),
            pltpu.VMEM((LANES + BF16_ROWS, n), F32),
            pltpu.VMEM((tk, n), F32), pltpu.VMEM((tk, n), F32),
            pltpu.VMEM((1, n), F32), pltpu.VMEM((1, n), F32),
            pltpu.VMEM((tk, n), BF16), pltpu.VMEM((tk, n), BF16),
            pltpu.VMEM((1, n), F32), pltpu.VMEM((1, n), F32)]


def _attn_a(z, zt, k_ctx, vt_ctx, *, seq, tq, tk):
    t = z.shape[0]
    nq = seq // tq
    n_ctx = k_ctx.shape[1]
    n = A_HEADS * tq
    return pl.pallas_call(
        functools.partial(_attn_a_kernel, tk=tk),
        grid=(t // seq, nq),
        in_specs=[
            pl.BlockSpec((GQA_COLS, tq), lambda b, i: (T_QA // GQA_COLS, b * nq + i)),
            pl.BlockSpec((1, n_ctx, LANES), lambda b, i: (b, 0, 0)),
            pl.BlockSpec((1, LANES, n_ctx), lambda b, i: (b, 0, 0)),
            pl.BlockSpec((seq, LANES), lambda b, i: (b, K_A // LANES)),
            pl.BlockSpec((LANES, seq), lambda b, i: (T_VA // LANES, b)),
        ],
        out_specs=pl.BlockSpec((tq, GQA_COLS), lambda b, i: (b * nq + i, 0)),
        out_shape=jax.ShapeDtypeStruct((t, GQA_COLS), BF16),
        scratch_shapes=_keys_major_scratch(n, tk),
        compiler_params=_cparams(("arbitrary", "arbitrary")),
        name="attn_a",
    )(zt, k_ctx, vt_ctx, z, zt)


def _attn_b_kernel(sink_ref, qt_ref, kc_ref, vct_ref, k_ref, vt_ref, o_ref):
    tq = qt_ref.shape[1]
    seq = k_ref.shape[0]
    n = B_HEADS * tq
    band = tq + 2 * WINDOW
    i = pl.program_id(1)
    lo = lax.broadcasted_iota(jnp.int32, (LANES, tq), 0) < HEAD_DIM
    q = _gqa_queries_t(qt_ref, lo)
    start = pl.multiple_of(jnp.clip(i * tq - WINDOW, 0, seq - band), LANES)
    s_c = jnp.dot(kc_ref[0], q, preferred_element_type=F32)
    s_b = jnp.dot(k_ref[pl.ds(start, band), :], q, preferred_element_type=F32)
    qpos = i * tq + lax.broadcasted_iota(jnp.int32, (band, tq), 1)
    kpos = start + lax.broadcasted_iota(jnp.int32, (band, tq), 0)
    inside = jnp.abs(kpos - qpos) <= WINDOW
    s_b = jnp.concatenate([jnp.where(inside, s_b[:, h * tq:(h + 1) * tq], NEG) for h in range(B_HEADS)], axis=1)
    sink = jnp.concatenate([jnp.full((1, tq), sink_ref[h] * LOG2E, F32) for h in range(B_HEADS)], axis=1)
    m = jnp.maximum(sink, jnp.maximum(jnp.max(s_c, axis=0, keepdims=True), jnp.max(s_b, axis=0, keepdims=True)))
    acc = (_pv_and_sum(vct_ref[0], jnp.exp2(s_c - m).astype(BF16))
           + _pv_and_sum(vt_ref[:, pl.ds(start, band)], jnp.exp2(s_b - m).astype(BF16)))
    l = jnp.exp2(sink - m) + acc[LANES:LANES + 1]
    _gqa_store_t(acc[0:LANES] * (1.0 / l), o_ref, tq, lo)


def _attn_b(z, zt, k_ctx, vt_ctx, sink, *, seq, tq):
    t = z.shape[0]
    nq = seq // tq
    n_ctx = k_ctx.shape[1]
    return pl.pallas_call(
        _attn_b_kernel,
        grid=(t // seq, nq),
        in_specs=[
            pl.BlockSpec(memory_space=pltpu.SMEM),
            pl.BlockSpec((GQA_COLS, tq), lambda b, i: (T_QB // GQA_COLS, b * nq + i)),
            pl.BlockSpec((1, n_ctx, LANES), lambda b, i: (b, 0, 0)),
            pl.BlockSpec((1, LANES, n_ctx), lambda b, i: (b, 0, 0)),
            pl.BlockSpec((seq, LANES), lambda b, i: (b, K_B // LANES)),
            pl.BlockSpec((LANES, seq), lambda b, i: (T_VB // LANES, b)),
        ],
        out_specs=pl.BlockSpec((tq, GQA_COLS), lambda b, i: (b * nq + i, 0)),
        out_shape=jax.ShapeDtypeStruct((t, GQA_COLS), BF16),
        compiler_params=_cparams(("arbitrary", "arbitrary")),
        name="attn_b",
    )(sink, zt, k_ctx, vt_ctx, z, zt)


def _attn_c_kernel(lam_ref, gsub_ref, qt_ref, kc_ref, vct_ref, k_ref, vt_ref, o_ref,
                   q_scr, m_ref, acc_ref, s0, s1, x0, x1, p0, p1, a0, a1, *, tk, lam_init):
    tq = qt_ref.shape[1]
    row = lax.broadcasted_iota(jnp.int32, (LANES, tq), 0)
    refs = (m_ref, acc_ref)
    q_scr[...] = _diff_queries_t(qt_ref[...], row)
    _softmax_init_t(*refs)
    _attend_pipelined_t(q_scr, kc_ref, vct_ref, k_ref, vt_ref, tk, (s0, s1), (x0, x1), (p0, p1), (a0, a1), *refs)
    lam = _diff_lambda(lam_ref, lam_init)
    o_ref[...] = _diff_output_t(_softmax_result_t(acc_ref), tq, lam, gsub_ref[...], lam_init, row)


def _attn_c(z, zt, k_ctx, vt_ctx, lam_par, gsub, *, seq, tq, tk, lam_init):
    t = z.shape[0]
    nq = seq // tq
    n_ctx = k_ctx.shape[1]
    n = 4 * tq
    return pl.pallas_call(
        functools.partial(_attn_c_kernel, tk=tk, lam_init=lam_init),
        grid=(t // seq, 2, nq),
        in_specs=[
            _const_spec((8, LANES)),
            _const_spec((1, LANES)),
            pl.BlockSpec((LANES, tq), lambda b, s, i: (T_QC // LANES + s, b * nq + i)),
            pl.BlockSpec((1, n_ctx, LANES), lambda b, s, i: (b, 0, s)),
            pl.BlockSpec((1, LANES, n_ctx), lambda b, s, i: (b, s, 0)),
            pl.BlockSpec((seq, LANES), lambda b, s, i: (b, K_C // LANES + s)),
            pl.BlockSpec((LANES, seq), lambda b, s, i: (T_VC // LANES + s, b)),
        ],
        out_specs=pl.BlockSpec((tq, LANES), lambda b, s, i: (b * nq + i, s)),
        out_shape=jax.ShapeDtypeStruct((t, C_HEADS * C_V_DIM), BF16),
        scratch_shapes=_keys_major_scratch(n, tk),
        compiler_params=_cparams(("arbitrary", "arbitrary", "arbitrary")),
        name="attn_c",
    )(lam_par, gsub, zt, k_ctx, vt_ctx, z, zt)


def _post_kernel(*refs, n_parts, halo, tiles_per_seq, final):
    it = iter(refs)
    tiles = [next(it) for _ in range(1 + n_parts)]
    halos = [(next(it), next(it)) for _ in range(1 + n_parts)] if halo else None
    (mod_ref, gn_ref, wo_ref, wu_ref, cw_ref, cb_ref, wd_ref, gf_ref, out_ref, act_scr) = (next(it) for _ in range(10))

    def rows_of(j):
        if halo:
            return jnp.concatenate([halos[j][0][...], tiles[j][...], halos[j][1][...]], axis=0)
        return tiles[j][...]

    tm = tiles[0].shape[0]
    x = rows_of(0)
    o = jnp.concatenate([rows_of(j) for j in range(1, 1 + n_parts)], axis=1)
    ext = x.shape[0]
    mod = mod_ref[0]
    x1 = x + mod[2:3] * jnp.dot(o, wo_ref[0], preferred_element_type=F32)
    h = _rms(x1, gn_ref[...]) * (1.0 + mod[4:5]) + mod[3:4]
    row = lax.broadcasted_iota(jnp.int32, (ext, 1), 0)
    if halo:
        t_in_seq = pl.program_id(0) % tiles_per_seq
        keep = ((row >= halo) | (t_in_seq > 0)) & ((row < halo + tm) | (t_in_seq < tiles_per_seq - 1))
        h = jnp.where(keep, h, 0.0)
    h = h.astype(BF16)

    def conv(u, c0):
        cw = cw_ref[:, c0:c0 + FFN_CHUNK]
        up = pltpu.roll(u, 1, 0)
        dn = pltpu.roll(u, ext - 1, 0)
        if not halo:
            up = jnp.where(row == 0, 0.0, up)
            dn = jnp.where(row == ext - 1, 0.0, dn)
        v = cw[0:1] * up + cw[1:2] * u + cw[2:3] * dn + cb_ref[:, c0:c0 + FFN_CHUNK]
        return v[halo:halo + tm]

    def up_proj(c):
        ca, cg = c * FFN_CHUNK, D_FF + c * FFN_CHUNK
        return (jnp.dot(h, wu_ref[0, :, ca:ca + FFN_CHUNK], preferred_element_type=F32),
                jnp.dot(h, wu_ref[0, :, cg:cg + FFN_CHUNK], preferred_element_type=F32))

    n_chunks = D_FF // FFN_CHUNK
    nxt = up_proj(0)
    for c in range(n_chunks):
        ua, ug = nxt
        if c + 1 < n_chunks:
            nxt = up_proj(c + 1)
        ca = c * FFN_CHUNK
        a = conv(ua, ca)
        g = conv(ug, D_FF + ca)
        act_scr[:, ca:ca + FFN_CHUNK] = (a / (1.0 + jnp.exp(-a)) * g).astype(BF16)
    x2 = x1[halo:halo + tm] + mod[5:6] * jnp.dot(act_scr[...], wd_ref[0], preferred_element_type=F32)
    if final:
        x2 = _rms(x2, gf_ref[...])
    out_ref[...] = x2


def _post(x, o_parts, mod_l, gn2, wo, wu, cw, cb, wd, gf, *, layer, tm, tiles_per_cond, tiles_per_seq):
    t = x.shape[0]
    final = layer == DEPTH - 1
    assert sum(o.shape[1] for o in o_parts) == D_MODEL
    halo = HALO if tiles_per_seq > 1 else 0
    tile = lambda i: (i, 0)
    rows = [x] + list(o_parts)
    in_specs = [pl.BlockSpec((tm, a.shape[1]), tile) for a in rows]
    args = list(rows)
    if halo:
        per = tm // halo
        prev = lambda i: (jnp.maximum(i * per - 1, 0), 0)
        nxt = lambda i: (jnp.minimum((i + 1) * per, t // halo - 1), 0)
        for a in rows:
            in_specs += [pl.BlockSpec((halo, a.shape[1]), prev), pl.BlockSpec((halo, a.shape[1]), nxt)]
            args += [a, a]
    in_specs += [
        pl.BlockSpec((1, 6, D_MODEL), lambda i: (i // tiles_per_cond, 0, 0)),
        _const_spec((1, D_MODEL)),
        _layer_spec((D_MODEL, D_MODEL), layer),
        _layer_spec((D_MODEL, 2 * D_FF), layer),
        _const_spec((3, 2 * D_FF)),
        _const_spec((1, 2 * D_FF)),
        _layer_spec((D_FF, D_MODEL), layer),
        _const_spec((1, D_MODEL)),
    ]
    args += [mod_l, gn2, wo, wu, cw, cb, wd, gf]
    return pl.pallas_call(
        functools.partial(_post_kernel, n_parts=len(o_parts), halo=halo, tiles_per_seq=tiles_per_seq, final=final),
        grid=(t // tm,),
        in_specs=in_specs,
        out_specs=pl.BlockSpec((tm, D_MODEL), tile),
        out_shape=jax.ShapeDtypeStruct((t, D_MODEL), F32),
        scratch_shapes=[pltpu.VMEM((tm, D_FF), BF16)],
        compiler_params=_cparams(("arbitrary",)),
        name="post",
    )(*args)


def _rope_tables(seq):
    t = np.arange(seq)
    rows = (t // GRID_W).astype(np.float32)[:, None]
    cols = (t % GRID_W).astype(np.float32)[:, None]

    def tab(half, reps):
        inv = np.float32(ROPE_THETA) ** (-np.arange(half, dtype=np.float32) / np.float32(half))
        ar, ac = rows * inv[None, :], cols * inv[None, :]
        cos = np.concatenate([np.cos(ar), np.cos(ar), np.cos(ac), np.cos(ac)], axis=-1)
        sin = np.concatenate([-np.sin(ar), np.sin(ar), -np.sin(ac), np.sin(ac)], axis=-1)
        return np.tile(cos, (1, reps)).astype(np.float32), np.tile(sin, (1, reps)).astype(np.float32)

    tabs = tab(HEAD_DIM // 4, LANES // HEAD_DIM) + tab(C_QK_DIM // 4, LANES // C_QK_DIM)
    tabs = tabs + tuple(np.ascontiguousarray(t.T) for t in tabs)
    return tuple(jnp.asarray(t) for t in tabs)


def _w_in_perms():
    sizes = (A_HEADS * HEAD_DIM, A_KV * HEAD_DIM, A_KV * HEAD_DIM, B_HEADS * HEAD_DIM, B_KV * HEAD_DIM,
             B_KV * HEAD_DIM, C_HEADS * 2 * C_QK_DIM, C_HEADS * 2 * C_QK_DIM, C_HEADS * C_V_DIM)
    offs = np.concatenate([[0], np.cumsum(sizes)])
    qa, ka, va, qb, kb, vb, qc, kc, vc = (np.arange(offs[j], offs[j + 1]) for j in range(9))
    pair = np.concatenate([np.arange(h * HEAD_DIM, (h + 1) * HEAD_DIM) for h in PAIRED_HEADS])
    return np.concatenate([qa[pair], va, qc, vc, vb, qb[pair]]), np.concatenate([ka, kb, kc])


def _w_out_perm():
    pair = np.concatenate([np.arange(h * HEAD_DIM, (h + 1) * HEAD_DIM) for h in PAIRED_HEADS])
    return np.concatenate([pair, GQA_COLS + pair, np.arange(MIX_C0, D_MODEL)])


def _take_runs(w, perm, axis):
    cuts = [0] + [j for j in range(1, len(perm)) if perm[j] != perm[j - 1] + 1] + [len(perm)]
    parts = [lax.slice_in_dim(w, int(perm[a]), int(perm[b - 1]) + 1, axis=axis) for a, b in zip(cuts[:-1], cuts[1:])]
    return jnp.concatenate(parts, axis=axis)


def kernel(x_prompt, x_sample, cache_a_k, cache_a_v, cache_b_k, cache_b_v, cache_c_k, cache_c_v, c, c_ctx, w_ada, b_ada, g_norm1, g_norm2, w_in, g_qa, g_ka, sink_b, lam_q1, lam_k1, lam_q2, lam_k2, g_subln, w_out, w_up, conv_w, conv_b, w_down, g_final):
    n_ctx_req, ctx_len, _ = x_prompt.shape
    n_lat_req, lat_len, _ = x_sample.shape
    past = cache_a_k.shape[2]

    conds = jnp.zeros((8, D_MODEL), F32).at[0].set(c_ctx).at[1:1 + n_lat_req].set(c)
    mod = _modulation(conds, w_ada, b_ada).reshape(DEPTH, 8, 6, D_MODEL)

    perm_t, perm_k = _w_in_perms()
    w_t = jnp.swapaxes(_take_runs(w_in, perm_t, 2), 1, 2).astype(BF16)
    w_k = _take_runs(w_in, perm_k, 2).astype(BF16)
    w_out_p = _take_runs(w_out, _w_out_perm(), 1).astype(BF16)
    w_up_b = w_up.astype(BF16)
    w_down_b = w_down.astype(BF16)
    rope_tabs = _rope_tables(lat_len)
    zpad = jnp.zeros((DEPTH, LANES - C_QK_DIM), F32)
    lam_rows = [jnp.concatenate([v, zpad], axis=-1) for v in (lam_q1, lam_k1, lam_q2, lam_k2)]
    lam_par = jnp.concatenate([jnp.stack(lam_rows, axis=1), jnp.zeros((DEPTH, 4, LANES), F32)], axis=1)
    gf = g_final.reshape(1, D_MODEL)

    xc = x_prompt.reshape(n_ctx_req * ctx_len, D_MODEL)
    xs = x_sample.reshape(n_lat_req * lat_len, D_MODEL)
    lat_tm = LATENT_TM
    caches = []
    for l in range(DEPTH):
        lam_init = 0.8 - 0.6 * math.exp(-0.3 * l)
        gn1 = g_norm1[l].reshape(1, D_MODEL)
        gn2 = g_norm2[l].reshape(1, D_MODEL)
        gq_col = jnp.tile(g_qa[l], LANES // HEAD_DIM)[:, None]
        gk = jnp.tile(g_ka[l], LANES // HEAD_DIM).reshape(1, LANES)
        gsub = jnp.tile(g_subln[l], LANES // C_V_DIM).reshape(1, LANES)
        post_w = (gn2, w_out_p, w_up_b, conv_w[l], conv_b[l].reshape(1, 2 * D_FF), w_down_b, gf)
        final = l == DEPTH - 1

        n_tiles = n_ctx_req
        zc, *caches, ztc = _in_proj(xc, mod[l, 0:1], gn1, w_t, w_k, jnp.broadcast_to(gq_col, (LANES, ctx_len)), gk,
                                    None, layer=l, tm=ctx_len, tiles_per_cond=n_tiles, tiles_per_seq=1,
                                    emit_cache="final" if final else "rows",
                                    prev_cache=caches[0] if final else None)
        oc = _attn_ctx(zc, ztc, sink_b[l], lam_par[l], gsub, seq=ctx_len, lam_init=lam_init)
        xc = _post(xc, [oc], mod[l, 0:1], *post_w, layer=l, tm=ctx_len, tiles_per_cond=n_tiles, tiles_per_seq=1)

        per_seq = lat_len // lat_tm
        zs, zts = _in_proj(xs, mod[l, 1:1 + n_lat_req], gn1, w_t, w_k, jnp.broadcast_to(gq_col, (LANES, lat_tm)), gk,
                           rope_tabs, layer=l, tm=lat_tm, tiles_per_cond=per_seq, tiles_per_seq=per_seq,
                           emit_cache=None)
        flat = lambda a: a[:, l].reshape(n_lat_req, past, -1).astype(BF16)
        flat_t = lambda a: jnp.swapaxes(flat(a), 1, 2)
        oa = _attn_a(zs, zts, flat(cache_a_k), flat_t(cache_a_v), seq=lat_len, tq=ATTN_A_TQ, tk=ATTN_TK)
        ob = _attn_b(zs, zts, flat(cache_b_k), flat_t(cache_b_v), sink_b[l], seq=lat_len, tq=ATTN_B_TQ)
        oc = _attn_c(zs, zts, flat(cache_c_k), flat_t(cache_c_v), lam_par[l], gsub, seq=lat_len, tq=ATTN_C_TQ,
                     tk=ATTN_TK, lam_init=lam_init)
        xs = _post(xs, [oa, ob, oc], mod[l, 1:1 + n_lat_req], *post_w, layer=l, tm=lat_tm, tiles_per_cond=per_seq,
                   tiles_per_seq=per_seq)

    heads = (A_KV, A_KV, B_KV, B_KV, C_HEADS, C_HEADS)
    new_caches = tuple(cch.reshape(n_ctx_req, DEPTH, ctx_len, h, -1) for cch, h in zip(caches, heads))
    return (xc.reshape(x_prompt.shape), xs.reshape(x_sample.shape)) + new_caches
```

```python
import functools
import math

import numpy as np
import jax
import jax.numpy as jnp
from jax import lax
from jax.experimental import pallas as pl
from jax.experimental.pallas import tpu as pltpu

D_MODEL = 1024
DEPTH = 2
GRID_W = 64
HEAD_DIM = 64
A_HEADS = 6
A_KV = 2
B_HEADS = 6
B_KV = 2
C_HEADS = 4
C_QK_DIM = 32
C_V_DIM = 2 * C_QK_DIM
WINDOW = 128
ROPE_THETA = 10000.0
D_FF = 2816
EPS = 1e-6
NEG = -1e30
LOG2E = math.log2(math.e)

LANES = 128
BF16_ROWS = 16
MXU_COLS = 256
VMEM_LIMIT = 56 * 1024 * 1024

GQA_COLS = A_HEADS * HEAD_DIM
MIX_C0 = 2 * GQA_COLS
Z_KA, Z_VA, Z_KB, Z_VB, Z_KC, Z_VC, CACHE_COLS = 0, 128, 256, 384, 512, 768, 1024
CACHE_PIECES = ((Z_KA, 128), (Z_VA, 128), (Z_KB, 128), (Z_VB, 128), (Z_KC, 256), (Z_VC, 256))
K_A, K_B, K_C, K_COLS = 0, 128, 256, 512
T_QA, T_VA, T_QC, T_VC, T_VB, T_QB, T_ROWS = 0, 384, 512, 768, 1024, 1152, 1536
PAIRED_HEADS = (0, 3, 1, 4, 2, 5)
FFN_CHUNK = MXU_COLS
HALO = BF16_ROWS
LATENT_TM = 512
ATTN_A_TQ, ATTN_B_TQ, ATTN_C_TQ, ATTN_TK = 512, 256, 512, 512
ATTN_B_TK = 256
MOD_COLS = 1536

F32 = jnp.float32
BF16 = jnp.bfloat16


def _cparams(sem):
    return pltpu.CompilerParams(dimension_semantics=sem, vmem_limit_bytes=VMEM_LIMIT)


def _const_spec(shape):
    nd = len(shape)
    return pl.BlockSpec(shape, lambda *_: (0,) * nd)


def _layer_spec(shape, layer):
    nd = len(shape)
    return pl.BlockSpec((1,) + tuple(shape), lambda *_: (layer,) + (0,) * nd)


def _mod_kernel(c_ref, w_ref, b_ref, o_ref):
    cond = c_ref[...]
    a = cond / (1.0 + jnp.exp(-cond))
    o_ref[0] = jnp.dot(a.astype(BF16), w_ref[0].astype(BF16), preferred_element_type=F32) + b_ref[0]


def _modulation(conds, w_ada, b_ada):
    nb = MOD_COLS
    n_out = w_ada.shape[-1]
    return pl.pallas_call(
        _mod_kernel,
        grid=(DEPTH, n_out // nb),
        in_specs=[
            pl.BlockSpec((8, D_MODEL), lambda l, j: (0, 0)),
            pl.BlockSpec((1, D_MODEL, nb), lambda l, j: (l, 0, j)),
            pl.BlockSpec((1, 1, nb), lambda l, j: (l, 0, j)),
        ],
        out_specs=pl.BlockSpec((1, 8, nb), lambda l, j: (l, 0, j)),
        out_shape=jax.ShapeDtypeStruct((DEPTH, 8, n_out), F32),
        compiler_params=_cparams(("arbitrary", "arbitrary")),
        name="modulation",
    )(conds, w_ada, b_ada.reshape(DEPTH, 1, n_out))


def _rms(x, g):
    ms = jnp.mean(x * x, axis=-1, keepdims=True)
    return x * lax.rsqrt(ms + EPS) * g


def _head_rms(x, g, lo):
    ss = x * x
    s_lo = jnp.sum(jnp.where(lo, ss, 0.0), axis=-1, keepdims=True)
    s_hi = jnp.sum(jnp.where(lo, 0.0, ss), axis=-1, keepdims=True)
    inv = jnp.where(lo, lax.rsqrt(s_lo * (1.0 / HEAD_DIM) + EPS), lax.rsqrt(s_hi * (1.0 / HEAD_DIM) + EPS))
    return x * inv * g


def _rope(x, cos, sin, chunk, first):
    sw = jnp.where(first, pltpu.roll(x, LANES - chunk, 1), pltpu.roll(x, chunk, 1))
    return x * cos + sw * sin


def _softmax_init_t(m_ref, acc_ref):
    m_ref[...] = jnp.full(m_ref.shape, NEG, F32)
    acc_ref[...] = jnp.zeros(acc_ref.shape, F32)


def _scores_t(k, q_scr, s_ref, mx_ref, cs, allowed):
    s = jnp.dot(k, q_scr[:, cs], preferred_element_type=F32)
    if allowed is not None:
        s = jnp.where(allowed(cs), s, NEG)
    s_ref[0:k.shape[0], cs] = s
    mx_ref[:, cs] = jnp.max(s, axis=0, keepdims=True)


def _pv_and_sum(vt, p):
    ones = jnp.ones((BF16_ROWS, vt.shape[1]), BF16)
    return jnp.dot(jnp.concatenate([vt, ones], axis=0), p, preferred_element_type=F32)


def _softmax_result_t(acc_ref):
    return acc_ref[0:LANES] * (1.0 / acc_ref[LANES:LANES + 1])


def _attend_blocks_t(q_scr, blocks, s_bufs, x_bufs, m_ref, acc_ref):
    n = q_scr.shape[1]
    groups = [slice(c, c + MXU_COLS) for c in range(0, n, MXU_COLS)]

    def step(par, cur, nxt):
        vt = cur[1]() if cur else None
        k = nxt[0]() if nxt else None
        for cs in groups:
            if cur:
                s = s_bufs[par][0:vt.shape[1], cs]
                m_prev = m_ref[:, cs]
                m_new = jnp.maximum(m_prev, x_bufs[par][:, cs])
                m_ref[:, cs] = m_new
                pv = _pv_and_sum(vt, jnp.exp2(s - m_new).astype(BF16))
                acc_ref[:, cs] = jnp.exp2(m_prev - m_new) * acc_ref[:, cs] + pv
            if nxt:
                _scores_t(k, q_scr, s_bufs[1 - par], x_bufs[1 - par], cs, nxt[2])

    step(1, None, blocks[0])
    for t, blk in enumerate(blocks):
        step(t % 2, blk, blocks[t + 1] if t + 1 < len(blocks) else None)


def _ctx_then_latent_blocks(kc_ref, vct_ref, k_ref, vt_ref, tk):
    blocks = [(lambda: kc_ref[0], lambda: vct_ref[0], None)]
    for j in range(k_ref.shape[0] // tk):
        blocks.append((lambda j=j: k_ref[j * tk:(j + 1) * tk, :], lambda j=j: vt_ref[:, j * tk:(j + 1) * tk], None))
    return blocks


def _diff_lambda(lam_ref, lam_init):
    f = lambda a, b: jnp.exp(jnp.sum(a * b, axis=-1, keepdims=True))
    return f(lam_ref[0:1], lam_ref[1:2]) - f(lam_ref[2:3], lam_ref[3:4]) + lam_init


def _swap_row_chunks(x, chunk):
    parts = []
    for r in range(0, x.shape[0], 2 * chunk):
        parts += [x[r + chunk:r + 2 * chunk], x[r:r + chunk]]
    return jnp.concatenate(parts, axis=0)


def _in_proj_kernel(*refs, use_rope, emit_cache):
    it = iter(refs)
    x_ref, mod_ref, gn_ref, wt_ref, wk_ref, gq_ref, gk_ref = (next(it) for _ in range(7))
    cos64 = sin64 = cos32 = sin32 = cos64t = sin64t = cos32t = sin32t = None
    if use_rope:
        cos64, sin64, cos32, sin32, cos64t, sin64t, cos32t, sin32t = (next(it)[...] for _ in range(8))
    prev_ref = next(it) if emit_cache == "final" else None
    zk_ref = next(it)
    cache_ref = next(it) if emit_cache == "rows" else None
    final_refs = [next(it) for _ in CACHE_PIECES] if emit_cache == "final" else None
    zt_ref = next(it)
    if final_refs:
        for ref, (start, width) in zip(final_refs, CACHE_PIECES):
            ref[0, 0] = prev_ref[:, start:start + width]

    tm = x_ref.shape[0]
    sub = min(tm, MXU_COLS)
    mod = mod_ref[0]
    lane = lax.broadcasted_iota(jnp.int32, (sub, LANES), 1)
    lo = lane < HEAD_DIM
    first16 = (lane & 31) < 16
    first8 = (lane & 15) < 8
    q_scale = HEAD_DIM ** -0.5 * LOG2E
    qc_scale = C_QK_DIM ** -0.5 * LOG2E

    def matmuls(r0):
        h = (_rms(x_ref[r0:r0 + sub], gn_ref[...]) * (1.0 + mod[1:2]) + mod[0:1]).astype(BF16)
        zk = jnp.dot(h, wk_ref[0], preferred_element_type=F32)
        zt = lax.dot_general(wt_ref[0], h, (((1,), (1,)), ((), ())), preferred_element_type=F32)
        return zk, zt

    def put_cache(r0, off, v):
        if cache_ref is not None:
            cache_ref[r0:r0 + sub, off:off + LANES] = v
        if final_refs:
            for ref, (start, width) in zip(final_refs, CACHE_PIECES):
                if start <= off < start + width:
                    ref[0, DEPTH - 1, r0:r0 + sub, off - start:off - start + LANES] = v

    def finish_keys(r0, zk):
        def tab(t):
            return t[r0:r0 + sub] if use_rope else None

        def rope64(v):
            return _rope(v, tab(cos64), tab(sin64), 16, first16) if use_rope else v

        def rope32(v):
            return _rope(v, tab(cos32), tab(sin32), 8, first8) if use_rope else v

        def put_k(off, v):
            zk_ref[r0:r0 + sub, off:off + LANES] = v.astype(BF16)

        ka = _head_rms(zk[:, K_A:K_A + LANES], gk_ref[...], lo)
        put_cache(r0, Z_KA, ka)
        put_k(K_A, rope64(ka))
        kb = zk[:, K_B:K_B + LANES]
        put_cache(r0, Z_KB, kb)
        put_k(K_B, rope64(kb))
        for s in range(2):
            kc = zk[:, K_C + LANES * s:K_C + LANES * (s + 1)]
            put_cache(r0, Z_KC + LANES * s, kc)
            put_k(K_C + LANES * s, rope32(kc))

    def finish_features(r0, zt):
        def t_slab(off):
            return zt[off:off + LANES]

        def put_t(off, v):
            zt_ref[off:off + LANES, r0:r0 + sub] = v.astype(BF16)

        def rope_t(v, cos_t, sin_t, chunk):
            if not use_rope:
                return v
            return v * cos_t[:, r0:r0 + sub] + _swap_row_chunks(v, chunk) * sin_t[:, r0:r0 + sub]

        def head_rms_t(v):
            ss = v * v
            halves = []
            for r in (0, HEAD_DIM):
                inv = lax.rsqrt(jnp.sum(ss[r:r + HEAD_DIM], axis=0, keepdims=True) * (1.0 / HEAD_DIM) + EPS)
                halves.append(v[r:r + HEAD_DIM] * inv)
            return jnp.concatenate(halves, axis=0) * gq_ref[:, 0:sub]

        for s in range(3):
            put_t(T_QA + LANES * s, rope_t(head_rms_t(t_slab(T_QA + LANES * s)), cos64t, sin64t, 16) * q_scale)
            put_t(T_QB + LANES * s, rope_t(t_slab(T_QB + LANES * s), cos64t, sin64t, 16) * q_scale)
        for s in range(2):
            put_t(T_QC + LANES * s, rope_t(t_slab(T_QC + LANES * s), cos32t, sin32t, 8) * qc_scale)
        for t_off, z_off in ((T_VA, Z_VA), (T_VB, Z_VB), (T_VC, Z_VC), (T_VC + LANES, Z_VC + LANES)):
            put_t(t_off, t_slab(t_off))
            if emit_cache:
                put_cache(r0, z_off, t_slab(t_off).T)

    starts = list(range(0, tm, sub))
    prods = [matmuls(r0) for r0 in starts]
    for r0, (zk, _) in zip(starts, prods):
        finish_keys(r0, zk)
    for r0, (_, zt) in zip(starts, prods):
        finish_features(r0, zt)


def _in_proj(x, mod_l, gn, w_t, w_k, gq_t, gk, rope_tabs, *, layer, tm, tiles_per_cond, tiles_per_seq, emit_cache,
             prev_cache=None):
    t = x.shape[0]
    use_rope = rope_tabs is not None
    assert (emit_cache == "final") == (prev_cache is not None) and DEPTH == 2
    in_specs = [
        pl.BlockSpec((tm, D_MODEL), lambda i: (i, 0)),
        pl.BlockSpec((1, 6, D_MODEL), lambda i: (i // tiles_per_cond, 0, 0)),
        _const_spec((1, D_MODEL)),
        _layer_spec((T_ROWS, D_MODEL), layer),
        _layer_spec((D_MODEL, K_COLS), layer),
        _const_spec((LANES, tm)),
        _const_spec((1, LANES)),
    ]
    args = [x, mod_l, gn, w_t, w_k, gq_t, gk]
    if use_rope:
        in_specs += [pl.BlockSpec((tm, LANES), lambda i: (i % tiles_per_seq, 0))] * 4
        in_specs += [pl.BlockSpec((LANES, tm), lambda i: (0, i % tiles_per_seq))] * 4
        args += list(rope_tabs)
    if prev_cache is not None:
        in_specs.append(pl.BlockSpec((tm, CACHE_COLS), lambda i: (i, 0)))
        args.append(prev_cache)
    out_shape = [jax.ShapeDtypeStruct((t, K_COLS), BF16)]
    out_specs = [pl.BlockSpec((tm, K_COLS), lambda i: (i, 0))]
    if emit_cache == "rows":
        out_shape.append(jax.ShapeDtypeStruct((t, CACHE_COLS), F32))
        out_specs.append(pl.BlockSpec((tm, CACHE_COLS), lambda i: (i, 0)))
    if emit_cache == "final":
        for _, width in CACHE_PIECES:
            out_shape.append(jax.ShapeDtypeStruct((t // tm, DEPTH, tm, width), F32))
            out_specs.append(pl.BlockSpec((1, DEPTH, tm, width), lambda i: (i, 0, 0, 0)))
    out_shape.append(jax.ShapeDtypeStruct((T_ROWS, t), BF16))
    out_specs.append(pl.BlockSpec((T_ROWS, tm), lambda i: (0, i)))
    return pl.pallas_call(
        functools.partial(_in_proj_kernel, use_rope=use_rope, emit_cache=emit_cache),
        grid=(t // tm,),
        in_specs=in_specs,
        out_specs=out_specs,
        out_shape=out_shape,
        compiler_params=_cparams(("arbitrary",)),
        name="in_proj",
    )(*args)


def _gqa_queries_t(qt_ref, lo):
    slabs = [qt_ref[LANES * s:LANES * (s + 1), :] for s in range(3)]
    zero = jnp.zeros_like(slabs[0])
    return jnp.concatenate([jnp.where(lo, s, zero) for s in slabs] + [jnp.where(lo, zero, s) for s in slabs], axis=1)


def _gqa_store_t(ot, o_ref, tq, lo, col0=0):
    for s in range(3):
        slab_t = jnp.where(lo, ot[:, s * tq:(s + 1) * tq], ot[:, (3 + s) * tq:(4 + s) * tq])
        o_ref[:, col0 + LANES * s:col0 + LANES * (s + 1)] = slab_t.T.astype(BF16)


def _diff_queries_t(qt, row):
    zero = jnp.zeros_like(qt)
    return jnp.concatenate([jnp.where((row >= C_QK_DIM * j) & (row < C_QK_DIM * (j + 1)), qt, zero)
                            for j in range(4)], axis=1)


def _diff_output_t(ot, tq, lam, gsub, lam_init, row):
    o_even = ot[:, 0:tq] - lam * ot[:, tq:2 * tq]
    o_odd = ot[:, 2 * tq:3 * tq] - lam * ot[:, 3 * tq:4 * tq]
    oc = jnp.where(row < C_V_DIM, o_even, o_odd).T
    lo = lax.broadcasted_iota(jnp.int32, (tq, LANES), 1) < C_V_DIM
    return (_head_rms(oc, gsub, lo) * (1.0 - lam_init)).astype(BF16)


def _attn_ctx_kernel(sink_ref, lam_ref, gsub_ref, z_ref, zt_ref, o_ref, *, lam_init):
    tq = z_ref.shape[0]
    row = lax.broadcasted_iota(jnp.int32, (LANES, tq), 0)
    lo = row < HEAD_DIM

    def attend(q, k, vt, sink=None):
        s = jnp.dot(k, q, preferred_element_type=F32)
        m = jnp.max(s, axis=0, keepdims=True)
        if sink is not None:
            m = jnp.maximum(m, sink)
        acc = _pv_and_sum(vt, jnp.exp2(s - m).astype(BF16))
        l = acc[LANES:LANES + 1]
        if sink is not None:
            l = l + jnp.exp2(sink - m)
        return acc[0:LANES] * (1.0 / l)

    def keys(off):
        return z_ref[:, off:off + LANES]

    def feat(off):
        return zt_ref[off:off + LANES, :]

    _gqa_store_t(attend(_gqa_queries_t(zt_ref.at[T_QA:T_QA + GQA_COLS], lo), keys(K_A), feat(T_VA)), o_ref, tq, lo)
    sink = jnp.concatenate([jnp.full((1, tq), sink_ref[h] * LOG2E, F32) for h in range(B_HEADS)], axis=1)
    _gqa_store_t(attend(_gqa_queries_t(zt_ref.at[T_QB:T_QB + GQA_COLS], lo), keys(K_B), feat(T_VB), sink),
                 o_ref, tq, lo, col0=GQA_COLS)
    lam = _diff_lambda(lam_ref, lam_init)
    for s in range(2):
        ot = attend(_diff_queries_t(feat(T_QC + LANES * s), row), keys(K_C + LANES * s), feat(T_VC + LANES * s))
        c0 = MIX_C0 + LANES * s
        o_ref[:, c0:c0 + LANES] = _diff_output_t(ot, tq, lam, gsub_ref[...], lam_init, row)


def _attn_ctx(z, zt, sink, lam_par, gsub, *, seq, lam_init):
    t = z.shape[0]
    return pl.pallas_call(
        functools.partial(_attn_ctx_kernel, lam_init=lam_init),
        grid=(t // seq,),
        in_specs=[
            pl.BlockSpec(memory_space=pltpu.SMEM),
            _const_spec((8, LANES)),
            _const_spec((1, LANES)),
            pl.BlockSpec((seq, K_COLS), lambda b: (b, 0)),
            pl.BlockSpec((T_ROWS, seq), lambda b: (0, b)),
        ],
        out_specs=pl.BlockSpec((seq, D_MODEL), lambda b: (b, 0)),
        out_shape=jax.ShapeDtypeStruct((t, D_MODEL), BF16),
        compiler_params=_cparams(("arbitrary",)),
        name="attn_ctx",
    )(sink, lam_par, gsub, z, zt)


def _attn_a_kernel(qt_ref, kc_ref, vct_ref, k_ref, vt_ref, o_ref, q_scr, m_ref, acc_ref,
                   s0, s1, x0, x1, *, tk):
    tq = qt_ref.shape[1]
    lo = lax.broadcasted_iota(jnp.int32, (LANES, tq), 0) < HEAD_DIM
    refs = (m_ref, acc_ref)
    q_scr[...] = _gqa_queries_t(qt_ref, lo)
    _softmax_init_t(*refs)
    _attend_blocks_t(q_scr, _ctx_then_latent_blocks(kc_ref, vct_ref, k_ref, vt_ref, tk), (s0, s1), (x0, x1), *refs)
    _gqa_store_t(_softmax_result_t(acc_ref), o_ref, tq, lo)


def _keys_major_scratch(n, tk):
    return [pltpu.VMEM((LANES, n), BF16), pltpu.VMEM((1, n), F32),
            pltpu.VMEM((LANES + BF16_ROWS, n), F32),
            pltpu.VMEM((tk, n), F32), pltpu.VMEM((tk, n), F32),
            pltpu.VMEM((1, n), F32), pltpu.VMEM((1, n), F32)]


def _attn_a(z, zt, k_ctx, vt_ctx, *, seq, tq, tk):
    t = z.shape[0]
    nq = seq // tq
    n_ctx = k_ctx.shape[1]
    n = A_HEADS * tq
    return pl.pallas_call(
        functools.partial(_attn_a_kernel, tk=tk),
        grid=(t // seq, nq),
        in_specs=[
            pl.BlockSpec((GQA_COLS, tq), lambda b, i: (T_QA // GQA_COLS, b * nq + i)),
            pl.BlockSpec((1, n_ctx, LANES), lambda b, i: (b, 0, 0)),
            pl.BlockSpec((1, LANES, n_ctx), lambda b, i: (b, 0, 0)),
            pl.BlockSpec((seq, LANES), lambda b, i: (b, K_A // LANES)),
            pl.BlockSpec((LANES, seq), lambda b, i: (T_VA // LANES, b)),
        ],
        out_specs=pl.BlockSpec((tq, GQA_COLS), lambda b, i: (b * nq + i, 0)),
        out_shape=jax.ShapeDtypeStruct((t, GQA_COLS), BF16),
        scratch_shapes=_keys_major_scratch(n, tk),
        compiler_params=_cparams(("arbitrary", "arbitrary")),
        name="attn_a",
    )(zt, k_ctx, vt_ctx, z, zt)


def _attn_b_kernel(sink_ref, qt_ref, kc_ref, vct_ref, k_ref, vt_ref, o_ref, q_scr, m_ref, acc_ref, s0, s1, x0, x1,
                   *, tk):
    tq = qt_ref.shape[1]
    seq = k_ref.shape[0]
    n = B_HEADS * tq
    band = tq + 2 * WINDOW
    i = pl.program_id(1)
    lo = lax.broadcasted_iota(jnp.int32, (LANES, tq), 0) < HEAD_DIM
    q_scr[...] = _gqa_queries_t(qt_ref, lo)
    m_ref[...] = jnp.concatenate([jnp.full((1, tq), sink_ref[h] * LOG2E, F32) for h in range(B_HEADS)], axis=1)
    acc_ref[...] = jnp.zeros(acc_ref.shape, F32)
    acc_ref[LANES:LANES + 1] = jnp.ones((1, n), F32)
    start = jnp.clip(i * tq - WINDOW, 0, seq - band)

    masks = {}

    def allowed(j, cs):
        q_off = cs.start % tq
        if (j, q_off) not in masks:
            kpos = start + j * tk + lax.broadcasted_iota(jnp.int32, (tk, MXU_COLS), 0)
            qpos = i * tq + q_off + lax.broadcasted_iota(jnp.int32, (tk, MXU_COLS), 1)
            masks[(j, q_off)] = jnp.abs(kpos - qpos) <= WINDOW
        return masks[(j, q_off)]

    def band_rows(j):
        return pl.ds(pl.multiple_of(start + j * tk, LANES), tk)

    blocks = [(lambda: kc_ref[0], lambda: vct_ref[0], None)]
    for j in range(band // tk):
        blocks.append((lambda j=j: k_ref[band_rows(j), :], lambda j=j: vt_ref[:, band_rows(j)],
                       functools.partial(allowed, j)))
    _attend_blocks_t(q_scr, blocks, (s0, s1), (x0, x1), m_ref, acc_ref)
    _gqa_store_t(_softmax_result_t(acc_ref), o_ref, tq, lo)


def _attn_b(z, zt, k_ctx, vt_ctx, sink, *, seq, tq, tk):
    t = z.shape[0]
    nq = seq // tq
    n_ctx = k_ctx.shape[1]
    assert (tq + 2 * WINDOW) % tk == 0 and MXU_COLS <= tq and n_ctx <= tk
    return pl.pallas_call(
        functools.partial(_attn_b_kernel, tk=tk),
        grid=(t // seq, nq),
        in_specs=[
            pl.BlockSpec(memory_space=pltpu.SMEM),
            pl.BlockSpec((GQA_COLS, tq), lambda b, i: (T_QB // GQA_COLS, b * nq + i)),
            pl.BlockSpec((1, n_ctx, LANES), lambda b, i: (b, 0, 0)),
            pl.BlockSpec((1, LANES, n_ctx), lambda b, i: (b, 0, 0)),
            pl.BlockSpec((seq, LANES), lambda b, i: (b, K_B // LANES)),
            pl.BlockSpec((LANES, seq), lambda b, i: (T_VB // LANES, b)),
        ],
        out_specs=pl.BlockSpec((tq, GQA_COLS), lambda b, i: (b * nq + i, 0)),
        out_shape=jax.ShapeDtypeStruct((t, GQA_COLS), BF16),
        scratch_shapes=_keys_major_scratch(B_HEADS * tq, tk),
        compiler_params=_cparams(("arbitrary", "arbitrary")),
        name="attn_b",
    )(sink, zt, k_ctx, vt_ctx, z, zt)


def _attn_c_kernel(lam_ref, gsub_ref, qt_ref, kc_ref, vct_ref, k_ref, vt_ref, o_ref,
                   q_scr, m_ref, acc_ref, s0, s1, x0, x1, *, tk, lam_init):
    tq = qt_ref.shape[1]
    row = lax.broadcasted_iota(jnp.int32, (LANES, tq), 0)
    refs = (m_ref, acc_ref)
    q_scr[...] = _diff_queries_t(qt_ref[...], row)
    _softmax_init_t(*refs)
    _attend_blocks_t(q_scr, _ctx_then_latent_blocks(kc_ref, vct_ref, k_ref, vt_ref, tk), (s0, s1), (x0, x1), *refs)
    lam = _diff_lambda(lam_ref, lam_init)
    o_ref[...] = _diff_output_t(_softmax_result_t(acc_ref), tq, lam, gsub_ref[...], lam_init, row)


def _attn_c(z, zt, k_ctx, vt_ctx, lam_par, gsub, *, seq, tq, tk, lam_init):
    t = z.shape[0]
    nq = seq // tq
    n_ctx = k_ctx.shape[1]
    n = 4 * tq
    return pl.pallas_call(
        functools.partial(_attn_c_kernel, tk=tk, lam_init=lam_init),
        grid=(t // seq, 2, nq),
        in_specs=[
            _const_spec((8, LANES)),
            _const_spec((1, LANES)),
            pl.BlockSpec((LANES, tq), lambda b, s, i: (T_QC // LANES + s, b * nq + i)),
            pl.BlockSpec((1, n_ctx, LANES), lambda b, s, i: (b, 0, s)),
            pl.BlockSpec((1, LANES, n_ctx), lambda b, s, i: (b, s, 0)),
            pl.BlockSpec((seq, LANES), lambda b, s, i: (b, K_C // LANES + s)),
            pl.BlockSpec((LANES, seq), lambda b, s, i: (T_VC // LANES + s, b)),
        ],
        out_specs=pl.BlockSpec((tq, LANES), lambda b, s, i: (b * nq + i, s)),
        out_shape=jax.ShapeDtypeStruct((t, C_HEADS * C_V_DIM), BF16),
        scratch_shapes=_keys_major_scratch(n, tk),
        compiler_params=_cparams(("arbitrary", "arbitrary", "arbitrary")),
        name="attn_c",
    )(lam_par, gsub, zt, k_ctx, vt_ctx, z, zt)


def _post_kernel(*refs, n_parts, halo, tiles_per_seq, final):
    it = iter(refs)
    tiles = [next(it) for _ in range(1 + n_parts)]
    halos = [(next(it), next(it)) for _ in range(1 + n_parts)] if halo else None
    (mod_ref, gn_ref, wo_ref, wu_ref, cw_ref, cb_ref, wd_ref, gf_ref, out_ref, act_scr) = (next(it) for _ in range(10))

    def rows_of(j):
        if halo:
            return jnp.concatenate([halos[j][0][...], tiles[j][...], halos[j][1][...]], axis=0)
        return tiles[j][...]

    tm = tiles[0].shape[0]
    x = rows_of(0)
    o = jnp.concatenate([rows_of(j) for j in range(1, 1 + n_parts)], axis=1)
    ext = x.shape[0]
    mod = mod_ref[0]
    x1 = x + mod[2:3] * jnp.dot(o, wo_ref[0], preferred_element_type=F32)
    h = _rms(x1, gn_ref[...]) * (1.0 + mod[4:5]) + mod[3:4]
    row = lax.broadcasted_iota(jnp.int32, (ext, 1), 0)
    if halo:
        t_in_seq = pl.program_id(0) % tiles_per_seq
        keep = ((row >= halo) | (t_in_seq > 0)) & ((row < halo + tm) | (t_in_seq < tiles_per_seq - 1))
        h = jnp.where(keep, h, 0.0)
    h = h.astype(BF16)

    def conv(u, c0):
        cw = cw_ref[:, c0:c0 + FFN_CHUNK]
        up = pltpu.roll(u, 1, 0)
        dn = pltpu.roll(u, ext - 1, 0)
        if not halo:
            up = jnp.where(row == 0, 0.0, up)
            dn = jnp.where(row == ext - 1, 0.0, dn)
        v = cw[0:1] * up + cw[1:2] * u + cw[2:3] * dn + cb_ref[:, c0:c0 + FFN_CHUNK]
        return v[halo:halo + tm]

    def up_proj(c):
        ca, cg = c * FFN_CHUNK, D_FF + c * FFN_CHUNK
        return (jnp.dot(h, wu_ref[0, :, ca:ca + FFN_CHUNK], preferred_element_type=F32),
                jnp.dot(h, wu_ref[0, :, cg:cg + FFN_CHUNK], preferred_element_type=F32))

    n_chunks = D_FF // FFN_CHUNK
    nxt = up_proj(0)
    for c in range(n_chunks):
        ua, ug = nxt
        if c + 1 < n_chunks:
            nxt = up_proj(c + 1)
        ca = c * FFN_CHUNK
        a = conv(ua, ca)
        g = conv(ug, D_FF + ca)
        act_scr[:, ca:ca + FFN_CHUNK] = (a / (1.0 + jnp.exp(-a)) * g).astype(BF16)
    x2 = x1[halo:halo + tm] + mod[5:6] * jnp.dot(act_scr[...], wd_ref[0], preferred_element_type=F32)
    if final:
        x2 = _rms(x2, gf_ref[...])
    out_ref[...] = x2


def _post(x, o_parts, mod_l, gn2, wo, wu, cw, cb, wd, gf, *, layer, tm, tiles_per_cond, tiles_per_seq):
    t = x.shape[0]
    final = layer == DEPTH - 1
    assert sum(o.shape[1] for o in o_parts) == D_MODEL
    halo = HALO if tiles_per_seq > 1 else 0
    tile = lambda i: (i, 0)
    rows = [x] + list(o_parts)
    in_specs = [pl.BlockSpec((tm, a.shape[1]), tile) for a in rows]
    args = list(rows)
    if halo:
        per = tm // halo
        prev = lambda i: (jnp.maximum(i * per - 1, 0), 0)
        nxt = lambda i: (jnp.minimum((i + 1) * per, t // halo - 1), 0)
        for a in rows:
            in_specs += [pl.BlockSpec((halo, a.shape[1]), prev), pl.BlockSpec((halo, a.shape[1]), nxt)]
            args += [a, a]
    in_specs += [
        pl.BlockSpec((1, 6, D_MODEL), lambda i: (i // tiles_per_cond, 0, 0)),
        _const_spec((1, D_MODEL)),
        _layer_spec((D_MODEL, D_MODEL), layer),
        _layer_spec((D_MODEL, 2 * D_FF), layer),
        _const_spec((3, 2 * D_FF)),
        _const_spec((1, 2 * D_FF)),
        _layer_spec((D_FF, D_MODEL), layer),
        _const_spec((1, D_MODEL)),
    ]
    args += [mod_l, gn2, wo, wu, cw, cb, wd, gf]
    return pl.pallas_call(
        functools.partial(_post_kernel, n_parts=len(o_parts), halo=halo, tiles_per_seq=tiles_per_seq, final=final),
        grid=(t // tm,),
        in_specs=in_specs,
        out_specs=pl.BlockSpec((tm, D_MODEL), tile),
        out_shape=jax.ShapeDtypeStruct((t, D_MODEL), F32),
        scratch_shapes=[pltpu.VMEM((tm, D_FF), BF16)],
        compiler_params=_cparams(("arbitrary",)),
        name="post",
    )(*args)


def _rope_tables(seq):
    t = np.arange(seq)
    rows = (t // GRID_W).astype(np.float32)[:, None]
    cols = (t % GRID_W).astype(np.float32)[:, None]

    def tab(half, reps):
        inv = np.float32(ROPE_THETA) ** (-np.arange(half, dtype=np.float32) / np.float32(half))
        ar, ac = rows * inv[None, :], cols * inv[None, :]
        cos = np.concatenate([np.cos(ar), np.cos(ar), np.cos(ac), np.cos(ac)], axis=-1)
        sin = np.concatenate([-np.sin(ar), np.sin(ar), -np.sin(ac), np.sin(ac)], axis=-1)
        return np.tile(cos, (1, reps)).astype(np.float32), np.tile(sin, (1, reps)).astype(np.float32)

    tabs = tab(HEAD_DIM // 4, LANES // HEAD_DIM) + tab(C_QK_DIM // 4, LANES // C_QK_DIM)
    tabs = tabs + tuple(np.ascontiguousarray(t.T) for t in tabs)
    return tuple(jnp.asarray(t) for t in tabs)


def _w_in_perms():
    sizes = (A_HEADS * HEAD_DIM, A_KV * HEAD_DIM, A_KV * HEAD_DIM, B_HEADS * HEAD_DIM, B_KV * HEAD_DIM,
             B_KV * HEAD_DIM, C_HEADS * 2 * C_QK_DIM, C_HEADS * 2 * C_QK_DIM, C_HEADS * C_V_DIM)
    offs = np.concatenate([[0], np.cumsum(sizes)])
    qa, ka, va, qb, kb, vb, qc, kc, vc = (np.arange(offs[j], offs[j + 1]) for j in range(9))
    pair = np.concatenate([np.arange(h * HEAD_DIM, (h + 1) * HEAD_DIM) for h in PAIRED_HEADS])
    return np.concatenate([qa[pair], va, qc, vc, vb, qb[pair]]), np.concatenate([ka, kb, kc])


def _w_out_perm():
    pair = np.concatenate([np.arange(h * HEAD_DIM, (h + 1) * HEAD_DIM) for h in PAIRED_HEADS])
    return np.concatenate([pair, GQA_COLS + pair, np.arange(MIX_C0, D_MODEL)])


def _take_runs(w, perm, axis):
    cuts = [0] + [j for j in range(1, len(perm)) if perm[j] != perm[j - 1] + 1] + [len(perm)]
    parts = [lax.slice_in_dim(w, int(perm[a]), int(perm[b - 1]) + 1, axis=axis) for a, b in zip(cuts[:-1], cuts[1:])]
    return jnp.concatenate(parts, axis=axis)


def kernel(x_prompt, x_sample, cache_a_k, cache_a_v, cache_b_k, cache_b_v, cache_c_k, cache_c_v, c, c_ctx, w_ada, b_ada, g_norm1, g_norm2, w_in, g_qa, g_ka, sink_b, lam_q1, lam_k1, lam_q2, lam_k2, g_subln, w_out, w_up, conv_w, conv_b, w_down, g_final):
    n_ctx_req, ctx_len, _ = x_prompt.shape
    n_lat_req, lat_len, _ = x_sample.shape
    past = cache_a_k.shape[2]

    conds = jnp.zeros((8, D_MODEL), F32).at[0].set(c_ctx).at[1:1 + n_lat_req].set(c)
    mod = _modulation(conds, w_ada, b_ada).reshape(DEPTH, 8, 6, D_MODEL)

    perm_t, perm_k = _w_in_perms()
    w_t = jnp.swapaxes(_take_runs(w_in, perm_t, 2), 1, 2).astype(BF16)
    w_k = _take_runs(w_in, perm_k, 2).astype(BF16)
    w_out_p = _take_runs(w_out, _w_out_perm(), 1).astype(BF16)
    w_up_b = w_up.astype(BF16)
    w_down_b = w_down.astype(BF16)
    rope_tabs = _rope_tables(lat_len)
    zpad = jnp.zeros((DEPTH, LANES - C_QK_DIM), F32)
    lam_rows = [jnp.concatenate([v, zpad], axis=-1) for v in (lam_q1, lam_k1, lam_q2, lam_k2)]
    lam_par = jnp.concatenate([jnp.stack(lam_rows, axis=1), jnp.zeros((DEPTH, 4, LANES), F32)], axis=1)
    gf = g_final.reshape(1, D_MODEL)

    xc = x_prompt.reshape(n_ctx_req * ctx_len, D_MODEL)
    xs = x_sample.reshape(n_lat_req * lat_len, D_MODEL)
    lat_tm = LATENT_TM
    caches = []
    for l in range(DEPTH):
        lam_init = 0.8 - 0.6 * math.exp(-0.3 * l)
        gn1 = g_norm1[l].reshape(1, D_MODEL)
        gn2 = g_norm2[l].reshape(1, D_MODEL)
        gq_col = jnp.tile(g_qa[l], LANES // HEAD_DIM)[:, None]
        gk = jnp.tile(g_ka[l], LANES // HEAD_DIM).reshape(1, LANES)
        gsub = jnp.tile(g_subln[l], LANES // C_V_DIM).reshape(1, LANES)
        post_w = (gn2, w_out_p, w_up_b, conv_w[l], conv_b[l].reshape(1, 2 * D_FF), w_down_b, gf)
        final = l == DEPTH - 1

        n_tiles = n_ctx_req
        zc, *caches, ztc = _in_proj(xc, mod[l, 0:1], gn1, w_t, w_k, jnp.broadcast_to(gq_col, (LANES, ctx_len)), gk,
                                    None, layer=l, tm=ctx_len, tiles_per_cond=n_tiles, tiles_per_seq=1,
                                    emit_cache="final" if final else "rows",
                                    prev_cache=caches[0] if final else None)
        oc = _attn_ctx(zc, ztc, sink_b[l], lam_par[l], gsub, seq=ctx_len, lam_init=lam_init)
        xc = _post(xc, [oc], mod[l, 0:1], *post_w, layer=l, tm=ctx_len, tiles_per_cond=n_tiles, tiles_per_seq=1)

        per_seq = lat_len // lat_tm
        zs, zts = _in_proj(xs, mod[l, 1:1 + n_lat_req], gn1, w_t, w_k, jnp.broadcast_to(gq_col, (LANES, lat_tm)), gk,
                           rope_tabs, layer=l, tm=lat_tm, tiles_per_cond=per_seq, tiles_per_seq=per_seq,
                           emit_cache=None)
        flat = lambda a: a[:, l].reshape(n_lat_req, past, -1).astype(BF16)
        flat_t = lambda a: jnp.swapaxes(flat(a), 1, 2)
        oa = _attn_a(zs, zts, flat(cache_a_k), flat_t(cache_a_v), seq=lat_len, tq=ATTN_A_TQ, tk=ATTN_TK)
        ob = _attn_b(zs, zts, flat(cache_b_k), flat_t(cache_b_v), sink_b[l], seq=lat_len, tq=ATTN_B_TQ,
                     tk=ATTN_B_TK)
        oc = _attn_c(zs, zts, flat(cache_c_k), flat_t(cache_c_v), lam_par[l], gsub, seq=lat_len, tq=ATTN_C_TQ,
                     tk=ATTN_TK, lam_init=lam_init)
        xs = _post(xs, [oa, ob, oc], mod[l, 1:1 + n_lat_req], *post_w, layer=l, tm=lat_tm, tiles_per_cond=per_seq,
                   tiles_per_seq=per_seq)

    heads = (A_KV, A_KV, B_KV, B_KV, C_HEADS, C_HEADS)
    new_caches = tuple(cch.reshape(n_ctx_req, DEPTH, ctx_len, h, -1) for cch, h in zip(caches, heads))
    return (xc.reshape(x_prompt.shape), xs.reshape(x_sample.shape)) + new_caches
```

```python
import functools
import math

import numpy as np
import jax
import jax.numpy as jnp
from jax import lax
from jax.experimental import pallas as pl
from jax.experimental.pallas import tpu as pltpu

D_MODEL = 1024
DEPTH = 2
GRID_W = 64
HEAD_DIM = 64
A_HEADS = 6
A_KV = 2
B_HEADS = 6
B_KV = 2
C_HEADS = 4
C_QK_DIM = 32
C_V_DIM = 2 * C_QK_DIM
WINDOW = 128
ROPE_THETA = 10000.0
D_FF = 2816
EPS = 1e-6
NEG = -1e30
LOG2E = math.log2(math.e)

LANES = 128
BF16_ROWS = 16
MXU_COLS = 256
VMEM_LIMIT = 56 * 1024 * 1024

GQA_COLS = A_HEADS * HEAD_DIM
MIX_C0 = 2 * GQA_COLS
Z_KA, Z_VA, Z_KB, Z_VB, Z_KC, Z_VC, CACHE_COLS = 0, 128, 256, 384, 512, 768, 1024
CACHE_PIECES = ((Z_KA, 128), (Z_VA, 128), (Z_KB, 128), (Z_VB, 128), (Z_KC, 256), (Z_VC, 256))
K_A, K_B, K_C, K_COLS = 0, 128, 256, 512
T_QA, T_VA, T_QC, T_VC, T_VB, T_QB, T_ROWS = 0, 384, 512, 768, 1024, 1152, 1536
PAIRED_HEADS = (0, 3, 1, 4, 2, 5)
FFN_CHUNK = MXU_COLS
HALO = BF16_ROWS
LATENT_TM = 512
ATTN_A_TQ, ATTN_B_TQ, ATTN_C_TQ, ATTN_TK = 512, 256, 512, 512
ATTN_B_TK = 256
MOD_COLS = 1536

F32 = jnp.float32
BF16 = jnp.bfloat16


def _cparams(sem):
    return pltpu.CompilerParams(dimension_semantics=sem, vmem_limit_bytes=VMEM_LIMIT)


def _const_spec(shape):
    nd = len(shape)
    return pl.BlockSpec(shape, lambda *_: (0,) * nd)


def _layer_spec(shape, layer):
    nd = len(shape)
    return pl.BlockSpec((1,) + tuple(shape), lambda *_: (layer,) + (0,) * nd)


def _mod_kernel(c_ref, w_ref, b_ref, o_ref):
    cond = c_ref[...]
    a = cond / (1.0 + jnp.exp(-cond))
    o_ref[0] = jnp.dot(a.astype(BF16), w_ref[0].astype(BF16), preferred_element_type=F32) + b_ref[0]


def _modulation(conds, w_ada, b_ada):
    nb = MOD_COLS
    n_out = w_ada.shape[-1]
    return pl.pallas_call(
        _mod_kernel,
        grid=(DEPTH, n_out // nb),
        in_specs=[
            pl.BlockSpec((8, D_MODEL), lambda l, j: (0, 0)),
            pl.BlockSpec((1, D_MODEL, nb), lambda l, j: (l, 0, j)),
            pl.BlockSpec((1, 1, nb), lambda l, j: (l, 0, j)),
        ],
        out_specs=pl.BlockSpec((1, 8, nb), lambda l, j: (l, 0, j)),
        out_shape=jax.ShapeDtypeStruct((DEPTH, 8, n_out), F32),
        compiler_params=_cparams(("arbitrary", "arbitrary")),
        name="modulation",
    )(conds, w_ada, b_ada.reshape(DEPTH, 1, n_out))


def _rms(x, g):
    ms = jnp.mean(x * x, axis=-1, keepdims=True)
    return x * lax.rsqrt(ms + EPS) * g


def _head_rms(x, g, lo):
    ss = x * x
    s_lo = jnp.sum(jnp.where(lo, ss, 0.0), axis=-1, keepdims=True)
    s_hi = jnp.sum(jnp.where(lo, 0.0, ss), axis=-1, keepdims=True)
    inv = jnp.where(lo, lax.rsqrt(s_lo * (1.0 / HEAD_DIM) + EPS), lax.rsqrt(s_hi * (1.0 / HEAD_DIM) + EPS))
    return x * inv * g


def _rope(x, cos, sin, chunk, first):
    sw = jnp.where(first, pltpu.roll(x, LANES - chunk, 1), pltpu.roll(x, chunk, 1))
    return x * cos + sw * sin


def _softmax_init_t(m_ref, acc_ref):
    m_ref[...] = jnp.full(m_ref.shape, NEG, F32)
    acc_ref[...] = jnp.zeros(acc_ref.shape, F32)


def _scores_t(k, q_scr, s_ref, mx_ref, cs, allowed):
    s = jnp.dot(k, q_scr[:, cs], preferred_element_type=F32)
    if allowed is not None:
        s = jnp.where(allowed(cs), s, NEG)
    s_ref[0:k.shape[0], cs] = s
    mx_ref[:, cs] = jnp.max(s, axis=0, keepdims=True)


def _pv_and_sum(vt, p):
    ones = jnp.ones((BF16_ROWS, vt.shape[1]), BF16)
    return jnp.dot(jnp.concatenate([vt, ones], axis=0), p, preferred_element_type=F32)


def _softmax_result_t(acc_ref):
    return acc_ref[0:LANES] * (1.0 / acc_ref[LANES:LANES + 1])


def _attend_blocks_t(q_scr, blocks, s_bufs, x_bufs, m_ref, acc_ref):
    n = q_scr.shape[1]
    groups = [slice(c, c + MXU_COLS) for c in range(0, n, MXU_COLS)]

    def step(par, cur, nxt):
        vt = cur[1]() if cur else None
        k = nxt[0]() if nxt else None
        for cs in groups:
            if cur:
                s = s_bufs[par][0:vt.shape[1], cs]
                m_prev = m_ref[:, cs]
                m_new = jnp.maximum(m_prev, x_bufs[par][:, cs])
                m_ref[:, cs] = m_new
                pv = _pv_and_sum(vt, jnp.exp2(s - m_new).astype(BF16))
                acc_ref[:, cs] = jnp.exp2(m_prev - m_new) * acc_ref[:, cs] + pv
            if nxt:
                _scores_t(k, q_scr, s_bufs[1 - par], x_bufs[1 - par], cs, nxt[2])

    step(1, None, blocks[0])
    for t, blk in enumerate(blocks):
        step(t % 2, blk, blocks[t + 1] if t + 1 < len(blocks) else None)


def _ctx_then_latent_blocks(kc_ref, vct_ref, k_ref, vt_ref, tk):
    blocks = [(lambda: kc_ref[0], lambda: vct_ref[0], None)]
    for j in range(k_ref.shape[0] // tk):
        blocks.append((lambda j=j: k_ref[j * tk:(j + 1) * tk, :], lambda j=j: vt_ref[:, j * tk:(j + 1) * tk], None))
    return blocks


def _diff_lambda(lam_ref, lam_init):
    f = lambda a, b: jnp.exp(jnp.sum(a * b, axis=-1, keepdims=True))
    return f(lam_ref[0:1], lam_ref[1:2]) - f(lam_ref[2:3], lam_ref[3:4]) + lam_init


def _swap_row_chunks(x, chunk):
    parts = []
    for r in range(0, x.shape[0], 2 * chunk):
        parts += [x[r + chunk:r + 2 * chunk], x[r:r + chunk]]
    return jnp.concatenate(parts, axis=0)


def _in_proj_kernel(*refs, use_rope, emit_cache):
    it = iter(refs)
    x_ref, mod_ref, gn_ref, wt_ref, wk_ref, gq_ref, gk_ref = (next(it) for _ in range(7))
    cos64 = sin64 = cos32 = sin32 = cos64t = sin64t = cos32t = sin32t = None
    if use_rope:
        cos64, sin64, cos32, sin32, cos64t, sin64t, cos32t, sin32t = (next(it)[...] for _ in range(8))
    prev_ref = next(it) if emit_cache == "final" else None
    zk_ref = next(it)
    cache_ref = next(it) if emit_cache == "rows" else None
    final_refs = [next(it) for _ in CACHE_PIECES] if emit_cache == "final" else None
    zt_ref = next(it)
    if final_refs:
        for ref, (start, width) in zip(final_refs, CACHE_PIECES):
            ref[0, 0] = prev_ref[:, start:start + width]

    tm = x_ref.shape[0]
    sub = min(tm, MXU_COLS)
    mod = mod_ref[0]
    lane = lax.broadcasted_iota(jnp.int32, (sub, LANES), 1)
    lo = lane < HEAD_DIM
    first16 = (lane & 31) < 16
    first8 = (lane & 15) < 8
    q_scale = HEAD_DIM ** -0.5 * LOG2E
    qc_scale = C_QK_DIM ** -0.5 * LOG2E

    def matmuls(r0):
        h = (_rms(x_ref[r0:r0 + sub], gn_ref[...]) * (1.0 + mod[1:2]) + mod[0:1]).astype(BF16)
        zk = jnp.dot(h, wk_ref[0], preferred_element_type=F32)
        zt = lax.dot_general(wt_ref[0], h, (((1,), (1,)), ((), ())), preferred_element_type=F32)
        return zk, zt

    def put_cache(r0, off, v):
        if cache_ref is not None:
            cache_ref[r0:r0 + sub, off:off + LANES] = v
        if final_refs:
            for ref, (start, width) in zip(final_refs, CACHE_PIECES):
                if start <= off < start + width:
                    ref[0, DEPTH - 1, r0:r0 + sub, off - start:off - start + LANES] = v

    def finish_keys(r0, zk):
        def tab(t):
            return t[r0:r0 + sub] if use_rope else None

        def rope64(v):
            return _rope(v, tab(cos64), tab(sin64), 16, first16) if use_rope else v

        def rope32(v):
            return _rope(v, tab(cos32), tab(sin32), 8, first8) if use_rope else v

        def put_k(off, v):
            zk_ref[r0:r0 + sub, off:off + LANES] = v.astype(BF16)

        ka = _head_rms(zk[:, K_A:K_A + LANES], gk_ref[...], lo)
        put_cache(r0, Z_KA, ka)
        put_k(K_A, rope64(ka))
        kb = zk[:, K_B:K_B + LANES]
        put_cache(r0, Z_KB, kb)
        put_k(K_B, rope64(kb))
        for s in range(2):
            kc = zk[:, K_C + LANES * s:K_C + LANES * (s + 1)]
            put_cache(r0, Z_KC + LANES * s, kc)
            put_k(K_C + LANES * s, rope32(kc))

    def finish_features(r0, zt):
        def t_slab(off):
            return zt[off:off + LANES]

        def put_t(off, v):
            zt_ref[off:off + LANES, r0:r0 + sub] = v.astype(BF16)

        def rope_t(v, cos_t, sin_t, chunk):
            if not use_rope:
                return v
            return v * cos_t[:, r0:r0 + sub] + _swap_row_chunks(v, chunk) * sin_t[:, r0:r0 + sub]

        def head_rms_t(v):
            ss = v * v
            halves = []
            for r in (0, HEAD_DIM):
                inv = lax.rsqrt(jnp.sum(ss[r:r + HEAD_DIM], axis=0, keepdims=True) * (1.0 / HEAD_DIM) + EPS)
                halves.append(v[r:r + HEAD_DIM] * inv)
            return jnp.concatenate(halves, axis=0) * gq_ref[:, 0:sub]

        for s in range(3):
            put_t(T_QA + LANES * s, rope_t(head_rms_t(t_slab(T_QA + LANES * s)), cos64t, sin64t, 16) * q_scale)
            put_t(T_QB + LANES * s, rope_t(t_slab(T_QB + LANES * s), cos64t, sin64t, 16) * q_scale)
        for s in range(2):
            put_t(T_QC + LANES * s, rope_t(t_slab(T_QC + LANES * s), cos32t, sin32t, 8) * qc_scale)
        for t_off, z_off in ((T_VA, Z_VA), (T_VB, Z_VB), (T_VC, Z_VC), (T_VC + LANES, Z_VC + LANES)):
            put_t(t_off, t_slab(t_off))
            if emit_cache:
                put_cache(r0, z_off, t_slab(t_off).T)

    starts = list(range(0, tm, sub))
    prods = [matmuls(r0) for r0 in starts]
    for r0, (zk, _) in zip(starts, prods):
        finish_keys(r0, zk)
    for r0, (_, zt) in zip(starts, prods):
        finish_features(r0, zt)


def _in_proj(x, mod_l, gn, w_t, w_k, gq_t, gk, rope_tabs, *, layer, tm, tiles_per_cond, tiles_per_seq, emit_cache,
             prev_cache=None):
    t = x.shape[0]
    use_rope = rope_tabs is not None
    assert (emit_cache == "final") == (prev_cache is not None) and DEPTH == 2
    in_specs = [
        pl.BlockSpec((tm, D_MODEL), lambda i: (i, 0)),
        pl.BlockSpec((1, 6, D_MODEL), lambda i: (i // tiles_per_cond, 0, 0)),
        _const_spec((1, D_MODEL)),
        _layer_spec((T_ROWS, D_MODEL), layer),
        _layer_spec((D_MODEL, K_COLS), layer),
        _const_spec((LANES, tm)),
        _const_spec((1, LANES)),
    ]
    args = [x, mod_l, gn, w_t, w_k, gq_t, gk]
    if use_rope:
        in_specs += [pl.BlockSpec((tm, LANES), lambda i: (i % tiles_per_seq, 0))] * 4
        in_specs += [pl.BlockSpec((LANES, tm), lambda i: (0, i % tiles_per_seq))] * 4
        args += list(rope_tabs)
    if prev_cache is not None:
        in_specs.append(pl.BlockSpec((tm, CACHE_COLS), lambda i: (i, 0)))
        args.append(prev_cache)
    out_shape = [jax.ShapeDtypeStruct((t, K_COLS), BF16)]
    out_specs = [pl.BlockSpec((tm, K_COLS), lambda i: (i, 0))]
    if emit_cache == "rows":
        out_shape.append(jax.ShapeDtypeStruct((t, CACHE_COLS), F32))
        out_specs.append(pl.BlockSpec((tm, CACHE_COLS), lambda i: (i, 0)))
    if emit_cache == "final":
        for _, width in CACHE_PIECES:
            out_shape.append(jax.ShapeDtypeStruct((t // tm, DEPTH, tm, width), F32))
            out_specs.append(pl.BlockSpec((1, DEPTH, tm, width), lambda i: (i, 0, 0, 0)))
    out_shape.append(jax.ShapeDtypeStruct((T_ROWS, t), BF16))
    out_specs.append(pl.BlockSpec((T_ROWS, tm), lambda i: (0, i)))
    return pl.pallas_call(
        functools.partial(_in_proj_kernel, use_rope=use_rope, emit_cache=emit_cache),
        grid=(t // tm,),
        in_specs=in_specs,
        out_specs=out_specs,
        out_shape=out_shape,
        compiler_params=_cparams(("arbitrary",)),
        name="in_proj",
    )(*args)


def _gqa_queries_t(qt_ref, lo):
    slabs = [qt_ref[LANES * s:LANES * (s + 1), :] for s in range(3)]
    zero = jnp.zeros_like(slabs[0])
    return jnp.concatenate([jnp.where(lo, s, zero) for s in slabs] + [jnp.where(lo, zero, s) for s in slabs], axis=1)


def _gqa_store_t(ot, o_ref, tq, lo, col0=0):
    for s in range(3):
        slab_t = jnp.where(lo, ot[:, s * tq:(s + 1) * tq], ot[:, (3 + s) * tq:(4 + s) * tq])
        o_ref[:, col0 + LANES * s:col0 + LANES * (s + 1)] = slab_t.T.astype(BF16)


def _diff_queries_t(qt, row):
    zero = jnp.zeros_like(qt)
    return jnp.concatenate([jnp.where((row >= C_QK_DIM * j) & (row < C_QK_DIM * (j + 1)), qt, zero)
                            for j in range(4)], axis=1)


def _diff_output_t(ot, tq, lam, gsub, lam_init, row):
    o_even = ot[:, 0:tq] - lam * ot[:, tq:2 * tq]
    o_odd = ot[:, 2 * tq:3 * tq] - lam * ot[:, 3 * tq:4 * tq]
    oc = jnp.where(row < C_V_DIM, o_even, o_odd).T
    lo = lax.broadcasted_iota(jnp.int32, (tq, LANES), 1) < C_V_DIM
    return (_head_rms(oc, gsub, lo) * (1.0 - lam_init)).astype(BF16)


def _attn_ctx_kernel(sink_ref, lam_ref, gsub_ref, z_ref, zt_ref, o_ref, *, lam_init):
    tq = z_ref.shape[0]
    row = lax.broadcasted_iota(jnp.int32, (LANES, tq), 0)
    lo = row < HEAD_DIM

    def scores(q, k, sink=None):
        s = jnp.dot(k, q, preferred_element_type=F32)
        m = jnp.max(s, axis=0, keepdims=True)
        return s, (m if sink is None else jnp.maximum(m, sink))

    def values(s, m, vt, sink=None):
        acc = _pv_and_sum(vt, jnp.exp2(s - m).astype(BF16))
        l = acc[LANES:LANES + 1]
        if sink is not None:
            l = l + jnp.exp2(sink - m)
        return acc[0:LANES] * (1.0 / l)

    def keys(off):
        return z_ref[:, off:off + LANES]

    def feat(off):
        return zt_ref[off:off + LANES, :]

    sink = jnp.concatenate([jnp.full((1, tq), sink_ref[h] * LOG2E, F32) for h in range(B_HEADS)], axis=1)
    lam = _diff_lambda(lam_ref, lam_init)

    def store_c(s, ot):
        c0 = MIX_C0 + LANES * s
        o_ref[:, c0:c0 + LANES] = _diff_output_t(ot, tq, lam, gsub_ref[...], lam_init, row)

    sa = scores(_gqa_queries_t(zt_ref.at[T_QA:T_QA + GQA_COLS], lo), keys(K_A))
    sb = scores(_gqa_queries_t(zt_ref.at[T_QB:T_QB + GQA_COLS], lo), keys(K_B), sink)
    _gqa_store_t(values(*sa, feat(T_VA)), o_ref, tq, lo)
    sc0 = scores(_diff_queries_t(feat(T_QC), row), keys(K_C))
    _gqa_store_t(values(*sb, feat(T_VB), sink), o_ref, tq, lo, col0=GQA_COLS)
    sc1 = scores(_diff_queries_t(feat(T_QC + LANES), row), keys(K_C + LANES))
    store_c(0, values(*sc0, feat(T_VC)))
    store_c(1, values(*sc1, feat(T_VC + LANES)))


def _attn_ctx(z, zt, sink, lam_par, gsub, *, seq, lam_init):
    t = z.shape[0]
    return pl.pallas_call(
        functools.partial(_attn_ctx_kernel, lam_init=lam_init),
        grid=(t // seq,),
        in_specs=[
            pl.BlockSpec(memory_space=pltpu.SMEM),
            _const_spec((8, LANES)),
            _const_spec((1, LANES)),
            pl.BlockSpec((seq, K_COLS), lambda b: (b, 0)),
            pl.BlockSpec((T_ROWS, seq), lambda b: (0, b)),
        ],
        out_specs=pl.BlockSpec((seq, D_MODEL), lambda b: (b, 0)),
        out_shape=jax.ShapeDtypeStruct((t, D_MODEL), BF16),
        compiler_params=_cparams(("arbitrary",)),
        name="attn_ctx",
    )(sink, lam_par, gsub, z, zt)


def _attn_a_kernel(qt_ref, kc_ref, vct_ref, k_ref, vt_ref, o_ref, q_scr, m_ref, acc_ref,
                   s0, s1, x0, x1, *, tk):
    tq = qt_ref.shape[1]
    lo = lax.broadcasted_iota(jnp.int32, (LANES, tq), 0) < HEAD_DIM
    refs = (m_ref, acc_ref)
    q_scr[...] = _gqa_queries_t(qt_ref, lo)
    _softmax_init_t(*refs)
    _attend_blocks_t(q_scr, _ctx_then_latent_blocks(kc_ref, vct_ref, k_ref, vt_ref, tk), (s0, s1), (x0, x1), *refs)
    _gqa_store_t(_softmax_result_t(acc_ref), o_ref, tq, lo)


def _keys_major_scratch(n, tk):
    return [pltpu.VMEM((LANES, n), BF16), pltpu.VMEM((1, n), F32),
            pltpu.VMEM((LANES + BF16_ROWS, n), F32),
            pltpu.VMEM((tk, n), F32), pltpu.VMEM((tk, n), F32),
            pltpu.VMEM((1, n), F32), pltpu.VMEM((1, n), F32)]


def _attn_a(z, zt, k_ctx, vt_ctx, *, seq, tq, tk):
    t = z.shape[0]
    nq = seq // tq
    n_ctx = k_ctx.shape[1]
    n = A_HEADS * tq
    return pl.pallas_call(
        functools.partial(_attn_a_kernel, tk=tk),
        grid=(t // seq, nq),
        in_specs=[
            pl.BlockSpec((GQA_COLS, tq), lambda b, i: (T_QA // GQA_COLS, b * nq + i)),
            pl.BlockSpec((1, n_ctx, LANES), lambda b, i: (b, 0, 0)),
            pl.BlockSpec((1, LANES, n_ctx), lambda b, i: (b, 0, 0)),
            pl.BlockSpec((seq, LANES), lambda b, i: (b, K_A // LANES)),
            pl.BlockSpec((LANES, seq), lambda b, i: (T_VA // LANES, b)),
        ],
        out_specs=pl.BlockSpec((tq, GQA_COLS), lambda b, i: (b * nq + i, 0)),
        out_shape=jax.ShapeDtypeStruct((t, GQA_COLS), BF16),
        scratch_shapes=_keys_major_scratch(n, tk),
        compiler_params=_cparams(("arbitrary", "arbitrary")),
        name="attn_a",
    )(zt, k_ctx, vt_ctx, z, zt)


def _attn_b_kernel(sink_ref, qt_ref, kc_ref, vct_ref, k_ref, vt_ref, o_ref, q_scr, m_ref, acc_ref, s0, s1, x0, x1,
                   *, tk):
    tq = qt_ref.shape[1]
    seq = k_ref.shape[0]
    n = B_HEADS * tq
    band = tq + 2 * WINDOW
    i = pl.program_id(1)
    lo = lax.broadcasted_iota(jnp.int32, (LANES, tq), 0) < HEAD_DIM
    q_scr[...] = _gqa_queries_t(qt_ref, lo)
    m_ref[...] = jnp.concatenate([jnp.full((1, tq), sink_ref[h] * LOG2E, F32) for h in range(B_HEADS)], axis=1)
    acc_ref[...] = jnp.zeros(acc_ref.shape, F32)
    acc_ref[LANES:LANES + 1] = jnp.ones((1, n), F32)
    start = jnp.clip(i * tq - WINDOW, 0, seq - band)

    masks = {}

    def allowed(j, cs):
        q_off = cs.start % tq
        if (j, q_off) not in masks:
            kpos = start + j * tk + lax.broadcasted_iota(jnp.int32, (tk, MXU_COLS), 0)
            qpos = i * tq + q_off + lax.broadcasted_iota(jnp.int32, (tk, MXU_COLS), 1)
            masks[(j, q_off)] = jnp.abs(kpos - qpos) <= WINDOW
        return masks[(j, q_off)]

    def band_rows(j):
        return pl.ds(pl.multiple_of(start + j * tk, LANES), tk)

    blocks = [(lambda: kc_ref[0], lambda: vct_ref[0], None)]
    for j in range(band // tk):
        blocks.append((lambda j=j: k_ref[band_rows(j), :], lambda j=j: vt_ref[:, band_rows(j)],
                       functools.partial(allowed, j)))
    _attend_blocks_t(q_scr, blocks, (s0, s1), (x0, x1), m_ref, acc_ref)
    _gqa_store_t(_softmax_result_t(acc_ref), o_ref, tq, lo)


def _attn_b(z, zt, k_ctx, vt_ctx, sink, *, seq, tq, tk):
    t = z.shape[0]
    nq = seq // tq
    n_ctx = k_ctx.shape[1]
    assert (tq + 2 * WINDOW) % tk == 0 and MXU_COLS <= tq and n_ctx <= tk
    return pl.pallas_call(
        functools.partial(_attn_b_kernel, tk=tk),
        grid=(t // seq, nq),
        in_specs=[
            pl.BlockSpec(memory_space=pltpu.SMEM),
            pl.BlockSpec((GQA_COLS, tq), lambda b, i: (T_QB // GQA_COLS, b * nq + i)),
            pl.BlockSpec((1, n_ctx, LANES), lambda b, i: (b, 0, 0)),
            pl.BlockSpec((1, LANES, n_ctx), lambda b, i: (b, 0, 0)),
            pl.BlockSpec((seq, LANES), lambda b, i: (b, K_B // LANES)),
            pl.BlockSpec((LANES, seq), lambda b, i: (T_VB // LANES, b)),
        ],
        out_specs=pl.BlockSpec((tq, GQA_COLS), lambda b, i: (b * nq + i, 0)),
        out_shape=jax.ShapeDtypeStruct((t, GQA_COLS), BF16),
        scratch_shapes=_keys_major_scratch(B_HEADS * tq, tk),
        compiler_params=_cparams(("arbitrary", "arbitrary")),
        name="attn_b",
    )(sink, zt, k_ctx, vt_ctx, z, zt)


def _attn_c_kernel(lam_ref, gsub_ref, qt_ref, kc_ref, vct_ref, k_ref, vt_ref, o_ref,
                   q_scr, m_ref, acc_ref, s0, s1, x0, x1, *, tk, lam_init):
    tq = qt_ref.shape[1]
    row = lax.broadcasted_iota(jnp.int32, (LANES, tq), 0)
    refs = (m_ref, acc_ref)
    q_scr[...] = _diff_queries_t(qt_ref[...], row)
    _softmax_init_t(*refs)
    _attend_blocks_t(q_scr, _ctx_then_latent_blocks(kc_ref, vct_ref, k_ref, vt_ref, tk), (s0, s1), (x0, x1), *refs)
    lam = _diff_lambda(lam_ref, lam_init)
    o_ref[...] = _diff_output_t(_softmax_result_t(acc_ref), tq, lam, gsub_ref[...], lam_init, row)


def _attn_c(z, zt, k_ctx, vt_ctx, lam_par, gsub, *, seq, tq, tk, lam_init):
    t = z.shape[0]
    nq = seq // tq
    n_ctx = k_ctx.shape[1]
    n = 4 * tq
    return pl.pallas_call(
        functools.partial(_attn_c_kernel, tk=tk, lam_init=lam_init),
        grid=(t // seq, 2, nq),
        in_specs=[
            _const_spec((8, LANES)),
            _const_spec((1, LANES)),
            pl.BlockSpec((LANES, tq), lambda b, s, i: (T_QC // LANES + s, b * nq + i)),
            pl.BlockSpec((1, n_ctx, LANES), lambda b, s, i: (b, 0, s)),
            pl.BlockSpec((1, LANES, n_ctx), lambda b, s, i: (b, s, 0)),
            pl.BlockSpec((seq, LANES), lambda b, s, i: (b, K_C // LANES + s)),
            pl.BlockSpec((LANES, seq), lambda b, s, i: (T_VC // LANES + s, b)),
        ],
        out_specs=pl.BlockSpec((tq, LANES), lambda b, s, i: (b * nq + i, s)),
        out_shape=jax.ShapeDtypeStruct((t, C_HEADS * C_V_DIM), BF16),
        scratch_shapes=_keys_major_scratch(n, tk),
        compiler_params=_cparams(("arbitrary", "arbitrary", "arbitrary")),
        name="attn_c",
    )(lam_par, gsub, zt, k_ctx, vt_ctx, z, zt)


def _post_kernel(*refs, n_parts, halo, tiles_per_seq, final):
    it = iter(refs)
    tiles = [next(it) for _ in range(1 + n_parts)]
    halos = [(next(it), next(it)) for _ in range(1 + n_parts)] if halo else None
    (mod_ref, gn_ref, wo_ref, wu_ref, cw_ref, cb_ref, wd_ref, gf_ref, out_ref, act_scr) = (next(it) for _ in range(10))

    def rows_of(j):
        if halo:
            return jnp.concatenate([halos[j][0][...], tiles[j][...], halos[j][1][...]], axis=0)
        return tiles[j][...]

    tm = tiles[0].shape[0]
    x = rows_of(0)
    o = jnp.concatenate([rows_of(j) for j in range(1, 1 + n_parts)], axis=1)
    ext = x.shape[0]
    mod = mod_ref[0]
    cut = ext // 2 // BF16_ROWS * BF16_ROWS
    halves = [(0, cut), (cut, ext)]
    proj = [jnp.dot(o[a:b], wo_ref[0], preferred_element_type=F32) for a, b in halves]
    x1 = jnp.concatenate([x[a:b] + mod[2:3] * pr for (a, b), pr in zip(halves, proj)], axis=0)
    h = jnp.concatenate([_rms(x1[a:b], gn_ref[...]) * (1.0 + mod[4:5]) + mod[3:4] for a, b in halves], axis=0)
    row = lax.broadcasted_iota(jnp.int32, (ext, 1), 0)
    if halo:
        t_in_seq = pl.program_id(0) % tiles_per_seq
        keep = ((row >= halo) | (t_in_seq > 0)) & ((row < halo + tm) | (t_in_seq < tiles_per_seq - 1))
        h = jnp.where(keep, h, 0.0)
    h = h.astype(BF16)

    def conv(u, c0):
        cw = cw_ref[:, c0:c0 + FFN_CHUNK]
        up = pltpu.roll(u, 1, 0)
        dn = pltpu.roll(u, ext - 1, 0)
        if not halo:
            up = jnp.where(row == 0, 0.0, up)
            dn = jnp.where(row == ext - 1, 0.0, dn)
        v = cw[0:1] * up + cw[1:2] * u + cw[2:3] * dn + cb_ref[:, c0:c0 + FFN_CHUNK]
        return v[halo:halo + tm]

    def up_proj(c):
        ca, cg = c * FFN_CHUNK, D_FF + c * FFN_CHUNK
        return (jnp.dot(h, wu_ref[0, :, ca:ca + FFN_CHUNK], preferred_element_type=F32),
                jnp.dot(h, wu_ref[0, :, cg:cg + FFN_CHUNK], preferred_element_type=F32))

    n_chunks = D_FF // FFN_CHUNK
    nxt = up_proj(0)
    for c in range(n_chunks):
        ua, ug = nxt
        if c + 1 < n_chunks:
            nxt = up_proj(c + 1)
        ca = c * FFN_CHUNK
        a = conv(ua, ca)
        g = conv(ug, D_FF + ca)
        act_scr[:, ca:ca + FFN_CHUNK] = (a / (1.0 + jnp.exp(-a)) * g).astype(BF16)
    x2 = x1[halo:halo + tm] + mod[5:6] * jnp.dot(act_scr[...], wd_ref[0], preferred_element_type=F32)
    if final:
        x2 = _rms(x2, gf_ref[...])
    out_ref[...] = x2


def _post(x, o_parts, mod_l, gn2, wo, wu, cw, cb, wd, gf, *, layer, tm, tiles_per_cond, tiles_per_seq):
    t = x.shape[0]
    final = layer == DEPTH - 1
    assert sum(o.shape[1] for o in o_parts) == D_MODEL
    halo = HALO if tiles_per_seq > 1 else 0
    tile = lambda i: (i, 0)
    rows = [x] + list(o_parts)
    in_specs = [pl.BlockSpec((tm, a.shape[1]), tile) for a in rows]
    args = list(rows)
    if halo:
        per = tm // halo
        prev = lambda i: (jnp.maximum(i * per - 1, 0), 0)
        nxt = lambda i: (jnp.minimum((i + 1) * per, t // halo - 1), 0)
        for a in rows:
            in_specs += [pl.BlockSpec((halo, a.shape[1]), prev), pl.BlockSpec((halo, a.shape[1]), nxt)]
            args += [a, a]
    in_specs += [
        pl.BlockSpec((1, 6, D_MODEL), lambda i: (i // tiles_per_cond, 0, 0)),
        _const_spec((1, D_MODEL)),
        _layer_spec((D_MODEL, D_MODEL), layer),
        _layer_spec((D_MODEL, 2 * D_FF), layer),
        _const_spec((3, 2 * D_FF)),
        _const_spec((1, 2 * D_FF)),
        _layer_spec((D_FF, D_MODEL), layer),
        _const_spec((1, D_MODEL)),
    ]
    args += [mod_l, gn2, wo, wu, cw, cb, wd, gf]
    return pl.pallas_call(
        functools.partial(_post_kernel, n_parts=len(o_parts), halo=halo, tiles_per_seq=tiles_per_seq, final=final),
        grid=(t // tm,),
        in_specs=in_specs,
        out_specs=pl.BlockSpec((tm, D_MODEL), tile),
        out_shape=jax.ShapeDtypeStruct((t, D_MODEL), F32),
        scratch_shapes=[pltpu.VMEM((tm, D_FF), BF16)],
        compiler_params=_cparams(("arbitrary",)),
        name="post",
    )(*args)


def _rope_tables(seq):
    t = np.arange(seq)
    rows = (t // GRID_W).astype(np.float32)[:, None]
    cols = (t % GRID_W).astype(np.float32)[:, None]

    def tab(half, reps):
        inv = np.float32(ROPE_THETA) ** (-np.arange(half, dtype=np.float32) / np.float32(half))
        ar, ac = rows * inv[None, :], cols * inv[None, :]
        cos = np.concatenate([np.cos(ar), np.cos(ar), np.cos(ac), np.cos(ac)], axis=-1)
        sin = np.concatenate([-np.sin(ar), np.sin(ar), -np.sin(ac), np.sin(ac)], axis=-1)
        return np.tile(cos, (1, reps)).astype(np.float32), np.tile(sin, (1, reps)).astype(np.float32)

    tabs = tab(HEAD_DIM // 4, LANES // HEAD_DIM) + tab(C_QK_DIM // 4, LANES // C_QK_DIM)
    tabs = tabs + tuple(np.ascontiguousarray(t.T) for t in tabs)
    return tuple(jnp.asarray(t) for t in tabs)


def _w_in_perms():
    sizes = (A_HEADS * HEAD_DIM, A_KV * HEAD_DIM, A_KV * HEAD_DIM, B_HEADS * HEAD_DIM, B_KV * HEAD_DIM,
             B_KV * HEAD_DIM, C_HEADS * 2 * C_QK_DIM, C_HEADS * 2 * C_QK_DIM, C_HEADS * C_V_DIM)
    offs = np.concatenate([[0], np.cumsum(sizes)])
    qa, ka, va, qb, kb, vb, qc, kc, vc = (np.arange(offs[j], offs[j + 1]) for j in range(9))
    pair = np.concatenate([np.arange(h * HEAD_DIM, (h + 1) * HEAD_DIM) for h in PAIRED_HEADS])
    return np.concatenate([qa[pair], va, qc, vc, vb, qb[pair]]), np.concatenate([ka, kb, kc])


def _w_out_perm():
    pair = np.concatenate([np.arange(h * HEAD_DIM, (h + 1) * HEAD_DIM) for h in PAIRED_HEADS])
    return np.concatenate([pair, GQA_COLS + pair, np.arange(MIX_C0, D_MODEL)])


def _take_runs(w, perm, axis):
    cuts = [0] + [j for j in range(1, len(perm)) if perm[j] != perm[j - 1] + 1] + [len(perm)]
    parts = [lax.slice_in_dim(w, int(perm[a]), int(perm[b - 1]) + 1, axis=axis) for a, b in zip(cuts[:-1], cuts[1:])]
    return jnp.concatenate(parts, axis=axis)


def kernel(x_prompt, x_sample, cache_a_k, cache_a_v, cache_b_k, cache_b_v, cache_c_k, cache_c_v, c, c_ctx, w_ada, b_ada, g_norm1, g_norm2, w_in, g_qa, g_ka, sink_b, lam_q1, lam_k1, lam_q2, lam_k2, g_subln, w_out, w_up, conv_w, conv_b, w_down, g_final):
    n_ctx_req, ctx_len, _ = x_prompt.shape
    n_lat_req, lat_len, _ = x_sample.shape
    past = cache_a_k.shape[2]

    conds = jnp.zeros((8, D_MODEL), F32).at[0].set(c_ctx).at[1:1 + n_lat_req].set(c)
    mod = _modulation(conds, w_ada, b_ada).reshape(DEPTH, 8, 6, D_MODEL)

    perm_t, perm_k = _w_in_perms()
    w_t = jnp.swapaxes(_take_runs(w_in, perm_t, 2), 1, 2).astype(BF16)
    w_k = _take_runs(w_in, perm_k, 2).astype(BF16)
    w_out_p = _take_runs(w_out, _w_out_perm(), 1).astype(BF16)
    w_up_b = w_up.astype(BF16)
    w_down_b = w_down.astype(BF16)
    rope_tabs = _rope_tables(lat_len)
    zpad = jnp.zeros((DEPTH, LANES - C_QK_DIM), F32)
    lam_rows = [jnp.concatenate([v, zpad], axis=-1) for v in (lam_q1, lam_k1, lam_q2, lam_k2)]
    lam_par = jnp.concatenate([jnp.stack(lam_rows, axis=1), jnp.zeros((DEPTH, 4, LANES), F32)], axis=1)
    gf = g_final.reshape(1, D_MODEL)

    xc = x_prompt.reshape(n_ctx_req * ctx_len, D_MODEL)
    xs = x_sample.reshape(n_lat_req * lat_len, D_MODEL)
    lat_tm = LATENT_TM
    caches = []
    for l in range(DEPTH):
        lam_init = 0.8 - 0.6 * math.exp(-0.3 * l)
        gn1 = g_norm1[l].reshape(1, D_MODEL)
        gn2 = g_norm2[l].reshape(1, D_MODEL)
        gq_col = jnp.tile(g_qa[l], LANES // HEAD_DIM)[:, None]
        gk = jnp.tile(g_ka[l], LANES // HEAD_DIM).reshape(1, LANES)
        gsub = jnp.tile(g_subln[l], LANES // C_V_DIM).reshape(1, LANES)
        post_w = (gn2, w_out_p, w_up_b, conv_w[l], conv_b[l].reshape(1, 2 * D_FF), w_down_b, gf)
        final = l == DEPTH - 1

        n_tiles = n_ctx_req
        zc, *caches, ztc = _in_proj(xc, mod[l, 0:1], gn1, w_t, w_k, jnp.broadcast_to(gq_col, (LANES, ctx_len)), gk,
                                    None, layer=l, tm=ctx_len, tiles_per_cond=n_tiles, tiles_per_seq=1,
                                    emit_cache="final" if final else "rows",
                                    prev_cache=caches[0] if final else None)
        oc = _attn_ctx(zc, ztc, sink_b[l], lam_par[l], gsub, seq=ctx_len, lam_init=lam_init)
        xc = _post(xc, [oc], mod[l, 0:1], *post_w, layer=l, tm=ctx_len, tiles_per_cond=n_tiles, tiles_per_seq=1)

        per_seq = lat_len // lat_tm
        zs, zts = _in_proj(xs, mod[l, 1:1 + n_lat_req], gn1, w_t, w_k, jnp.broadcast_to(gq_col, (LANES, lat_tm)), gk,
                           rope_tabs, layer=l, tm=lat_tm, tiles_per_cond=per_seq, tiles_per_seq=per_seq,
                           emit_cache=None)
        flat = lambda a: a[:, l].reshape(n_lat_req, past, -1).astype(BF16)
        flat_t = lambda a: jnp.swapaxes(flat(a), 1, 2)
        oa = _attn_a(zs, zts, flat(cache_a_k), flat_t(cache_a_v), seq=lat_len, tq=ATTN_A_TQ, tk=ATTN_TK)
        ob = _attn_b(zs, zts, flat(cache_b_k), flat_t(cache_b_v), sink_b[l], seq=lat_len, tq=ATTN_B_TQ,
                     tk=ATTN_B_TK)
        oc = _attn_c(zs, zts, flat(cache_c_k), flat_t(cache_c_v), lam_par[l], gsub, seq=lat_len, tq=ATTN_C_TQ,
                     tk=ATTN_TK, lam_init=lam_init)
        xs = _post(xs, [oa, ob, oc], mod[l, 1:1 + n_lat_req], *post_w, layer=l, tm=lat_tm, tiles_per_cond=per_seq,
                   tiles_per_seq=per_seq)

    heads = (A_KV, A_KV, B_KV, B_KV, C_HEADS, C_HEADS)
    new_caches = tuple(cch.reshape(n_ctx_req, DEPTH, ctx_len, h, -1) for cch, h in zip(caches, heads))
    return (xc.reshape(x_prompt.shape), xs.reshape(x_sample.shape)) + new_caches
```

```python
import functools
import math

import numpy as np
import jax
import jax.numpy as jnp
from jax import lax
from jax.experimental import pallas as pl
from jax.experimental.pallas import tpu as pltpu

D_MODEL = 1024
DEPTH = 2
GRID_W = 64
HEAD_DIM = 64
A_HEADS = 6
A_KV = 2
B_HEADS = 6
B_KV = 2
C_HEADS = 4
C_QK_DIM = 32
C_V_DIM = 2 * C_QK_DIM
WINDOW = 128
ROPE_THETA = 10000.0
D_FF = 2816
EPS = 1e-6
NEG = -1e30
LOG2E = math.log2(math.e)

LANES = 128
BF16_ROWS = 16
MXU_COLS = 256
VMEM_LIMIT = 56 * 1024 * 1024

GQA_COLS = A_HEADS * HEAD_DIM
MIX_C0 = 2 * GQA_COLS
Z_KA, Z_VA, Z_KB, Z_VB, Z_KC, Z_VC, CACHE_COLS = 0, 128, 256, 384, 512, 768, 1024
CACHE_PIECES = ((Z_KA, 128), (Z_VA, 128), (Z_KB, 128), (Z_VB, 128), (Z_KC, 256), (Z_VC, 256))
K_A, K_B, K_C, K_COLS = 0, 128, 256, 512
T_QA, T_VA, T_QC, T_VC, T_VB, T_QB, T_ROWS = 0, 384, 512, 768, 1024, 1152, 1536
PAIRED_HEADS = (0, 3, 1, 4, 2, 5)
FFN_CHUNK = MXU_COLS
HALO = BF16_ROWS
LATENT_TM = 512
CONTEXT_POST_TM = 512
ATTN_A_TQ, ATTN_B_TQ, ATTN_C_TQ, ATTN_TK = 512, 256, 512, 512
ATTN_B_TK = 256
MOD_COLS = 1536

F32 = jnp.float32
BF16 = jnp.bfloat16


def _cparams(sem):
    return pltpu.CompilerParams(dimension_semantics=sem, vmem_limit_bytes=VMEM_LIMIT)


def _const_spec(shape):
    nd = len(shape)
    return pl.BlockSpec(shape, lambda *_: (0,) * nd)


def _layer_spec(shape, layer):
    nd = len(shape)
    return pl.BlockSpec((1,) + tuple(shape), lambda *_: (layer,) + (0,) * nd)


def _mod_kernel(c_ref, w_ref, b_ref, o_ref):
    cond = c_ref[...]
    a = cond / (1.0 + jnp.exp(-cond))
    o_ref[0] = jnp.dot(a.astype(BF16), w_ref[0].astype(BF16), preferred_element_type=F32) + b_ref[0]


def _modulation(conds, w_ada, b_ada):
    nb = MOD_COLS
    n_out = w_ada.shape[-1]
    return pl.pallas_call(
        _mod_kernel,
        grid=(DEPTH, n_out // nb),
        in_specs=[
            pl.BlockSpec((8, D_MODEL), lambda l, j: (0, 0)),
            pl.BlockSpec((1, D_MODEL, nb), lambda l, j: (l, 0, j)),
            pl.BlockSpec((1, 1, nb), lambda l, j: (l, 0, j)),
        ],
        out_specs=pl.BlockSpec((1, 8, nb), lambda l, j: (l, 0, j)),
        out_shape=jax.ShapeDtypeStruct((DEPTH, 8, n_out), F32),
        compiler_params=_cparams(("arbitrary", "arbitrary")),
        name="modulation",
    )(conds, w_ada, b_ada.reshape(DEPTH, 1, n_out))


def _rms(x, g):
    ms = jnp.mean(x * x, axis=-1, keepdims=True)
    return x * lax.rsqrt(ms + EPS) * g


def _head_rms(x, g, lo):
    ss = x * x
    s_lo = jnp.sum(jnp.where(lo, ss, 0.0), axis=-1, keepdims=True)
    s_hi = jnp.sum(jnp.where(lo, 0.0, ss), axis=-1, keepdims=True)
    inv = jnp.where(lo, lax.rsqrt(s_lo * (1.0 / HEAD_DIM) + EPS), lax.rsqrt(s_hi * (1.0 / HEAD_DIM) + EPS))
    return x * inv * g


def _rope(x, cos, sin, chunk, first):
    sw = jnp.where(first, pltpu.roll(x, LANES - chunk, 1), pltpu.roll(x, chunk, 1))
    return x * cos + sw * sin


def _softmax_init_t(m_ref, acc_ref):
    m_ref[...] = jnp.full(m_ref.shape, NEG, F32)
    acc_ref[...] = jnp.zeros(acc_ref.shape, F32)


def _scores_t(k, q_scr, s_ref, mx_ref, cs, allowed):
    s = jnp.dot(k, q_scr[:, cs], preferred_element_type=F32)
    if allowed is not None:
        s = jnp.where(allowed(cs), s, NEG)
    s_ref[0:k.shape[0], cs] = s
    mx_ref[:, cs] = jnp.max(s, axis=0, keepdims=True)


def _pv_and_sum(vt, p):
    ones = jnp.ones((BF16_ROWS, vt.shape[1]), BF16)
    return jnp.dot(jnp.concatenate([vt, ones], axis=0), p, preferred_element_type=F32)


def _softmax_result_t(acc_ref):
    return acc_ref[0:LANES] * (1.0 / acc_ref[LANES:LANES + 1])


def _attend_blocks_t(q_scr, blocks, s_bufs, x_bufs, m_ref, acc_ref):
    n = q_scr.shape[1]
    groups = [slice(c, c + MXU_COLS) for c in range(0, n, MXU_COLS)]

    def step(par, cur, nxt):
        vt = cur[1]() if cur else None
        k = nxt[0]() if nxt else None
        for cs in groups:
            if cur:
                s = s_bufs[par][0:vt.shape[1], cs]
                m_prev = m_ref[:, cs]
                m_new = jnp.maximum(m_prev, x_bufs[par][:, cs])
                m_ref[:, cs] = m_new
                pv = _pv_and_sum(vt, jnp.exp2(s - m_new).astype(BF16))
                acc_ref[:, cs] = jnp.exp2(m_prev - m_new) * acc_ref[:, cs] + pv
            if nxt:
                _scores_t(k, q_scr, s_bufs[1 - par], x_bufs[1 - par], cs, nxt[2])

    step(1, None, blocks[0])
    for t, blk in enumerate(blocks):
        step(t % 2, blk, blocks[t + 1] if t + 1 < len(blocks) else None)


def _ctx_then_latent_blocks(kc_ref, vct_ref, k_ref, vt_ref, tk):
    blocks = [(lambda: kc_ref[0], lambda: vct_ref[0], None)]
    for j in range(k_ref.shape[0] // tk):
        blocks.append((lambda j=j: k_ref[j * tk:(j + 1) * tk, :], lambda j=j: vt_ref[:, j * tk:(j + 1) * tk], None))
    return blocks


def _diff_lambda(lam_ref, lam_init):
    f = lambda a, b: jnp.exp(jnp.sum(a * b, axis=-1, keepdims=True))
    return f(lam_ref[0:1], lam_ref[1:2]) - f(lam_ref[2:3], lam_ref[3:4]) + lam_init


def _swap_row_chunks(x, chunk):
    parts = []
    for r in range(0, x.shape[0], 2 * chunk):
        parts += [x[r + chunk:r + 2 * chunk], x[r:r + chunk]]
    return jnp.concatenate(parts, axis=0)


def _in_proj_kernel(*refs, use_rope, emit_cache):
    it = iter(refs)
    x_ref, mod_ref, gn_ref, wt_ref, wk_ref, gq_ref, gk_ref = (next(it) for _ in range(7))
    cos64 = sin64 = cos32 = sin32 = cos64t = sin64t = cos32t = sin32t = None
    if use_rope:
        cos64, sin64, cos32, sin32, cos64t, sin64t, cos32t, sin32t = (next(it)[...] for _ in range(8))
    prev_ref = next(it) if emit_cache == "final" else None
    zk_ref = next(it)
    cache_ref = next(it) if emit_cache == "rows" else None
    final_refs = [next(it) for _ in CACHE_PIECES] if emit_cache == "final" else None
    zt_ref = next(it)
    if final_refs:
        for ref, (start, width) in zip(final_refs, CACHE_PIECES):
            ref[0, 0] = prev_ref[:, start:start + width]

    tm = x_ref.shape[0]
    sub = min(tm, MXU_COLS)
    mod = mod_ref[0]
    lane = lax.broadcasted_iota(jnp.int32, (sub, LANES), 1)
    lo = lane < HEAD_DIM
    first16 = (lane & 31) < 16
    first8 = (lane & 15) < 8
    q_scale = HEAD_DIM ** -0.5 * LOG2E
    qc_scale = C_QK_DIM ** -0.5 * LOG2E

    def matmuls(r0):
        h = (_rms(x_ref[r0:r0 + sub], gn_ref[...]) * (1.0 + mod[1:2]) + mod[0:1]).astype(BF16)
        zk = jnp.dot(h, wk_ref[0], preferred_element_type=F32)
        zt = lax.dot_general(wt_ref[0], h, (((1,), (1,)), ((), ())), preferred_element_type=F32)
        return zk, zt

    def put_cache(r0, off, v):
        if cache_ref is not None:
            cache_ref[r0:r0 + sub, off:off + LANES] = v
        if final_refs:
            for ref, (start, width) in zip(final_refs, CACHE_PIECES):
                if start <= off < start + width:
                    ref[0, DEPTH - 1, r0:r0 + sub, off - start:off - start + LANES] = v

    def finish_keys(r0, zk):
        def tab(t):
            return t[r0:r0 + sub] if use_rope else None

        def rope64(v):
            return _rope(v, tab(cos64), tab(sin64), 16, first16) if use_rope else v

        def rope32(v):
            return _rope(v, tab(cos32), tab(sin32), 8, first8) if use_rope else v

        def put_k(off, v):
            zk_ref[r0:r0 + sub, off:off + LANES] = v.astype(BF16)

        ka = _head_rms(zk[:, K_A:K_A + LANES], gk_ref[...], lo)
        put_cache(r0, Z_KA, ka)
        put_k(K_A, rope64(ka))
        kb = zk[:, K_B:K_B + LANES]
        put_cache(r0, Z_KB, kb)
        put_k(K_B, rope64(kb))
        for s in range(2):
            kc = zk[:, K_C + LANES * s:K_C + LANES * (s + 1)]
            put_cache(r0, Z_KC + LANES * s, kc)
            put_k(K_C + LANES * s, rope32(kc))

    def finish_features(r0, zt):
        def t_slab(off):
            return zt[off:off + LANES]

        def put_t(off, v):
            zt_ref[off:off + LANES, r0:r0 + sub] = v.astype(BF16)

        def rope_t(v, cos_t, sin_t, chunk):
            if not use_rope:
                return v
            return v * cos_t[:, r0:r0 + sub] + _swap_row_chunks(v, chunk) * sin_t[:, r0:r0 + sub]

        def head_rms_t(v):
            ss = v * v
            halves = []
            for r in (0, HEAD_DIM):
                inv = lax.rsqrt(jnp.sum(ss[r:r + HEAD_DIM], axis=0, keepdims=True) * (1.0 / HEAD_DIM) + EPS)
                halves.append(v[r:r + HEAD_DIM] * inv)
            return jnp.concatenate(halves, axis=0) * gq_ref[:, 0:sub]

        for s in range(3):
            put_t(T_QA + LANES * s, rope_t(head_rms_t(t_slab(T_QA + LANES * s)), cos64t, sin64t, 16) * q_scale)
            put_t(T_QB + LANES * s, rope_t(t_slab(T_QB + LANES * s), cos64t, sin64t, 16) * q_scale)
        for s in range(2):
            put_t(T_QC + LANES * s, rope_t(t_slab(T_QC + LANES * s), cos32t, sin32t, 8) * qc_scale)
        for t_off, z_off in ((T_VA, Z_VA), (T_VB, Z_VB), (T_VC, Z_VC), (T_VC + LANES, Z_VC + LANES)):
            put_t(t_off, t_slab(t_off))
            if emit_cache:
                put_cache(r0, z_off, t_slab(t_off).T)

    starts = list(range(0, tm, sub))
    prods = [matmuls(r0) for r0 in starts]
    for r0, (zk, _) in zip(starts, prods):
        finish_keys(r0, zk)
    for r0, (_, zt) in zip(starts, prods):
        finish_features(r0, zt)


def _in_proj(x, mod_l, gn, w_t, w_k, gq_t, gk, rope_tabs, *, layer, tm, tiles_per_cond, tiles_per_seq, emit_cache,
             prev_cache=None):
    t = x.shape[0]
    use_rope = rope_tabs is not None
    assert (emit_cache == "final") == (prev_cache is not None) and DEPTH == 2
    in_specs = [
        pl.BlockSpec((tm, D_MODEL), lambda i: (i, 0)),
        pl.BlockSpec((1, 6, D_MODEL), lambda i: (i // tiles_per_cond, 0, 0)),
        _const_spec((1, D_MODEL)),
        _layer_spec((T_ROWS, D_MODEL), layer),
        _layer_spec((D_MODEL, K_COLS), layer),
        _const_spec((LANES, tm)),
        _const_spec((1, LANES)),
    ]
    args = [x, mod_l, gn, w_t, w_k, gq_t, gk]
    if use_rope:
        in_specs += [pl.BlockSpec((tm, LANES), lambda i: (i % tiles_per_seq, 0))] * 4
        in_specs += [pl.BlockSpec((LANES, tm), lambda i: (0, i % tiles_per_seq))] * 4
        args += list(rope_tabs)
    if prev_cache is not None:
        in_specs.append(pl.BlockSpec((tm, CACHE_COLS), lambda i: (i, 0)))
        args.append(prev_cache)
    out_shape = [jax.ShapeDtypeStruct((t, K_COLS), BF16)]
    out_specs = [pl.BlockSpec((tm, K_COLS), lambda i: (i, 0))]
    if emit_cache == "rows":
        out_shape.append(jax.ShapeDtypeStruct((t, CACHE_COLS), F32))
        out_specs.append(pl.BlockSpec((tm, CACHE_COLS), lambda i: (i, 0)))
    if emit_cache == "final":
        for _, width in CACHE_PIECES:
            out_shape.append(jax.ShapeDtypeStruct((t // tm, DEPTH, tm, width), F32))
            out_specs.append(pl.BlockSpec((1, DEPTH, tm, width), lambda i: (i, 0, 0, 0)))
    out_shape.append(jax.ShapeDtypeStruct((T_ROWS, t), BF16))
    out_specs.append(pl.BlockSpec((T_ROWS, tm), lambda i: (0, i)))
    return pl.pallas_call(
        functools.partial(_in_proj_kernel, use_rope=use_rope, emit_cache=emit_cache),
        grid=(t // tm,),
        in_specs=in_specs,
        out_specs=out_specs,
        out_shape=out_shape,
        compiler_params=_cparams(("arbitrary",)),
        name="in_proj",
    )(*args)


def _gqa_queries_t(qt_ref, lo):
    slabs = [qt_ref[LANES * s:LANES * (s + 1), :] for s in range(3)]
    zero = jnp.zeros_like(slabs[0])
    return jnp.concatenate([jnp.where(lo, s, zero) for s in slabs] + [jnp.where(lo, zero, s) for s in slabs], axis=1)


def _gqa_store_t(ot, o_ref, tq, lo, col0=0):
    for s in range(3):
        slab_t = jnp.where(lo, ot[:, s * tq:(s + 1) * tq], ot[:, (3 + s) * tq:(4 + s) * tq])
        o_ref[:, col0 + LANES * s:col0 + LANES * (s + 1)] = slab_t.T.astype(BF16)


def _diff_queries_t(qt, row):
    zero = jnp.zeros_like(qt)
    return jnp.concatenate([jnp.where((row >= C_QK_DIM * j) & (row < C_QK_DIM * (j + 1)), qt, zero)
                            for j in range(4)], axis=1)


def _diff_output_t(ot, tq, lam, gsub, lam_init, row):
    o_even = ot[:, 0:tq] - lam * ot[:, tq:2 * tq]
    o_odd = ot[:, 2 * tq:3 * tq] - lam * ot[:, 3 * tq:4 * tq]
    oc = jnp.where(row < C_V_DIM, o_even, o_odd).T
    lo = lax.broadcasted_iota(jnp.int32, (tq, LANES), 1) < C_V_DIM
    return (_head_rms(oc, gsub, lo) * (1.0 - lam_init)).astype(BF16)


def _attn_ctx_kernel(sink_ref, lam_ref, gsub_ref, z_ref, zt_ref, o_ref, *, lam_init):
    tq = z_ref.shape[0]
    row = lax.broadcasted_iota(jnp.int32, (LANES, tq), 0)
    lo = row < HEAD_DIM

    def scores(q, k, sink=None):
        s = jnp.dot(k, q, preferred_element_type=F32)
        m = jnp.max(s, axis=0, keepdims=True)
        return s, (m if sink is None else jnp.maximum(m, sink))

    def values(s, m, vt, sink=None):
        acc = _pv_and_sum(vt, jnp.exp2(s - m).astype(BF16))
        l = acc[LANES:LANES + 1]
        if sink is not None:
            l = l + jnp.exp2(sink - m)
        return acc[0:LANES] * (1.0 / l)

    def keys(off):
        return z_ref[:, off:off + LANES]

    def feat(off):
        return zt_ref[off:off + LANES, :]

    sink = jnp.concatenate([jnp.full((1, tq), sink_ref[h] * LOG2E, F32) for h in range(B_HEADS)], axis=1)
    lam = _diff_lambda(lam_ref, lam_init)

    def store_c(s, ot):
        c0 = MIX_C0 + LANES * s
        o_ref[:, c0:c0 + LANES] = _diff_output_t(ot, tq, lam, gsub_ref[...], lam_init, row)

    sa = scores(_gqa_queries_t(zt_ref.at[T_QA:T_QA + GQA_COLS], lo), keys(K_A))
    sb = scores(_gqa_queries_t(zt_ref.at[T_QB:T_QB + GQA_COLS], lo), keys(K_B), sink)
    _gqa_store_t(values(*sa, feat(T_VA)), o_ref, tq, lo)
    sc0 = scores(_diff_queries_t(feat(T_QC), row), keys(K_C))
    _gqa_store_t(values(*sb, feat(T_VB), sink), o_ref, tq, lo, col0=GQA_COLS)
    sc1 = scores(_diff_queries_t(feat(T_QC + LANES), row), keys(K_C + LANES))
    store_c(0, values(*sc0, feat(T_VC)))
    store_c(1, values(*sc1, feat(T_VC + LANES)))


def _attn_ctx(z, zt, sink, lam_par, gsub, *, seq, lam_init):
    t = z.shape[0]
    return pl.pallas_call(
        functools.partial(_attn_ctx_kernel, lam_init=lam_init),
        grid=(t // seq,),
        in_specs=[
            pl.BlockSpec(memory_space=pltpu.SMEM),
            _const_spec((8, LANES)),
            _const_spec((1, LANES)),
            pl.BlockSpec((seq, K_COLS), lambda b: (b, 0)),
            pl.BlockSpec((T_ROWS, seq), lambda b: (0, b)),
        ],
        out_specs=pl.BlockSpec((seq, D_MODEL), lambda b: (b, 0)),
        out_shape=jax.ShapeDtypeStruct((t, D_MODEL), BF16),
        compiler_params=_cparams(("arbitrary",)),
        name="attn_ctx",
    )(sink, lam_par, gsub, z, zt)


def _attn_a_kernel(qt_ref, kc_ref, vct_ref, k_ref, vt_ref, o_ref, q_scr, m_ref, acc_ref,
                   s0, s1, x0, x1, *, tk):
    tq = qt_ref.shape[1]
    lo = lax.broadcasted_iota(jnp.int32, (LANES, tq), 0) < HEAD_DIM
    refs = (m_ref, acc_ref)
    q_scr[...] = _gqa_queries_t(qt_ref, lo)
    _softmax_init_t(*refs)
    _attend_blocks_t(q_scr, _ctx_then_latent_blocks(kc_ref, vct_ref, k_ref, vt_ref, tk), (s0, s1), (x0, x1), *refs)
    _gqa_store_t(_softmax_result_t(acc_ref), o_ref, tq, lo)


def _keys_major_scratch(n, tk):
    return [pltpu.VMEM((LANES, n), BF16), pltpu.VMEM((1, n), F32),
            pltpu.VMEM((LANES + BF16_ROWS, n), F32),
            pltpu.VMEM((tk, n), F32), pltpu.VMEM((tk, n), F32),
            pltpu.VMEM((1, n), F32), pltpu.VMEM((1, n), F32)]


def _attn_a(z, zt, k_ctx, vt_ctx, *, seq, tq, tk):
    t = z.shape[0]
    nq = seq // tq
    n_ctx = k_ctx.shape[1]
    n = A_HEADS * tq
    return pl.pallas_call(
        functools.partial(_attn_a_kernel, tk=tk),
        grid=(t // seq, nq),
        in_specs=[
            pl.BlockSpec((GQA_COLS, tq), lambda b, i: (T_QA // GQA_COLS, b * nq + i)),
            pl.BlockSpec((1, n_ctx, LANES), lambda b, i: (b, 0, 0)),
            pl.BlockSpec((1, LANES, n_ctx), lambda b, i: (b, 0, 0)),
            pl.BlockSpec((seq, LANES), lambda b, i: (b, K_A // LANES)),
            pl.BlockSpec((LANES, seq), lambda b, i: (T_VA // LANES, b)),
        ],
        out_specs=pl.BlockSpec((tq, GQA_COLS), lambda b, i: (b * nq + i, 0)),
        out_shape=jax.ShapeDtypeStruct((t, GQA_COLS), BF16),
        scratch_shapes=_keys_major_scratch(n, tk),
        compiler_params=_cparams(("arbitrary", "arbitrary")),
        name="attn_a",
    )(zt, k_ctx, vt_ctx, z, zt)


def _attn_b_kernel(sink_ref, qt_ref, kc_ref, vct_ref, k_ref, vt_ref, o_ref, q_scr, m_ref, acc_ref, s0, s1, x0, x1,
                   *, tk):
    tq = qt_ref.shape[1]
    seq = k_ref.shape[0]
    n = B_HEADS * tq
    band = tq + 2 * WINDOW
    i = pl.program_id(1)
    lo = lax.broadcasted_iota(jnp.int32, (LANES, tq), 0) < HEAD_DIM
    q_scr[...] = _gqa_queries_t(qt_ref, lo)
    m_ref[...] = jnp.concatenate([jnp.full((1, tq), sink_ref[h] * LOG2E, F32) for h in range(B_HEADS)], axis=1)
    acc_ref[...] = jnp.zeros(acc_ref.shape, F32)
    acc_ref[LANES:LANES + 1] = jnp.ones((1, n), F32)
    start = jnp.clip(i * tq - WINDOW, 0, seq - band)

    masks = {}

    def allowed(j, cs):
        q_off = cs.start % tq
        if (j, q_off) not in masks:
            kpos = start + j * tk + lax.broadcasted_iota(jnp.int32, (tk, MXU_COLS), 0)
            qpos = i * tq + q_off + lax.broadcasted_iota(jnp.int32, (tk, MXU_COLS), 1)
            masks[(j, q_off)] = jnp.abs(kpos - qpos) <= WINDOW
        return masks[(j, q_off)]

    def band_rows(j):
        return pl.ds(pl.multiple_of(start + j * tk, LANES), tk)

    blocks = [(lambda: kc_ref[0], lambda: vct_ref[0], None)]
    for j in range(band // tk):
        blocks.append((lambda j=j: k_ref[band_rows(j), :], lambda j=j: vt_ref[:, band_rows(j)],
                       functools.partial(allowed, j)))
    _attend_blocks_t(q_scr, blocks, (s0, s1), (x0, x1), m_ref, acc_ref)
    _gqa_store_t(_softmax_result_t(acc_ref), o_ref, tq, lo)


def _attn_b(z, zt, k_ctx, vt_ctx, sink, *, seq, tq, tk):
    t = z.shape[0]
    nq = seq // tq
    n_ctx = k_ctx.shape[1]
    assert (tq + 2 * WINDOW) % tk == 0 and MXU_COLS <= tq and n_ctx <= tk
    return pl.pallas_call(
        functools.partial(_attn_b_kernel, tk=tk),
        grid=(t // seq, nq),
        in_specs=[
            pl.BlockSpec(memory_space=pltpu.SMEM),
            pl.BlockSpec((GQA_COLS, tq), lambda b, i: (T_QB // GQA_COLS, b * nq + i)),
            pl.BlockSpec((1, n_ctx, LANES), lambda b, i: (b, 0, 0)),
            pl.BlockSpec((1, LANES, n_ctx), lambda b, i: (b, 0, 0)),
            pl.BlockSpec((seq, LANES), lambda b, i: (b, K_B // LANES)),
            pl.BlockSpec((LANES, seq), lambda b, i: (T_VB // LANES, b)),
        ],
        out_specs=pl.BlockSpec((tq, GQA_COLS), lambda b, i: (b * nq + i, 0)),
        out_shape=jax.ShapeDtypeStruct((t, GQA_COLS), BF16),
        scratch_shapes=_keys_major_scratch(B_HEADS * tq, tk),
        compiler_params=_cparams(("arbitrary", "arbitrary")),
        name="attn_b",
    )(sink, zt, k_ctx, vt_ctx, z, zt)


def _attn_c_kernel(lam_ref, gsub_ref, qt_ref, kc_ref, vct_ref, k_ref, vt_ref, o_ref,
                   q_scr, m_ref, acc_ref, s0, s1, x0, x1, *, tk, lam_init):
    tq = qt_ref.shape[1]
    row = lax.broadcasted_iota(jnp.int32, (LANES, tq), 0)
    refs = (m_ref, acc_ref)
    q_scr[...] = _diff_queries_t(qt_ref[...], row)
    _softmax_init_t(*refs)
    _attend_blocks_t(q_scr, _ctx_then_latent_blocks(kc_ref, vct_ref, k_ref, vt_ref, tk), (s0, s1), (x0, x1), *refs)
    lam = _diff_lambda(lam_ref, lam_init)
    o_ref[...] = _diff_output_t(_softmax_result_t(acc_ref), tq, lam, gsub_ref[...], lam_init, row)


def _attn_c(z, zt, k_ctx, vt_ctx, lam_par, gsub, *, seq, tq, tk, lam_init):
    t = z.shape[0]
    nq = seq // tq
    n_ctx = k_ctx.shape[1]
    n = 4 * tq
    return pl.pallas_call(
        functools.partial(_attn_c_kernel, tk=tk, lam_init=lam_init),
        grid=(t // seq, 2, nq),
        in_specs=[
            _const_spec((8, LANES)),
            _const_spec((1, LANES)),
            pl.BlockSpec((LANES, tq), lambda b, s, i: (T_QC // LANES + s, b * nq + i)),
            pl.BlockSpec((1, n_ctx, LANES), lambda b, s, i: (b, 0, s)),
            pl.BlockSpec((1, LANES, n_ctx), lambda b, s, i: (b, s, 0)),
            pl.BlockSpec((seq, LANES), lambda b, s, i: (b, K_C // LANES + s)),
            pl.BlockSpec((LANES, seq), lambda b, s, i: (T_VC // LANES + s, b)),
        ],
        out_specs=pl.BlockSpec((tq, LANES), lambda b, s, i: (b * nq + i, s)),
        out_shape=jax.ShapeDtypeStruct((t, C_HEADS * C_V_DIM), BF16),
        scratch_shapes=_keys_major_scratch(n, tk),
        compiler_params=_cparams(("arbitrary", "arbitrary", "arbitrary")),
        name="attn_c",
    )(lam_par, gsub, zt, k_ctx, vt_ctx, z, zt)


def _post_kernel(*refs, n_parts, halo, tiles_per_seq, seq_len, final):
    it = iter(refs)
    tiles = [next(it) for _ in range(1 + n_parts)]
    halos = [(next(it), next(it)) for _ in range(1 + n_parts)] if halo else None
    (mod_ref, gn_ref, wo_ref, wu_ref, cw_ref, cb_ref, wd_ref, gf_ref, out_ref, act_scr) = (next(it) for _ in range(10))

    def rows_of(j):
        if halo:
            return jnp.concatenate([halos[j][0][...], tiles[j][...], halos[j][1][...]], axis=0)
        return tiles[j][...]

    tm = tiles[0].shape[0]
    x = rows_of(0)
    o = jnp.concatenate([rows_of(j) for j in range(1, 1 + n_parts)], axis=1)
    ext = x.shape[0]
    mod = mod_ref[0]
    cut = ext // 2 // BF16_ROWS * BF16_ROWS
    halves = [(0, cut), (cut, ext)]
    proj = [jnp.dot(o[a:b], wo_ref[0], preferred_element_type=F32) for a, b in halves]
    x1 = jnp.concatenate([x[a:b] + mod[2:3] * pr for (a, b), pr in zip(halves, proj)], axis=0)
    h = jnp.concatenate([_rms(x1[a:b], gn_ref[...]) * (1.0 + mod[4:5]) + mod[3:4] for a, b in halves], axis=0)
    row = lax.broadcasted_iota(jnp.int32, (ext, 1), 0)
    if halo:
        t_in_seq = pl.program_id(0) % tiles_per_seq
        keep = ((row >= halo) | (t_in_seq > 0)) & ((row < halo + tm) | (t_in_seq < tiles_per_seq - 1))
        h = jnp.where(keep, h, 0.0)
    h = h.astype(BF16)

    def conv(u, c0):
        cw = cw_ref[:, c0:c0 + FFN_CHUNK]
        up = pltpu.roll(u, 1, 0)
        dn = pltpu.roll(u, ext - 1, 0)
        if not halo:
            up = jnp.where(row % seq_len == 0, 0.0, up)
            dn = jnp.where(row % seq_len == seq_len - 1, 0.0, dn)
        v = cw[0:1] * up + cw[1:2] * u + cw[2:3] * dn + cb_ref[:, c0:c0 + FFN_CHUNK]
        return v[halo:halo + tm]

    def up_proj(c):
        ca, cg = c * FFN_CHUNK, D_FF + c * FFN_CHUNK
        return (jnp.dot(h, wu_ref[0, :, ca:ca + FFN_CHUNK], preferred_element_type=F32),
                jnp.dot(h, wu_ref[0, :, cg:cg + FFN_CHUNK], preferred_element_type=F32))

    n_chunks = D_FF // FFN_CHUNK
    nxt = up_proj(0)
    for c in range(n_chunks):
        ua, ug = nxt
        if c + 1 < n_chunks:
            nxt = up_proj(c + 1)
        ca = c * FFN_CHUNK
        a = conv(ua, ca)
        g = conv(ug, D_FF + ca)
        act_scr[:, ca:ca + FFN_CHUNK] = (a / (1.0 + jnp.exp(-a)) * g).astype(BF16)
    x2 = x1[halo:halo + tm] + mod[5:6] * jnp.dot(act_scr[...], wd_ref[0], preferred_element_type=F32)
    if final:
        x2 = _rms(x2, gf_ref[...])
    out_ref[...] = x2


def _post(x, o_parts, mod_l, gn2, wo, wu, cw, cb, wd, gf, *, layer, tm, tiles_per_cond, seq_len):
    t = x.shape[0]
    final = layer == DEPTH - 1
    assert sum(o.shape[1] for o in o_parts) == D_MODEL and (seq_len % tm == 0 or tm % seq_len == 0)
    tiles_per_seq = max(seq_len // tm, 1)
    halo = HALO if tiles_per_seq > 1 else 0
    tile = lambda i: (i, 0)
    rows = [x] + list(o_parts)
    in_specs = [pl.BlockSpec((tm, a.shape[1]), tile) for a in rows]
    args = list(rows)
    if halo:
        per = tm // halo
        prev = lambda i: (jnp.maximum(i * per - 1, 0), 0)
        nxt = lambda i: (jnp.minimum((i + 1) * per, t // halo - 1), 0)
        for a in rows:
            in_specs += [pl.BlockSpec((halo, a.shape[1]), prev), pl.BlockSpec((halo, a.shape[1]), nxt)]
            args += [a, a]
    in_specs += [
        pl.BlockSpec((1, 6, D_MODEL), lambda i: (i // tiles_per_cond, 0, 0)),
        _const_spec((1, D_MODEL)),
        _layer_spec((D_MODEL, D_MODEL), layer),
        _layer_spec((D_MODEL, 2 * D_FF), layer),
        _const_spec((3, 2 * D_FF)),
        _const_spec((1, 2 * D_FF)),
        _layer_spec((D_FF, D_MODEL), layer),
        _const_spec((1, D_MODEL)),
    ]
    args += [mod_l, gn2, wo, wu, cw, cb, wd, gf]
    return pl.pallas_call(
        functools.partial(_post_kernel, n_parts=len(o_parts), halo=halo, tiles_per_seq=tiles_per_seq,
                          seq_len=seq_len, final=final),
        grid=(t // tm,),
        in_specs=in_specs,
        out_specs=pl.BlockSpec((tm, D_MODEL), tile),
        out_shape=jax.ShapeDtypeStruct((t, D_MODEL), F32),
        scratch_shapes=[pltpu.VMEM((tm, D_FF), BF16)],
        compiler_params=_cparams(("arbitrary",)),
        name="post",
    )(*args)


def _rope_tables(seq):
    t = np.arange(seq)
    rows = (t // GRID_W).astype(np.float32)[:, None]
    cols = (t % GRID_W).astype(np.float32)[:, None]

    def tab(half, reps):
        inv = np.float32(ROPE_THETA) ** (-np.arange(half, dtype=np.float32) / np.float32(half))
        ar, ac = rows * inv[None, :], cols * inv[None, :]
        cos = np.concatenate([np.cos(ar), np.cos(ar), np.cos(ac), np.cos(ac)], axis=-1)
        sin = np.concatenate([-np.sin(ar), np.sin(ar), -np.sin(ac), np.sin(ac)], axis=-1)
        return np.tile(cos, (1, reps)).astype(np.float32), np.tile(sin, (1, reps)).astype(np.float32)

    tabs = tab(HEAD_DIM // 4, LANES // HEAD_DIM) + tab(C_QK_DIM // 4, LANES // C_QK_DIM)
    tabs = tabs + tuple(np.ascontiguousarray(t.T) for t in tabs)
    return tuple(jnp.asarray(t) for t in tabs)


def _w_in_perms():
    sizes = (A_HEADS * HEAD_DIM, A_KV * HEAD_DIM, A_KV * HEAD_DIM, B_HEADS * HEAD_DIM, B_KV * HEAD_DIM,
             B_KV * HEAD_DIM, C_HEADS * 2 * C_QK_DIM, C_HEADS * 2 * C_QK_DIM, C_HEADS * C_V_DIM)
    offs = np.concatenate([[0], np.cumsum(sizes)])
    qa, ka, va, qb, kb, vb, qc, kc, vc = (np.arange(offs[j], offs[j + 1]) for j in range(9))
    pair = np.concatenate([np.arange(h * HEAD_DIM, (h + 1) * HEAD_DIM) for h in PAIRED_HEADS])
    return np.concatenate([qa[pair], va, qc, vc, vb, qb[pair]]), np.concatenate([ka, kb, kc])


def _w_out_perm():
    pair = np.concatenate([np.arange(h * HEAD_DIM, (h + 1) * HEAD_DIM) for h in PAIRED_HEADS])
    return np.concatenate([pair, GQA_COLS + pair, np.arange(MIX_C0, D_MODEL)])


def _take_runs(w, perm, axis):
    cuts = [0] + [j for j in range(1, len(perm)) if perm[j] != perm[j - 1] + 1] + [len(perm)]
    parts = [lax.slice_in_dim(w, int(perm[a]), int(perm[b - 1]) + 1, axis=axis) for a, b in zip(cuts[:-1], cuts[1:])]
    return jnp.concatenate(parts, axis=axis)


def kernel(x_prompt, x_sample, cache_a_k, cache_a_v, cache_b_k, cache_b_v, cache_c_k, cache_c_v, c, c_ctx, w_ada, b_ada, g_norm1, g_norm2, w_in, g_qa, g_ka, sink_b, lam_q1, lam_k1, lam_q2, lam_k2, g_subln, w_out, w_up, conv_w, conv_b, w_down, g_final):
    n_ctx_req, ctx_len, _ = x_prompt.shape
    n_lat_req, lat_len, _ = x_sample.shape
    past = cache_a_k.shape[2]

    conds = jnp.zeros((8, D_MODEL), F32).at[0].set(c_ctx).at[1:1 + n_lat_req].set(c)
    mod = _modulation(conds, w_ada, b_ada).reshape(DEPTH, 8, 6, D_MODEL)

    perm_t, perm_k = _w_in_perms()
    w_t = jnp.swapaxes(_take_runs(w_in, perm_t, 2), 1, 2).astype(BF16)
    w_k = _take_runs(w_in, perm_k, 2).astype(BF16)
    w_out_p = _take_runs(w_out, _w_out_perm(), 1).astype(BF16)
    w_up_b = w_up.astype(BF16)
    w_down_b = w_down.astype(BF16)
    rope_tabs = _rope_tables(lat_len)
    zpad = jnp.zeros((DEPTH, LANES - C_QK_DIM), F32)
    lam_rows = [jnp.concatenate([v, zpad], axis=-1) for v in (lam_q1, lam_k1, lam_q2, lam_k2)]
    lam_par = jnp.concatenate([jnp.stack(lam_rows, axis=1), jnp.zeros((DEPTH, 4, LANES), F32)], axis=1)
    gf = g_final.reshape(1, D_MODEL)

    xc = x_prompt.reshape(n_ctx_req * ctx_len, D_MODEL)
    xs = x_sample.reshape(n_lat_req * lat_len, D_MODEL)
    lat_tm = LATENT_TM
    caches = []
    for l in range(DEPTH):
        lam_init = 0.8 - 0.6 * math.exp(-0.3 * l)
        gn1 = g_norm1[l].reshape(1, D_MODEL)
        gn2 = g_norm2[l].reshape(1, D_MODEL)
        gq_col = jnp.tile(g_qa[l], LANES // HEAD_DIM)[:, None]
        gk = jnp.tile(g_ka[l], LANES // HEAD_DIM).reshape(1, LANES)
        gsub = jnp.tile(g_subln[l], LANES // C_V_DIM).reshape(1, LANES)
        post_w = (gn2, w_out_p, w_up_b, conv_w[l], conv_b[l].reshape(1, 2 * D_FF), w_down_b, gf)
        final = l == DEPTH - 1

        n_tiles = n_ctx_req
        zc, *caches, ztc = _in_proj(xc, mod[l, 0:1], gn1, w_t, w_k, jnp.broadcast_to(gq_col, (LANES, ctx_len)), gk,
                                    None, layer=l, tm=ctx_len, tiles_per_cond=n_tiles, tiles_per_seq=1,
                                    emit_cache="final" if final else "rows",
                                    prev_cache=caches[0] if final else None)
        oc = _attn_ctx(zc, ztc, sink_b[l], lam_par[l], gsub, seq=ctx_len, lam_init=lam_init)
        xc = _post(xc, [oc], mod[l, 0:1], *post_w, layer=l, tm=CONTEXT_POST_TM, tiles_per_cond=n_tiles,
                   seq_len=ctx_len)

        per_seq = lat_len // lat_tm
        zs, zts = _in_proj(xs, mod[l, 1:1 + n_lat_req], gn1, w_t, w_k, jnp.broadcast_to(gq_col, (LANES, lat_tm)), gk,
                           rope_tabs, layer=l, tm=lat_tm, tiles_per_cond=per_seq, tiles_per_seq=per_seq,
                           emit_cache=None)
        flat = lambda a: a[:, l].reshape(n_lat_req, past, -1).astype(BF16)
        flat_t = lambda a: jnp.swapaxes(flat(a), 1, 2)
        oa = _attn_a(zs, zts, flat(cache_a_k), flat_t(cache_a_v), seq=lat_len, tq=ATTN_A_TQ, tk=ATTN_TK)
        ob = _attn_b(zs, zts, flat(cache_b_k), flat_t(cache_b_v), sink_b[l], seq=lat_len, tq=ATTN_B_TQ,
                     tk=ATTN_B_TK)
        oc = _attn_c(zs, zts, flat(cache_c_k), flat_t(cache_c_v), lam_par[l], gsub, seq=lat_len, tq=ATTN_C_TQ,
                     tk=ATTN_TK, lam_init=lam_init)
        xs = _post(xs, [oa, ob, oc], mod[l, 1:1 + n_lat_req], *post_w, layer=l, tm=lat_tm, tiles_per_cond=per_seq,
                   seq_len=lat_len)

    heads = (A_KV, A_KV, B_KV, B_KV, C_HEADS, C_HEADS)
    new_caches = tuple(cch.reshape(n_ctx_req, DEPTH, ctx_len, h, -1) for cch, h in zip(caches, heads))
    return (xc.reshape(x_prompt.shape), xs.reshape(x_sample.shape)) + new_caches
```
